```python
import jax, jax.numpy as jnp
from jax import lax
import numpy as np

D_MODEL = 2048
BATCH = 8
SEQ = 2048
DEPTH = 1

HEAD_DIM = 128
HEADS_PER_GROUP = 8
DILATED_GROUPS = ((128, 1), (512, 4), (2048, 16))
N_GROUPS = len(DILATED_GROUPS)
ATTN_WIDTH = N_GROUPS * HEADS_PER_GROUP * HEAD_DIM
ATTN_OUT_WIDTH = HEADS_PER_GROUP * HEAD_DIM
CONV_WIDTH = D_MODEL
CONV_K = 3
IN_COLS = 3 * ATTN_WIDTH + 3 * CONV_WIDTH + 2 * D_MODEL
FFN_HIDDEN = -(-(8 * D_MODEL) // (3 * 256)) * 256
ALPHA = (2 * DEPTH) ** 0.25
BETA = (8 * DEPTH) ** -0.25
LN_EPS = 1e-5

kernel_name = "hybrid_dilated_attn_shortconv_gated_deepnorm"


def layer_norm(x, g, b):
    xf = x.astype(jnp.float32)
    mu = xf.mean(-1, keepdims=True)
    var = jnp.square(xf - mu).mean(-1, keepdims=True)
    y = (xf - mu) * lax.rsqrt(var + LN_EPS) * g.astype(jnp.float32) + b.astype(jnp.float32)
    return y.astype(x.dtype)


def dilated_band_attention(q, k, v, window, dilation):
    b, s, h, dh = q.shape
    band = window // dilation
    length = s // dilation
    n_blk = -(-length // band)
    pad = n_blk * band - length

    def to_blocks(t):
        t = t.reshape(b, length, dilation, h, dh).transpose(0, 2, 3, 1, 4)
        t = jnp.pad(t, ((0, 0), (0, 0), (0, 0), (0, pad), (0, 0)))
        return t.reshape(b, dilation, h, n_blk, band, dh)

    def with_prev_block(t):
        prev = jnp.pad(t, ((0, 0), (0, 0), (0, 0), (1, 0), (0, 0), (0, 0)))[:, :, :, :-1]
        return jnp.concatenate([prev, t], axis=4)

    qb = to_blocks(q)
    kb = with_prev_block(to_blocks(k))
    vb = with_prev_block(to_blocks(v))

    scores = jnp.einsum('brhnqd,brhnkd->brhnqk', qb, kb).astype(jnp.float32) * (dh ** -0.5)
    qi = jnp.arange(band)[:, None]
    kj = jnp.arange(2 * band)[None, :]
    dist = band + qi - kj
    in_band = (dist >= 0) & (dist <= band)
    first_blk = (jnp.arange(n_blk) == 0)[:, None, None]
    valid = in_band[None] & ~(first_blk & (kj[None] < band))
    scores = jnp.where(valid, scores, -jnp.inf)
    m = scores.max(-1, keepdims=True)
    p = jnp.exp(scores - m)
    denom = p.sum(-1)
    o = jnp.einsum('brhnqk,brhnkd->brhnqd', p, vb.astype(jnp.float32)) / denom[..., None]
    lse = m[..., 0] + jnp.log(denom)

    o = o.reshape(b, dilation, h, n_blk * band, dh)[:, :, :, :length]
    o = o.transpose(0, 3, 1, 2, 4).reshape(b, s, h, dh)
    lse = lse.reshape(b, dilation, h, n_blk * band)[:, :, :, :length]
    lse = lse.transpose(0, 3, 1, 2).reshape(b, s, h)
    return o, lse


def causal_short_conv(z, w):
    s = z.shape[1]
    zp = jnp.pad(z, ((0, 0), (CONV_K - 1, 0), (0, 0)))
    y = w[0] * z
    for tap in range(1, CONV_K):
        y = y + w[tap] * zp[:, CONV_K - 1 - tap: CONV_K - 1 - tap + s]
    return y


def setup_inputs(seed: int = 0) -> dict:
    key = jax.random.key(seed)
    ks = jax.random.split(key, 16)
    f32 = jnp.float32
    d = D_MODEL

    def nrm(k, shape, scale):
        return jax.random.normal(k, shape, f32) * scale

    x = jax.random.normal(ks[0], (BATCH, SEQ, d), f32)
    w_qk = nrm(ks[1], (DEPTH, d, 2 * ATTN_WIDTH), d ** -0.5)
    w_v = nrm(ks[2], (DEPTH, d, ATTN_WIDTH), BETA * d ** -0.5)
    w_conv_in = nrm(ks[3], (DEPTH, d, 3 * CONV_WIDTH), d ** -0.5)
    w_gates = nrm(ks[4], (DEPTH, d, 2 * d), d ** -0.5)
    w_in = jnp.concatenate([w_qk, w_v, w_conv_in, w_gates], axis=-1)
    conv_w = nrm(ks[5], (DEPTH, CONV_K, CONV_WIDTH), CONV_K ** -0.5)
    w_attn_o = nrm(ks[6], (DEPTH, ATTN_OUT_WIDTH, d), BETA * ATTN_OUT_WIDTH ** -0.5)
    w_conv_o = nrm(ks[7], (DEPTH, CONV_WIDTH, d), BETA * CONV_WIDTH ** -0.5)
    w_out = nrm(ks[8], (DEPTH, d, d), BETA * d ** -0.5)
    ln1_g = 1.0 + nrm(ks[9], (DEPTH, d), 0.02)
    ln1_b = nrm(ks[10], (DEPTH, d), 0.02)
    w_ffn_gate = nrm(ks[11], (DEPTH, d, FFN_HIDDEN), d ** -0.5)
    w_ffn_up = nrm(ks[12], (DEPTH, d, FFN_HIDDEN), BETA * d ** -0.5)
    w_ffn_down = nrm(ks[13], (DEPTH, FFN_HIDDEN, d), BETA * FFN_HIDDEN ** -0.5)
    ln2_g = 1.0 + nrm(ks[14], (DEPTH, d), 0.02)
    ln2_b = nrm(ks[15], (DEPTH, d), 0.02)
    return {"x": x, "w_in": w_in, "conv_w": conv_w, "w_attn_o": w_attn_o,
            "w_conv_o": w_conv_o, "w_out": w_out, "ln1_g": ln1_g, "ln1_b": ln1_b,
            "w_ffn_gate": w_ffn_gate, "w_ffn_up": w_ffn_up, "w_ffn_down": w_ffn_down,
            "ln2_g": ln2_g, "ln2_b": ln2_b}


def reference(x, w_in, conv_w, w_attn_o, w_conv_o, w_out, ln1_g, ln1_b,
              w_ffn_gate, w_ffn_up, w_ffn_down, ln2_g, ln2_b):
    b, s, _ = x.shape
    cols = (ATTN_WIDTH,) * 3 + (CONV_WIDTH,) * 3 + (D_MODEL,) * 2
    split_points = [int(c) for c in np.cumsum(cols)[:-1]]
    for layer in range(DEPTH):
        proj = x @ w_in[layer]
        q, k, v, u, c_gate, b_gate, g_attn, g_conv = jnp.split(proj, split_points, axis=-1)

        q = q.reshape(b, s, N_GROUPS, HEADS_PER_GROUP, HEAD_DIM)
        k = k.reshape(b, s, N_GROUPS, HEADS_PER_GROUP, HEAD_DIM)
        v = v.reshape(b, s, N_GROUPS, HEADS_PER_GROUP, HEAD_DIM)
        outs, lses = [], []
        for g, (window, dilation) in enumerate(DILATED_GROUPS):
            o_g, lse_g = dilated_band_attention(q[:, :, g], k[:, :, g], v[:, :, g], window, dilation)
            outs.append(o_g)
            lses.append(lse_g)
        mix_w = jax.nn.softmax(jnp.stack(lses, axis=0), axis=0)
        attn = jnp.sum(mix_w[..., None] * jnp.stack(outs, axis=0), axis=0)
        attn = attn.reshape(b, s, ATTN_OUT_WIDTH).astype(x.dtype) @ w_attn_o[layer]

        conv = (b_gate * causal_short_conv(c_gate * u, conv_w[layer])) @ w_conv_o[layer]

        merged = (jax.nn.sigmoid(g_attn) * attn + jax.nn.sigmoid(g_conv) * conv) @ w_out[layer]
        x = layer_norm(ALPHA * x + merged, ln1_g[layer], ln1_b[layer])

        hidden = jax.nn.silu(x @ w_ffn_gate[layer]) * (x @ w_ffn_up[layer])
        x = layer_norm(ALPHA * x + hidden @ w_ffn_down[layer], ln2_g[layer], ln2_b[layer])
    return x
```

```python
import functools
import math

import jax
import jax.numpy as jnp
from jax import lax
from jax.experimental import pallas as pl
from jax.experimental.pallas import tpu as pltpu

D_MODEL = 2048
HEAD_DIM = 128
HEADS_PER_GROUP = 8
DILATED_GROUPS = ((128, 1), (512, 4), (2048, 16))
N_GROUPS = len(DILATED_GROUPS)
GROUP_WIDTH = HEADS_PER_GROUP * HEAD_DIM
ATTN_WIDTH = N_GROUPS * GROUP_WIDTH
CONV_WIDTH = D_MODEL
CONV_K = 3
FFN_HIDDEN = 5632
DEPTH = 1
ALPHA = (2 * DEPTH) ** 0.25
LN_EPS = 1e-5

COL_Q, COL_K, COL_V = 0, ATTN_WIDTH, 2 * ATTN_WIDTH
COL_U = 3 * ATTN_WIDTH
COL_C = COL_U + CONV_WIDTH
COL_B = COL_C + CONV_WIDTH
COL_GA = COL_B + CONV_WIDTH
COL_GC = COL_GA + D_MODEL
IN_COLS = COL_GC + D_MODEL

V7X_VMEM_BYTES = 64 * 1024 * 1024
ATTN_BLOCK = 128

F32 = jnp.float32
BF16 = jnp.bfloat16


def _vmem_limit(pipelined_block_bytes, resident_bytes):
    need = 2 * pipelined_block_bytes + resident_bytes
    return min(int(need * 1.25), V7X_VMEM_BYTES - 4 * 1024 * 1024)


def _nbytes(shape, dtype):
    return math.prod(shape) * jnp.dtype(dtype).itemsize


def _matmul_kernel(x_ref, w_ref, o_ref):
    o_ref[...] = jnp.dot(x_ref[...], w_ref[...], preferred_element_type=F32).astype(o_ref.dtype)


def _matmul(x, w, *, tm, tn, out_dtype):
    m, k = x.shape
    _, n = w.shape
    blocks = _nbytes((tm, k), x.dtype) + _nbytes((k, tn), w.dtype) + _nbytes((tm, tn), out_dtype)
    return pl.pallas_call(
        _matmul_kernel,
        grid=(m // tm, n // tn),
        in_specs=[pl.BlockSpec((tm, k), lambda i, j: (i, 0)),
                  pl.BlockSpec((k, tn), lambda i, j: (0, j))],
        out_specs=pl.BlockSpec((tm, tn), lambda i, j: (i, j)),
        out_shape=jax.ShapeDtypeStruct((m, n), out_dtype),
        compiler_params=pltpu.CompilerParams(
            dimension_semantics=("parallel", "arbitrary"),
            vmem_limit_bytes=_vmem_limit(blocks, _nbytes((tm, tn), F32))),
        name="in_proj",
    )(x, w)


def _attn_kernel(q1_ref, k1_ref, v1_ref, q2_ref, k2_ref, v2_ref, q3_ref, k3_ref, v3_ref, o_ref):
    blk = ATTN_BLOCK
    n_blk = q1_ref.shape[0] // blk
    scale = HEAD_DIM ** -0.5
    qi = lax.broadcasted_iota(jnp.int32, (blk, blk), 0)
    kj = lax.broadcasted_iota(jnp.int32, (blk, blk), 1)
    diff = qi - kj
    ge, le = diff >= 0, diff <= 0
    mod4 = (diff & 3) == 0
    mod16 = (diff & 15) == 0

    def scores(q, k_ref, j, mask):
        rows = pl.ds(pl.multiple_of(j * blk, blk), blk)
        s = lax.dot_general(q, k_ref[rows, :], (((1,), (1,)), ((), ())),
                            preferred_element_type=F32) * scale
        return jnp.where(mask, s, -jnp.inf)

    def first_tile(q, k_ref, v_ref, j, mask):
        rows = pl.ds(pl.multiple_of(j * blk, blk), blk)
        s = scores(q, k_ref, j, mask)
        m = s.max(-1, keepdims=True)
        p = jnp.exp(s - m)
        l = p.sum(-1, keepdims=True)
        acc = jnp.dot(p.astype(BF16), v_ref[rows, :], preferred_element_type=F32)
        return m, l, acc

    def next_tile(q, k_ref, v_ref, j, mask, carry):
        m, l, acc = carry
        rows = pl.ds(pl.multiple_of(j * blk, blk), blk)
        s = scores(q, k_ref, j, mask)
        m_new = jnp.maximum(m, s.max(-1, keepdims=True))
        a = jnp.exp(m - m_new)
        p = jnp.exp(s - m_new)
        l = a * l + p.sum(-1, keepdims=True)
        acc = a * acc + jnp.dot(p.astype(BF16), v_ref[rows, :], preferred_element_type=F32)
        return m_new, l, acc

    def merge(c1, c2):
        m1, l1, a1 = c1
        m2, l2, a2 = c2
        m = jnp.maximum(m1, m2)
        e1, e2 = jnp.exp(m1 - m), jnp.exp(m2 - m)
        return m, e1 * l1 + e2 * l2, e1 * a1 + e2 * a2

    def query_block(i, _):
        rows = pl.ds(pl.multiple_of(i * blk, blk), blk)

        q = q1_ref[rows, :]
        c1 = first_tile(q, k1_ref, v1_ref, i, ge)
        c1 = lax.fori_loop(jnp.maximum(i - 1, 0), i,
                           lambda j, c: next_tile(q, k1_ref, v1_ref, j, le, c), c1)

        q = q2_ref[rows, :]
        c2 = first_tile(q, k2_ref, v2_ref, i, ge & mod4)
        c2 = lax.fori_loop(jnp.maximum(i - 3, 0), i,
                           lambda j, c: next_tile(q, k2_ref, v2_ref, j, mod4, c), c2)
        far = i >= 4
        c2 = lax.fori_loop(jnp.where(far, i - 4, 0), jnp.where(far, i - 3, 0),
                           lambda j, c: next_tile(q, k2_ref, v2_ref, j, le & mod4, c), c2)

        q = q3_ref[rows, :]
        c3 = first_tile(q, k3_ref, v3_ref, i, ge & mod16)
        c3 = lax.fori_loop(0, i, lambda j, c: next_tile(q, k3_ref, v3_ref, j, mod16, c), c3)

        _, l, acc = merge(merge(c1, c2), c3)
        o_ref[rows, :] = (acc / l).astype(o_ref.dtype)
        return 0

    lax.fori_loop(0, n_blk, query_block, 0)


def _attention(proj, batch, seq):
    def col_spec(col0, g):
        base = (col0 + g * GROUP_WIDTH) // HEAD_DIM
        return pl.BlockSpec((seq, HEAD_DIM), lambda b, h: (b, base + h))

    in_specs = []
    for g in range(N_GROUPS):
        in_specs += [col_spec(COL_Q, g), col_spec(COL_K, g), col_spec(COL_V, g)]
    blocks = 10 * _nbytes((seq, HEAD_DIM), BF16)
    return pl.pallas_call(
        _attn_kernel,
        grid=(batch, HEADS_PER_GROUP),
        in_specs=in_specs,
        out_specs=pl.BlockSpec((seq, HEAD_DIM), lambda b, h: (b, h)),
        out_shape=jax.ShapeDtypeStruct((batch * seq, GROUP_WIDTH), BF16),
        compiler_params=pltpu.CompilerParams(
            dimension_semantics=("parallel", "parallel"),
            vmem_limit_bytes=_vmem_limit(blocks, 8 * 1024 * 1024)),
        name="dilated_attention",
    )(*([proj] * 9))


def _conv_kernel(u_ref, c_ref, b_ref, w_ref, o_ref):
    z = c_ref[...].astype(F32) * u_ref[...].astype(F32)
    w = w_ref[...]
    row = lax.broadcasted_iota(jnp.int32, z.shape, 0)
    y = w[0:1, :] * z
    for tap in range(1, CONV_K):
        shifted = jnp.where(row >= tap, pltpu.roll(z, tap, axis=0), 0.0)
        y = y + w[tap:tap + 1, :] * shifted
    o_ref[...] = (b_ref[...].astype(F32) * y).astype(o_ref.dtype)


def _gated_conv(proj, conv_w, batch, seq, *, tn):
    def col_spec(col0):
        return pl.BlockSpec((seq, tn), lambda b, j: (b, col0 // tn + j))

    blocks = 4 * _nbytes((seq, tn), BF16) + _nbytes((CONV_K, tn), F32)
    return pl.pallas_call(
        _conv_kernel,
        grid=(batch, CONV_WIDTH // tn),
        in_specs=[col_spec(COL_U), col_spec(COL_C), col_spec(COL_B),
                  pl.BlockSpec((CONV_K, tn), lambda b, j: (0, j))],
        out_specs=pl.BlockSpec((seq, tn), lambda b, j: (b, j)),
        out_shape=jax.ShapeDtypeStruct((batch * seq, CONV_WIDTH), BF16),
        compiler_params=pltpu.CompilerParams(
            dimension_semantics=("parallel", "parallel"),
            vmem_limit_bytes=_vmem_limit(blocks, 6 * _nbytes((seq, tn), F32))),
        name="gated_conv",
    )(proj, proj, proj, conv_w)


def _merge_kernel(attn_ref, conv_ref, ga_ref, gc_ref, wao_ref, wco_ref, o_ref):
    a = jnp.dot(attn_ref[...], wao_ref[...], preferred_element_type=F32)
    c = jnp.dot(conv_ref[...], wco_ref[...], preferred_element_type=F32)
    ga = jax.nn.sigmoid(ga_ref[...].astype(F32))
    gc = jax.nn.sigmoid(gc_ref[...].astype(F32))
    o_ref[...] = (ga * a + gc * c).astype(o_ref.dtype)


def _gated_merge(attn, conv, proj, w_attn_o, w_conv_o, *, tm, tn):
    m = attn.shape[0]
    blocks = (_nbytes((tm, GROUP_WIDTH), BF16) + _nbytes((tm, CONV_WIDTH), BF16)
              + 3 * _nbytes((tm, tn), BF16) + _nbytes((GROUP_WIDTH + CONV_WIDTH, tn), BF16))
    return pl.pallas_call(
        _merge_kernel,
        grid=(m // tm, D_MODEL // tn),
        in_specs=[pl.BlockSpec((tm, GROUP_WIDTH), lambda i, j: (i, 0)),
                  pl.BlockSpec((tm, CONV_WIDTH), lambda i, j: (i, 0)),
                  pl.BlockSpec((tm, tn), lambda i, j: (i, COL_GA // tn + j)),
                  pl.BlockSpec((tm, tn), lambda i, j: (i, COL_GC // tn + j)),
                  pl.BlockSpec((GROUP_WIDTH, tn), lambda i, j: (0, j)),
                  pl.BlockSpec((CONV_WIDTH, tn), lambda i, j: (0, j))],
        out_specs=pl.BlockSpec((tm, tn), lambda i, j: (i, j)),
        out_shape=jax.ShapeDtypeStruct((m, D_MODEL), BF16),
        compiler_params=pltpu.CompilerParams(
            dimension_semantics=("parallel", "arbitrary"),
            vmem_limit_bytes=_vmem_limit(blocks, 4 * _nbytes((tm, tn), F32))),
        name="gated_merge",
    )(attn, conv, proj, proj, w_attn_o, w_conv_o)


def _layer_norm(z, g, b):
    mu = jnp.mean(z, axis=-1, keepdims=True)
    zc = z - mu
    var = jnp.mean(zc * zc, axis=-1, keepdims=True)
    return zc * lax.rsqrt(var + LN_EPS) * g + b


def _out_ln_kernel(mi_ref, w_ref, x_ref, g_ref, b_ref, o_ref):
    y = jnp.dot(mi_ref[...], w_ref[...], preferred_element_type=F32)
    o_ref[...] = _layer_norm(ALPHA * x_ref[...] + y, g_ref[...], b_ref[...])


def _out_proj_ln(merged, w_out, x, g, b, *, tm):
    m = merged.shape[0]
    blocks = (_nbytes((tm, D_MODEL), BF16) + _nbytes((D_MODEL, D_MODEL), BF16)
              + 2 * _nbytes((tm, D_MODEL), F32))
    row = pl.BlockSpec((tm, D_MODEL), lambda i: (i, 0))
    vec = pl.BlockSpec((1, D_MODEL), lambda i: (0, 0))
    return pl.pallas_call(
        _out_ln_kernel,
        grid=(m // tm,),
        in_specs=[row, pl.BlockSpec((D_MODEL, D_MODEL), lambda i: (0, 0)), row, vec, vec],
        out_specs=row,
        out_shape=jax.ShapeDtypeStruct((m, D_MODEL), F32),
        compiler_params=pltpu.CompilerParams(
            dimension_semantics=("parallel",),
            vmem_limit_bytes=_vmem_limit(blocks, 3 * _nbytes((tm, D_MODEL), F32))),
        name="out_proj_ln",
    )(merged, w_out, x, g, b)


def _ffn_kernel(h_ref, wg_ref, wu_ref, wd_ref, g_ref, b_ref, o_ref, hb_ref, acc_ref):
    f = pl.program_id(1)

    @pl.when(f == 0)
    def _():
        hb_ref[...] = h_ref[...].astype(BF16)

    hb = hb_ref[...]
    gate = jnp.dot(hb, wg_ref[...], preferred_element_type=F32)
    up = jnp.dot(hb, wu_ref[...], preferred_element_type=F32)
    hidden = (jax.nn.silu(gate) * up).astype(BF16)
    part = jnp.dot(hidden, wd_ref[...], preferred_element_type=F32)

    @pl.when(f == 0)
    def _():
        acc_ref[...] = part

    @pl.when(f > 0)
    def _():
        acc_ref[...] += part

    @pl.when(f == pl.num_programs(1) - 1)
    def _():
        o_ref[...] = _layer_norm(ALPHA * h_ref[...] + acc_ref[...], g_ref[...], b_ref[...])


def _ffn_ln(h, w_gate, w_up, w_down, g, b, *, tm, tf):
    m = h.shape[0]
    blocks = (2 * _nbytes((tm, D_MODEL), F32) + 2 * _nbytes((D_MODEL, tf), BF16)
              + _nbytes((tf, D_MODEL), BF16))
    scratch = _nbytes((tm, D_MODEL), BF16) + _nbytes((tm, D_MODEL), F32)
    row = pl.BlockSpec((tm, D_MODEL), lambda i, f: (i, 0))
    vec = pl.BlockSpec((1, D_MODEL), lambda i, f: (0, 0))
    return pl.pallas_call(
        _ffn_kernel,
        grid=(m // tm, FFN_HIDDEN // tf),
        in_specs=[row,
                  pl.BlockSpec((D_MODEL, tf), lambda i, f: (0, f)),
                  pl.BlockSpec((D_MODEL, tf), lambda i, f: (0, f)),
                  pl.BlockSpec((tf, D_MODEL), lambda i, f: (f, 0)),
                  vec, vec],
        out_specs=row,
        out_shape=jax.ShapeDtypeStruct((m, D_MODEL), F32),
        scratch_shapes=[pltpu.VMEM((tm, D_MODEL), BF16), pltpu.VMEM((tm, D_MODEL), F32)],
        compiler_params=pltpu.CompilerParams(
            dimension_semantics=("parallel", "arbitrary"),
            vmem_limit_bytes=_vmem_limit(blocks, scratch + 4 * _nbytes((tm, tf), F32)
                                         + _nbytes((tm, D_MODEL), F32))),
        name="ffn_ln",
    )(h, w_gate, w_up, w_down, g, b)


def kernel(x, w_in, conv_w, w_attn_o, w_conv_o, w_out, ln1_g, ln1_b,
           w_ffn_gate, w_ffn_up, w_ffn_down, ln2_g, ln2_b):
    batch, seq, d = x.shape
    assert d == D_MODEL and seq % ATTN_BLOCK == 0 and w_in.shape == (DEPTH, D_MODEL, IN_COLS)
    h = x.reshape(batch * seq, d)
    for layer in range(DEPTH):
        proj = _matmul(h.astype(BF16), w_in[layer].astype(BF16), tm=1024, tn=1024, out_dtype=BF16)
        attn = _attention(proj, batch, seq)
        conv = _gated_conv(proj, conv_w[layer], batch, seq, tn=512)
        merged = _gated_merge(attn, conv, proj, w_attn_o[layer].astype(BF16),
                              w_conv_o[layer].astype(BF16), tm=1024, tn=512)
        h = _out_proj_ln(merged, w_out[layer].astype(BF16), h,
                         ln1_g[layer][None, :], ln1_b[layer][None, :], tm=512)
        h = _ffn_ln(h, w_ffn_gate[layer].astype(BF16), w_ffn_up[layer].astype(BF16),
                    w_ffn_down[layer].astype(BF16), ln2_g[layer][None, :], ln2_b[layer][None, :],
                    tm=512, tf=512)
    return h.reshape(batch, seq, d)
```

```python
import functools
import math

import jax
import jax.numpy as jnp
from jax import lax
from jax.experimental import pallas as pl
from jax.experimental.pallas import tpu as pltpu

D_MODEL = 2048
HEAD_DIM = 128
HEADS_PER_GROUP = 8
DILATED_GROUPS = ((128, 1), (512, 4), (2048, 16))
N_GROUPS = len(DILATED_GROUPS)
GROUP_WIDTH = HEADS_PER_GROUP * HEAD_DIM
ATTN_WIDTH = N_GROUPS * GROUP_WIDTH
CONV_WIDTH = D_MODEL
CONV_K = 3
FFN_HIDDEN = 5632
DEPTH = 1
ALPHA = (2 * DEPTH) ** 0.25
LN_EPS = 1e-5
IN_COLS = 3 * ATTN_WIDTH + 3 * CONV_WIDTH + 2 * D_MODEL

NAT_Q, NAT_K, NAT_V = 0, GROUP_WIDTH, 2 * GROUP_WIDTH
NAT_U = 3 * GROUP_WIDTH
NAT_C = NAT_U + CONV_WIDTH
NAT_B = NAT_C + CONV_WIDTH
NAT_GA = NAT_B + CONV_WIDTH
NAT_GC = NAT_GA + D_MODEL
NAT_COLS = NAT_GC + D_MODEL

V7X_VMEM_BYTES = 64 * 1024 * 1024
BAND = 128

F32 = jnp.float32
BF16 = jnp.bfloat16


def _vmem_limit(pipelined_block_bytes, resident_bytes):
    need = 2 * pipelined_block_bytes + resident_bytes
    return min(int(need * 1.25), V7X_VMEM_BYTES - 4 * 1024 * 1024)


def _nbytes(shape, dtype):
    return math.prod(shape) * jnp.dtype(dtype).itemsize


def _matmul_kernel(x_ref, w_ref, o_ref):
    o_ref[...] = jnp.dot(x_ref[...], w_ref[...], preferred_element_type=F32).astype(o_ref.dtype)


def _in_proj_natural(x, w_in, *, tm, tn):
    m, k = x.shape
    assert tn == GROUP_WIDTH
    q_blocks = ATTN_WIDTH // tn

    def w_col(j):
        return jnp.where(j < 3, j * q_blocks, j + 3 * q_blocks - 3)

    blocks = _nbytes((tm, k), BF16) + _nbytes((k, tn), BF16) + _nbytes((tm, tn), BF16)
    return pl.pallas_call(
        _matmul_kernel,
        grid=(m // tm, NAT_COLS // tn),
        in_specs=[pl.BlockSpec((tm, k), lambda i, j: (i, 0)),
                  pl.BlockSpec((k, tn), lambda i, j: (0, w_col(j)))],
        out_specs=pl.BlockSpec((tm, tn), lambda i, j: (i, j)),
        out_shape=jax.ShapeDtypeStruct((m, NAT_COLS), BF16),
        compiler_params=pltpu.CompilerParams(
            dimension_semantics=("parallel", "arbitrary"),
            vmem_limit_bytes=_vmem_limit(blocks, _nbytes((tm, tn), F32))),
        name="in_proj_natural",
    )(x, w_in)


def _dilated_proj_kernel(x_ref, w_ref, o_ref, *, dilation):
    length = x_ref.shape[0]
    w = w_ref[...]
    for r in range(dilation):
        y = jnp.dot(x_ref[:, r * D_MODEL:(r + 1) * D_MODEL], w, preferred_element_type=F32)
        o_ref[r * length:(r + 1) * length, :] = y.astype(o_ref.dtype)


def _in_proj_dilated(x, w_in, group, batch, seq):
    dilation = DILATED_GROUPS[group][1]
    length = seq // dilation
    tn = GROUP_WIDTH
    q_blocks = ATTN_WIDTH // tn
    xv = x.reshape(batch, length, dilation * D_MODEL)
    blocks = (_nbytes((seq, D_MODEL), BF16) + _nbytes((D_MODEL, tn), BF16)
              + _nbytes((seq, tn), BF16))
    return pl.pallas_call(
        functools.partial(_dilated_proj_kernel, dilation=dilation),
        grid=(batch, 3),
        in_specs=[pl.BlockSpec((None, length, dilation * D_MODEL), lambda b, j: (b, 0, 0)),
                  pl.BlockSpec((D_MODEL, tn), lambda b, j: (0, j * q_blocks + group))],
        out_specs=pl.BlockSpec((seq, tn), lambda b, j: (b, j)),
        out_shape=jax.ShapeDtypeStruct((batch * seq, 3 * tn), BF16),
        compiler_params=pltpu.CompilerParams(
            dimension_semantics=("parallel", "arbitrary"),
            vmem_limit_bytes=_vmem_limit(blocks, 2 * _nbytes((length, tn), F32))),
        name=f"in_proj_dilated{dilation}",
    )(xv, w_in)


def _attn_kernel(q1_ref, k1_ref, v1_ref, q2_ref, k2_ref, v2_ref, q3_ref, k3_ref, v3_ref,
                 o_ref, o2_ref, lse2_ref, o3_ref, lse3_ref):
    seq = q1_ref.shape[0]
    n_tiles = seq // BAND
    scale = HEAD_DIM ** -0.5
    qi = lax.broadcasted_iota(jnp.int32, (1, BAND, BAND), 1)
    kj = lax.broadcasted_iota(jnp.int32, (1, BAND, BAND), 2)
    causal = qi >= kj
    in_band = qi <= kj

    def qk(q, k):
        return jnp.einsum("nqd,nkd->nqk", q, k, preferred_element_type=F32) * scale

    def pv(p, v):
        return jnp.einsum("nqk,nkd->nqd", p.astype(BF16), v, preferred_element_type=F32)

    def group(q_ref, k_ref, v_ref, tiles_per_residue):
        q = q_ref[...].reshape(n_tiles, BAND, HEAD_DIM)
        k = k_ref[...].reshape(n_tiles, BAND, HEAD_DIM)
        v = v_ref[...].reshape(n_tiles, BAND, HEAD_DIM)
        s = jnp.where(causal, qk(q, k), -jnp.inf)
        m = s.max(-1, keepdims=True)
        if tiles_per_residue > 1:
            tile = lax.broadcasted_iota(jnp.int32, (n_tiles - 1, 1, 1), 0) + 1
            has_prev = (tile % tiles_per_residue) != 0
            sp = jnp.where(in_band & has_prev, qk(q[1:], k[:-1]), -jnp.inf)
            m = jnp.concatenate([m[:1], jnp.maximum(m[1:], sp.max(-1, keepdims=True))], axis=0)
            pp = jnp.exp(sp - m[1:])
        p = jnp.exp(s - m)
        l = p.sum(-1, keepdims=True)
        acc = pv(p, v)
        if tiles_per_residue > 1:
            l = jnp.concatenate([l[:1], l[1:] + pp.sum(-1, keepdims=True)], axis=0)
            acc = jnp.concatenate([acc[:1], acc[1:] + pv(pp, v[:-1])], axis=0)
        o = (acc / l).reshape(seq, HEAD_DIM)
        lse = (m + jnp.log(l)).reshape(seq, 1)
        return o, lse

    def to_natural(o, lse, o_nat_ref, lse_nat_ref, dilation):
        length = seq // dilation
        for r in range(dilation):
            rows = slice(r * length, (r + 1) * length)
            o_nat_ref[pl.ds(r, length, stride=dilation), :] = o[rows, :]
            lse_nat_ref[pl.ds(r, length, stride=dilation), :] = lse[rows, :]
        return o_nat_ref[...], lse_nat_ref[...]

    o1, lse1 = group(q1_ref, k1_ref, v1_ref, seq // BAND)
    d2, d3 = DILATED_GROUPS[1][1], DILATED_GROUPS[2][1]
    o2, lse2 = to_natural(*group(q2_ref, k2_ref, v2_ref, seq // d2 // BAND), o2_ref, lse2_ref, d2)
    o3, lse3 = to_natural(*group(q3_ref, k3_ref, v3_ref, seq // d3 // BAND), o3_ref, lse3_ref, d3)

    top = jnp.maximum(jnp.maximum(lse1, lse2), lse3)
    w1, w2, w3 = jnp.exp(lse1 - top), jnp.exp(lse2 - top), jnp.exp(lse3 - top)
    o_ref[...] = ((w1 * o1 + w2 * o2 + w3 * o3) / (w1 + w2 + w3)).astype(o_ref.dtype)


def _attention(proj_nat, qkv2, qkv3, batch, seq):
    heads = HEADS_PER_GROUP

    def head_spec(section):
        return pl.BlockSpec((seq, HEAD_DIM), lambda b, h: (b, section * heads + h))

    qkv_specs = [head_spec(0), head_spec(1), head_spec(2)]
    tile = _nbytes((seq, HEAD_DIM), F32)
    return pl.pallas_call(
        _attn_kernel,
        grid=(batch, heads),
        in_specs=qkv_specs * 3,
        out_specs=pl.BlockSpec((seq, HEAD_DIM), lambda b, h: (b, h)),
        out_shape=jax.ShapeDtypeStruct((batch * seq, GROUP_WIDTH), BF16),
        scratch_shapes=[pltpu.VMEM((seq, HEAD_DIM), F32), pltpu.VMEM((seq, 1), F32),
                        pltpu.VMEM((seq, HEAD_DIM), F32), pltpu.VMEM((seq, 1), F32)],
        compiler_params=pltpu.CompilerParams(
            dimension_semantics=("parallel", "parallel"),
            vmem_limit_bytes=_vmem_limit(10 * _nbytes((seq, HEAD_DIM), BF16), 24 * tile)),
        name="dilated_attention",
    )(proj_nat, proj_nat, proj_nat, qkv2, qkv2, qkv2, qkv3, qkv3, qkv3)


def _conv_kernel(u_ref, c_ref, b_ref, w_ref, o_ref):
    z = c_ref[...].astype(F32) * u_ref[...].astype(F32)
    w = w_ref[...]
    row = lax.broadcasted_iota(jnp.int32, z.shape, 0)
    y = w[0:1, :] * z
    for tap in range(1, CONV_K):
        shifted = jnp.where(row >= tap, pltpu.roll(z, tap, axis=0), 0.0)
        y = y + w[tap:tap + 1, :] * shifted
    o_ref[...] = (b_ref[...].astype(F32) * y).astype(o_ref.dtype)


def _gated_conv(proj_nat, conv_w, batch, seq, *, tn):
    def col_spec(col0):
        return pl.BlockSpec((seq, tn), lambda b, j: (b, col0 // tn + j))

    blocks = 4 * _nbytes((seq, tn), BF16) + _nbytes((CONV_K, tn), F32)
    return pl.pallas_call(
        _conv_kernel,
        grid=(batch, CONV_WIDTH // tn),
        in_specs=[col_spec(NAT_U), col_spec(NAT_C), col_spec(NAT_B),
                  pl.BlockSpec((CONV_K, tn), lambda b, j: (0, j))],
        out_specs=pl.BlockSpec((seq, tn), lambda b, j: (b, j)),
        out_shape=jax.ShapeDtypeStruct((batch * seq, CONV_WIDTH), BF16),
        compiler_params=pltpu.CompilerParams(
            dimension_semantics=("parallel", "parallel"),
            vmem_limit_bytes=_vmem_limit(blocks, 6 * _nbytes((seq, tn), F32))),
        name="gated_conv",
    )(proj_nat, proj_nat, proj_nat, conv_w)


def _merge_kernel(attn_ref, conv_ref, ga_ref, gc_ref, wao_ref, wco_ref, o_ref):
    a = jnp.dot(attn_ref[...], wao_ref[...], preferred_element_type=F32)
    c = jnp.dot(conv_ref[...], wco_ref[...], preferred_element_type=F32)
    ga = jax.nn.sigmoid(ga_ref[...].astype(F32))
    gc = jax.nn.sigmoid(gc_ref[...].astype(F32))
    o_ref[...] = (ga * a + gc * c).astype(o_ref.dtype)


def _gated_merge(attn, conv, proj_nat, w_attn_o, w_conv_o, *, tm, tn):
    m = attn.shape[0]
    blocks = (_nbytes((tm, GROUP_WIDTH), BF16) + _nbytes((tm, CONV_WIDTH), BF16)
              + 3 * _nbytes((tm, tn), BF16) + _nbytes((GROUP_WIDTH + CONV_WIDTH, tn), BF16))
    return pl.pallas_call(
        _merge_kernel,
        grid=(m // tm, D_MODEL // tn),
        in_specs=[pl.BlockSpec((tm, GROUP_WIDTH), lambda i, j: (i, 0)),
                  pl.BlockSpec((tm, CONV_WIDTH), lambda i, j: (i, 0)),
                  pl.BlockSpec((tm, tn), lambda i, j: (i, NAT_GA // tn + j)),
                  pl.BlockSpec((tm, tn), lambda i, j: (i, NAT_GC // tn + j)),
                  pl.BlockSpec((GROUP_WIDTH, tn), lambda i, j: (0, j)),
                  pl.BlockSpec((CONV_WIDTH, tn), lambda i, j: (0, j))],
        out_specs=pl.BlockSpec((tm, tn), lambda i, j: (i, j)),
        out_shape=jax.ShapeDtypeStruct((m, D_MODEL), BF16),
        compiler_params=pltpu.CompilerParams(
            dimension_semantics=("parallel", "arbitrary"),
            vmem_limit_bytes=_vmem_limit(blocks, 4 * _nbytes((tm, tn), F32))),
        name="gated_merge",
    )(attn, conv, proj_nat, proj_nat, w_attn_o, w_conv_o)


def _layer_norm(z, g, b):
    mu = jnp.mean(z, axis=-1, keepdims=True)
    zc = z - mu
    var = jnp.mean(zc * zc, axis=-1, keepdims=True)
    return zc * lax.rsqrt(var + LN_EPS) * g + b


def _out_ln_kernel(mi_ref, w_ref, x_ref, g_ref, b_ref, o_ref):
    y = jnp.dot(mi_ref[...], w_ref[...], preferred_element_type=F32)
    o_ref[...] = _layer_norm(ALPHA * x_ref[...] + y, g_ref[...], b_ref[...])


def _out_proj_ln(merged, w_out, x, g, b, *, tm):
    m = merged.shape[0]
    blocks = (_nbytes((tm, D_MODEL), BF16) + _nbytes((D_MODEL, D_MODEL), BF16)
              + 2 * _nbytes((tm, D_MODEL), F32))
    row = pl.BlockSpec((tm, D_MODEL), lambda i: (i, 0))
    vec = pl.BlockSpec((1, D_MODEL), lambda i: (0, 0))
    return pl.pallas_call(
        _out_ln_kernel,
        grid=(m // tm,),
        in_specs=[row, pl.BlockSpec((D_MODEL, D_MODEL), lambda i: (0, 0)), row, vec, vec],
        out_specs=row,
        out_shape=jax.ShapeDtypeStruct((m, D_MODEL), F32),
        compiler_params=pltpu.CompilerParams(
            dimension_semantics=("parallel",),
            vmem_limit_bytes=_vmem_limit(blocks, 3 * _nbytes((tm, D_MODEL), F32))),
        name="out_proj_ln",
    )(merged, w_out, x, g, b)


def _ffn_kernel(h_ref, wg_ref, wu_ref, wd_ref, g_ref, b_ref, o_ref, hb_ref, acc_ref):
    f = pl.program_id(1)

    @pl.when(f == 0)
    def _():
        hb_ref[...] = h_ref[...].astype(BF16)

    hb = hb_ref[...]
    gate = jnp.dot(hb, wg_ref[...], preferred_element_type=F32)
    up = jnp.dot(hb, wu_ref[...], preferred_element_type=F32)
    hidden = (jax.nn.silu(gate) * up).astype(BF16)
    part = jnp.dot(hidden, wd_ref[...], preferred_element_type=F32)

    @pl.when(f == 0)
    def _():
        acc_ref[...] = part

    @pl.when(f > 0)
    def _():
        acc_ref[...] += part

    @pl.when(f == pl.num_programs(1) - 1)
    def _():
        o_ref[...] = _layer_norm(ALPHA * h_ref[...] + acc_ref[...], g_ref[...], b_ref[...])


def _ffn_ln(h, w_gate, w_up, w_down, g, b, *, tm, tf):
    m = h.shape[0]
    blocks = (2 * _nbytes((tm, D_MODEL), F32) + 2 * _nbytes((D_MODEL, tf), BF16)
              + _nbytes((tf, D_MODEL), BF16))
    scratch = _nbytes((tm, D_MODEL), BF16) + _nbytes((tm, D_MODEL), F32)
    row = pl.BlockSpec((tm, D_MODEL), lambda i, f: (i, 0))
    vec = pl.BlockSpec((1, D_MODEL), lambda i, f: (0, 0))
    return pl.pallas_call(
        _ffn_kernel,
        grid=(m // tm, FFN_HIDDEN // tf),
        in_specs=[row,
                  pl.BlockSpec((D_MODEL, tf), lambda i, f: (0, f)),
                  pl.BlockSpec((D_MODEL, tf), lambda i, f: (0, f)),
                  pl.BlockSpec((tf, D_MODEL), lambda i, f: (f, 0)),
                  vec, vec],
        out_specs=row,
        out_shape=jax.ShapeDtypeStruct((m, D_MODEL), F32),
        scratch_shapes=[pltpu.VMEM((tm, D_MODEL), BF16), pltpu.VMEM((tm, D_MODEL), F32)],
        compiler_params=pltpu.CompilerParams(
            dimension_semantics=("parallel", "arbitrary"),
            vmem_limit_bytes=_vmem_limit(blocks, scratch + 4 * _nbytes((tm, tf), F32)
                                         + _nbytes((tm, D_MODEL), F32))),
        name="ffn_ln",
    )(h, w_gate, w_up, w_down, g, b)


def kernel(x, w_in, conv_w, w_attn_o, w_conv_o, w_out, ln1_g, ln1_b,
           w_ffn_gate, w_ffn_up, w_ffn_down, ln2_g, ln2_b):
    batch, seq, d = x.shape
    assert d == D_MODEL and w_in.shape == (DEPTH, D_MODEL, IN_COLS)
    assert all(seq % (dil * BAND) == 0 for _, dil in DILATED_GROUPS)
    h = x.reshape(batch * seq, d)
    for layer in range(DEPTH):
        hb = h.astype(BF16)
        w_in_b = w_in[layer].astype(BF16)
        proj_nat = _in_proj_natural(hb, w_in_b, tm=1024, tn=GROUP_WIDTH)
        qkv2 = _in_proj_dilated(hb, w_in_b, 1, batch, seq)
        qkv3 = _in_proj_dilated(hb, w_in_b, 2, batch, seq)
        attn = _attention(proj_nat, qkv2, qkv3, batch, seq)
        conv = _gated_conv(proj_nat, conv_w[layer], batch, seq, tn=512)
        merged = _gated_merge(attn, conv, proj_nat, w_attn_o[layer].astype(BF16),
                              w_conv_o[layer].astype(BF16), tm=1024, tn=512)
        h = _out_proj_ln(merged, w_out[layer].astype(BF16), h,
                         ln1_g[layer][None, :], ln1_b[layer][None, :], tm=512)
        h = _ffn_ln(h, w_ffn_gate[layer].astype(BF16), w_ffn_up[layer].astype(BF16),
                    w_ffn_down[layer].astype(BF16), ln2_g[layer][None, :], ln2_b[layer][None, :],
                    tm=512, tf=512)
    return h.reshape(batch, seq, d)
```

```python
import functools
import math

import jax
import jax.numpy as jnp
from jax import lax
from jax.experimental import pallas as pl
from jax.experimental.pallas import tpu as pltpu

D_MODEL = 2048
HEAD_DIM = 128
HEADS_PER_GROUP = 8
DILATED_GROUPS = ((128, 1), (512, 4), (2048, 16))
N_GROUPS = len(DILATED_GROUPS)
GROUP_WIDTH = HEADS_PER_GROUP * HEAD_DIM
ATTN_WIDTH = N_GROUPS * GROUP_WIDTH
CONV_WIDTH = D_MODEL
CONV_K = 3
FFN_HIDDEN = 5632
DEPTH = 1
ALPHA = (2 * DEPTH) ** 0.25
LN_EPS = 1e-5
IN_COLS = 3 * ATTN_WIDTH + 3 * CONV_WIDTH + 2 * D_MODEL

REST_COL0 = 3 * ATTN_WIDTH
REST_U = 0
REST_C = REST_U + CONV_WIDTH
REST_B = REST_C + CONV_WIDTH
REST_GA = REST_B + CONV_WIDTH
REST_GC = REST_GA + D_MODEL
REST_COLS = REST_GC + D_MODEL

V7X_VMEM_BYTES = 64 * 1024 * 1024
BAND = 128
PROJ_ROWS = 1024
PROJ_COLS = 1024

F32 = jnp.float32
BF16 = jnp.bfloat16


def _vmem_limit(pipelined_block_bytes, resident_bytes):
    need = 2 * pipelined_block_bytes + resident_bytes
    return min(int(need * 1.25), V7X_VMEM_BYTES - 4 * 1024 * 1024)


def _nbytes(shape, dtype):
    return math.prod(shape) * jnp.dtype(dtype).itemsize


def _qkv_proj_kernel(x_ref, w_ref, o_ref, xb_ref, y_ref, *, dilation):
    rows = x_ref.shape[0]
    per_residue = rows // dilation

    @pl.when(pl.program_id(2) == 0)
    def _():
        xb_ref[...] = x_ref[...].astype(BF16)

    y = jnp.dot(xb_ref[...], w_ref[...], preferred_element_type=F32)
    for head in range(HEADS_PER_GROUP):
        y_head = y[:, head * HEAD_DIM:(head + 1) * HEAD_DIM]
        if dilation == 1:
            o_ref[head, 0] = y_head.astype(o_ref.dtype)
        else:
            y_ref[head] = y_head
            for r in range(dilation):
                o_ref[head, r] = y_ref[head, pl.ds(r, per_residue, stride=dilation), :].astype(o_ref.dtype)


def _qkv_proj(x, w_in, group, batch, seq):
    dilation = DILATED_GROUPS[group][1]
    length = seq // dilation
    tiles = seq // PROJ_ROWS
    per_residue = PROJ_ROWS // dilation
    assert PROJ_COLS == GROUP_WIDTH
    sections = ATTN_WIDTH // PROJ_COLS
    blocks = (_nbytes((PROJ_ROWS, D_MODEL), F32) + _nbytes((D_MODEL, PROJ_COLS), BF16)
              + _nbytes((PROJ_ROWS, PROJ_COLS), BF16))
    out = pl.pallas_call(
        functools.partial(_qkv_proj_kernel, dilation=dilation),
        grid=(batch, tiles, 3),
        in_specs=[pl.BlockSpec((PROJ_ROWS, D_MODEL), lambda b, t, j: (b * tiles + t, 0)),
                  pl.BlockSpec((D_MODEL, PROJ_COLS), lambda b, t, j: (0, j * sections + group))],
        out_specs=pl.BlockSpec((HEADS_PER_GROUP, None, dilation, per_residue, HEAD_DIM),
                               lambda b, t, j: (j, b, 0, t, 0)),
        out_shape=jax.ShapeDtypeStruct((3 * HEADS_PER_GROUP, batch, dilation, length, HEAD_DIM), BF16),
        scratch_shapes=[pltpu.VMEM((PROJ_ROWS, D_MODEL), BF16),
                        pltpu.VMEM((HEADS_PER_GROUP, PROJ_ROWS, HEAD_DIM), F32)],
        compiler_params=pltpu.CompilerParams(
            dimension_semantics=("parallel", "parallel", "arbitrary"),
            vmem_limit_bytes=_vmem_limit(blocks, _nbytes((PROJ_ROWS, D_MODEL), BF16)
                                         + 2 * _nbytes((PROJ_ROWS, PROJ_COLS), F32))),
        name=f"qkv_proj_dilation{dilation}",
    )(x, w_in)
    return out.reshape(3 * HEADS_PER_GROUP, batch, seq, HEAD_DIM)


def _rest_proj_kernel(x_ref, w_ref, o_ref, xb_ref):
    @pl.when(pl.program_id(1) == 0)
    def _():
        xb_ref[...] = x_ref[...].astype(BF16)

    o_ref[...] = jnp.dot(xb_ref[...], w_ref[...], preferred_element_type=F32).astype(o_ref.dtype)


def _rest_proj(x, w_in):
    m = x.shape[0]
    col0 = REST_COL0 // PROJ_COLS
    blocks = (_nbytes((PROJ_ROWS, D_MODEL), F32) + _nbytes((D_MODEL, PROJ_COLS), BF16)
              + _nbytes((PROJ_ROWS, PROJ_COLS), BF16))
    return pl.pallas_call(
        _rest_proj_kernel,
        grid=(m // PROJ_ROWS, REST_COLS // PROJ_COLS),
        in_specs=[pl.BlockSpec((PROJ_ROWS, D_MODEL), lambda i, j: (i, 0)),
                  pl.BlockSpec((D_MODEL, PROJ_COLS), lambda i, j: (0, col0 + j))],
        out_specs=pl.BlockSpec((PROJ_ROWS, PROJ_COLS), lambda i, j: (i, j)),
        out_shape=jax.ShapeDtypeStruct((m, REST_COLS), BF16),
        scratch_shapes=[pltpu.VMEM((PROJ_ROWS, D_MODEL), BF16)],
        compiler_params=pltpu.CompilerParams(
            dimension_semantics=("parallel", "arbitrary"),
            vmem_limit_bytes=_vmem_limit(blocks, _nbytes((PROJ_ROWS, D_MODEL), BF16)
                                         + _nbytes((PROJ_ROWS, PROJ_COLS), F32))),
        name="rest_proj",
    )(x, w_in)


def _attn_kernel(q1_ref, k1_ref, v1_ref, q2_ref, k2_ref, v2_ref, q3_ref, k3_ref, v3_ref,
                 o_ref, o_nat_ref, lse_nat_ref):
    seq = q1_ref.shape[0]
    n_tiles = seq // BAND
    scale = HEAD_DIM ** -0.5
    qi = lax.broadcasted_iota(jnp.int32, (BAND, 2 * BAND), 0)
    kj = lax.broadcasted_iota(jnp.int32, (BAND, 2 * BAND), 1)
    window_mask = ((kj < BAND) & (qi <= kj)) | ((kj >= BAND) & (qi >= kj - BAND))
    causal_mask = (lax.broadcasted_iota(jnp.int32, (BAND, BAND), 0)
                   >= lax.broadcasted_iota(jnp.int32, (BAND, BAND), 1))

    def tile(q_ref, k_ref, v_ref, n, has_prev):
        q = q_ref[n * BAND:(n + 1) * BAND, :]
        keys = slice((n - 1) * BAND, (n + 1) * BAND) if has_prev else slice(n * BAND, (n + 1) * BAND)
        s = lax.dot_general(q, k_ref[keys, :], (((1,), (1,)), ((), ())),
                            preferred_element_type=F32) * scale
        s = jnp.where(window_mask if has_prev else causal_mask, s, -jnp.inf)
        m = s.max(-1, keepdims=True)
        p = jnp.exp(s - m)
        l = p.sum(-1, keepdims=True)
        acc = jnp.dot(p.astype(BF16), v_ref[keys, :], preferred_element_type=F32)
        return acc / l, m + jnp.log(l)

    groups = ((q1_ref, k1_ref, v1_ref), (q2_ref, k2_ref, v2_ref), (q3_ref, k3_ref, v3_ref))
    for g, (q_ref, k_ref, v_ref) in enumerate(groups):
        dilation = DILATED_GROUPS[g][1]
        tiles_per_residue = n_tiles // dilation
        for n in range(n_tiles):
            residue, t = divmod(n, tiles_per_residue)
            o, lse = tile(q_ref, k_ref, v_ref, n, has_prev=t > 0)
            rows = pl.ds(t * BAND * dilation + residue, BAND, stride=dilation)
            o_nat_ref[g, rows, :] = o
            lse_nat_ref[g, rows, :] = jnp.broadcast_to(lse, (BAND, HEAD_DIM))

    def mix(c, _):
        rows = pl.ds(pl.multiple_of(c * BAND, BAND), BAND)
        lse = [lse_nat_ref[g, rows, :] for g in range(N_GROUPS)]
        top = jnp.maximum(jnp.maximum(lse[0], lse[1]), lse[2])
        w = [jnp.exp(x - top) for x in lse]
        num = w[0] * o_nat_ref[0, rows, :] + w[1] * o_nat_ref[1, rows, :] + w[2] * o_nat_ref[2, rows, :]
        o_ref[rows, :] = (num / (w[0] + w[1] + w[2])).astype(o_ref.dtype)
        return 0

    lax.fori_loop(0, n_tiles, mix, 0)


def _attention(qkv, batch, seq):
    heads = HEADS_PER_GROUP

    def head_spec(section):
        return pl.BlockSpec((None, None, seq, HEAD_DIM), lambda b, h: (section * heads + h, b, 0, 0))

    qkv_specs = [head_spec(0), head_spec(1), head_spec(2)]
    scratch = 2 * _nbytes((N_GROUPS, seq, HEAD_DIM), F32)
    return pl.pallas_call(
        _attn_kernel,
        grid=(batch, heads),
        in_specs=qkv_specs * N_GROUPS,
        out_specs=pl.BlockSpec((seq, HEAD_DIM), lambda b, h: (b, h)),
        out_shape=jax.ShapeDtypeStruct((batch * seq, GROUP_WIDTH), BF16),
        scratch_shapes=[pltpu.VMEM((N_GROUPS, seq, HEAD_DIM), F32),
                        pltpu.VMEM((N_GROUPS, seq, HEAD_DIM), F32)],
        compiler_params=pltpu.CompilerParams(
            dimension_semantics=("parallel", "parallel"),
            vmem_limit_bytes=_vmem_limit(10 * _nbytes((seq, HEAD_DIM), BF16), scratch + 4 * 1024 * 1024)),
        name="dilated_attention",
    )(*[a for g in range(N_GROUPS) for a in (qkv[g],) * 3])


def _conv_kernel(u_ref, c_ref, b_ref, w_ref, o_ref):
    z = c_ref[...].astype(F32) * u_ref[...].astype(F32)
    w = w_ref[...]
    row = lax.broadcasted_iota(jnp.int32, z.shape, 0)
    y = w[0:1, :] * z
    for tap in range(1, CONV_K):
        shifted = jnp.where(row >= tap, pltpu.roll(z, tap, axis=0), 0.0)
        y = y + w[tap:tap + 1, :] * shifted
    o_ref[...] = (b_ref[...].astype(F32) * y).astype(o_ref.dtype)


def _gated_conv(rest, conv_w, batch, seq, *, tn):
    def col_spec(col0):
        return pl.BlockSpec((seq, tn), lambda b, j: (b, col0 // tn + j))

    blocks = 4 * _nbytes((seq, tn), BF16) + _nbytes((CONV_K, tn), F32)
    return pl.pallas_call(
        _conv_kernel,
        grid=(batch, CONV_WIDTH // tn),
        in_specs=[col_spec(REST_U), col_spec(REST_C), col_spec(REST_B),
                  pl.BlockSpec((CONV_K, tn), lambda b, j: (0, j))],
        out_specs=pl.BlockSpec((seq, tn), lambda b, j: (b, j)),
        out_shape=jax.ShapeDtypeStruct((batch * seq, CONV_WIDTH), BF16),
        compiler_params=pltpu.CompilerParams(
            dimension_semantics=("parallel", "parallel"),
            vmem_limit_bytes=_vmem_limit(blocks, 6 * _nbytes((seq, tn), F32))),
        name="gated_conv",
    )(rest, rest, rest, conv_w)


def _merge_kernel(attn_ref, conv_ref, ga_ref, gc_ref, wao_ref, wco_ref, o_ref):
    a = jnp.dot(attn_ref[...], wao_ref[...], preferred_element_type=F32)
    c = jnp.dot(conv_ref[...], wco_ref[...], preferred_element_type=F32)
    ga = jax.nn.sigmoid(ga_ref[...].astype(F32))
    gc = jax.nn.sigmoid(gc_ref[...].astype(F32))
    o_ref[...] = (ga * a + gc * c).astype(o_ref.dtype)


def _gated_merge(attn, conv, rest, w_attn_o, w_conv_o, *, tm, tn):
    m = attn.shape[0]
    blocks = (_nbytes((tm, GROUP_WIDTH), BF16) + _nbytes((tm, CONV_WIDTH), BF16)
              + 3 * _nbytes((tm, tn), BF16) + _nbytes((GROUP_WIDTH + CONV_WIDTH, tn), BF16))
    return pl.pallas_call(
        _merge_kernel,
        grid=(m // tm, D_MODEL // tn),
        in_specs=[pl.BlockSpec((tm, GROUP_WIDTH), lambda i, j: (i, 0)),
                  pl.BlockSpec((tm, CONV_WIDTH), lambda i, j: (i, 0)),
                  pl.BlockSpec((tm, tn), lambda i, j: (i, REST_GA // tn + j)),
                  pl.BlockSpec((tm, tn), lambda i, j: (i, REST_GC // tn + j)),
                  pl.BlockSpec((GROUP_WIDTH, tn), lambda i, j: (0, j)),
                  pl.BlockSpec((CONV_WIDTH, tn), lambda i, j: (0, j))],
        out_specs=pl.BlockSpec((tm, tn), lambda i, j: (i, j)),
        out_shape=jax.ShapeDtypeStruct((m, D_MODEL), BF16),
        compiler_params=pltpu.CompilerParams(
            dimension_semantics=("parallel", "arbitrary"),
            vmem_limit_bytes=_vmem_limit(blocks, 4 * _nbytes((tm, tn), F32))),
        name="gated_merge",
    )(attn, conv, rest, rest, w_attn_o, w_conv_o)


def _layer_norm(z, g, b):
    mu = jnp.mean(z, axis=-1, keepdims=True)
    zc = z - mu
    var = jnp.mean(zc * zc, axis=-1, keepdims=True)
    return zc * lax.rsqrt(var + LN_EPS) * g + b


def _out_ln_kernel(mi_ref, w_ref, x_ref, g_ref, b_ref, o_ref):
    y = jnp.dot(mi_ref[...], w_ref[...], preferred_element_type=F32)
    o_ref[...] = _layer_norm(ALPHA * x_ref[...] + y, g_ref[...], b_ref[...])


def _out_proj_ln(merged, w_out, x, g, b, *, tm):
    m = merged.shape[0]
    blocks = (_nbytes((tm, D_MODEL), BF16) + _nbytes((D_MODEL, D_MODEL), BF16)
              + 2 * _nbytes((tm, D_MODEL), F32))
    row = pl.BlockSpec((tm, D_MODEL), lambda i: (i, 0))
    vec = pl.BlockSpec((1, D_MODEL), lambda i: (0, 0))
    return pl.pallas_call(
        _out_ln_kernel,
        grid=(m // tm,),
        in_specs=[row, pl.BlockSpec((D_MODEL, D_MODEL), lambda i: (0, 0)), row, vec, vec],
        out_specs=row,
        out_shape=jax.ShapeDtypeStruct((m, D_MODEL), F32),
        compiler_params=pltpu.CompilerParams(
            dimension_semantics=("parallel",),
            vmem_limit_bytes=_vmem_limit(blocks, 3 * _nbytes((tm, D_MODEL), F32))),
        name="out_proj_ln",
    )(merged, w_out, x, g, b)


def _ffn_kernel(h_ref, wg_ref, wu_ref, wd_ref, g_ref, b_ref, o_ref, hb_ref):
    f = pl.program_id(1)

    @pl.when(f == 0)
    def _():
        hb_ref[...] = h_ref[...].astype(BF16)
        o_ref[...] = jnp.zeros_like(o_ref)

    hb = hb_ref[...]
    gate = jnp.dot(hb, wg_ref[...], preferred_element_type=F32)
    up = jnp.dot(hb, wu_ref[...], preferred_element_type=F32)
    hidden = (jax.nn.silu(gate) * up).astype(BF16)
    o_ref[...] += jnp.dot(hidden, wd_ref[...], preferred_element_type=F32)

    @pl.when(f == pl.num_programs(1) - 1)
    def _():
        o_ref[...] = _layer_norm(ALPHA * h_ref[...] + o_ref[...], g_ref[...], b_ref[...])


def _ffn_ln(h, w_gate, w_up, w_down, g, b, *, tm, tf):
    m = h.shape[0]
    blocks = (2 * _nbytes((tm, D_MODEL), F32) + 2 * _nbytes((D_MODEL, tf), BF16)
              + _nbytes((tf, D_MODEL), BF16))
    scratch = _nbytes((tm, D_MODEL), BF16)
    row = pl.BlockSpec((tm, D_MODEL), lambda i, f: (i, 0))
    vec = pl.BlockSpec((1, D_MODEL), lambda i, f: (0, 0))
    return pl.pallas_call(
        _ffn_kernel,
        grid=(m // tm, FFN_HIDDEN // tf),
        in_specs=[row,
                  pl.BlockSpec((D_MODEL, tf), lambda i, f: (0, f)),
                  pl.BlockSpec((D_MODEL, tf), lambda i, f: (0, f)),
                  pl.BlockSpec((tf, D_MODEL), lambda i, f: (f, 0)),
                  vec, vec],
        out_specs=row,
        out_shape=jax.ShapeDtypeStruct((m, D_MODEL), F32),
        scratch_shapes=[pltpu.VMEM((tm, D_MODEL), BF16)],
        compiler_params=pltpu.CompilerParams(
            dimension_semantics=("parallel", "arbitrary"),
            vmem_limit_bytes=_vmem_limit(blocks, scratch + 4 * _nbytes((tm, tf), F32)
                                         + _nbytes((tm, D_MODEL), F32))),
        name="ffn_ln",
    )(h, w_gate, w_up, w_down, g, b)


def kernel(x, w_in, conv_w, w_attn_o, w_conv_o, w_out, ln1_g, ln1_b,
           w_ffn_gate, w_ffn_up, w_ffn_down, ln2_g, ln2_b):
    batch, seq, d = x.shape
    assert d == D_MODEL and w_in.shape == (DEPTH, D_MODEL, IN_COLS)
    assert seq % PROJ_ROWS == 0 and all(PROJ_ROWS % (dil * 16) == 0 for _, dil in DILATED_GROUPS)
    assert all(seq % (dil * BAND) == 0 for _, dil in DILATED_GROUPS)
    h = x.reshape(batch * seq, d)
    for layer in range(DEPTH):
        w_in_b = w_in[layer].astype(BF16)
        qkv = [_qkv_proj(h, w_in_b, g, batch, seq) for g in range(N_GROUPS)]
        rest = _rest_proj(h, w_in_b)
        attn = _attention(qkv, batch, seq)
        conv = _gated_conv(rest, conv_w[layer], batch, seq, tn=512)
        merged = _gated_merge(attn, conv, rest, w_attn_o[layer].astype(BF16),
                              w_conv_o[layer].astype(BF16), tm=1024, tn=512)
        h = _out_proj_ln(merged, w_out[layer].astype(BF16), h,
                         ln1_g[layer][None, :], ln1_b[layer][None, :], tm=512)
        h = _ffn_ln(h, w_ffn_gate[layer].astype(BF16), w_ffn_up[layer].astype(BF16),
                    w_ffn_down[layer].astype(BF16), ln2_g[layer][None, :], ln2_b[layer][None, :],
                    tm=1024, tf=256)
    return h.reshape(batch, seq, d)
```

```python
import functools
import math

import jax
import jax.numpy as jnp
from jax import lax
from jax.experimental import pallas as pl
from jax.experimental.pallas import tpu as pltpu

D_MODEL = 2048
HEAD_DIM = 128
HEADS_PER_GROUP = 8
DILATED_GROUPS = ((128, 1), (512, 4), (2048, 16))
N_GROUPS = len(DILATED_GROUPS)
GROUP_WIDTH = HEADS_PER_GROUP * HEAD_DIM
ATTN_WIDTH = N_GROUPS * GROUP_WIDTH
CONV_WIDTH = D_MODEL
CONV_K = 3
FFN_HIDDEN = 5632
DEPTH = 1
ALPHA = (2 * DEPTH) ** 0.25
LN_EPS = 1e-5
IN_COLS = 3 * ATTN_WIDTH + 3 * CONV_WIDTH + 2 * D_MODEL

REST_COL0 = 3 * ATTN_WIDTH
REST_U = 0
REST_C = REST_U + CONV_WIDTH
REST_B = REST_C + CONV_WIDTH
REST_GA = REST_B + CONV_WIDTH
REST_GC = REST_GA + D_MODEL
REST_COLS = REST_GC + D_MODEL

V7X_VMEM_BYTES = 64 * 1024 * 1024
BAND = 128
ATTN_TILES_IN_FLIGHT = 5
MAX_ROW_STRIDE = 4
PROJ_ROWS = 1024
PROJ_COLS = 1024

F32 = jnp.float32
BF16 = jnp.bfloat16


def _vmem_limit(pipelined_block_bytes, resident_bytes):
    need = 2 * pipelined_block_bytes + resident_bytes
    return min(int(need * 1.25), V7X_VMEM_BYTES - 4 * 1024 * 1024)


def _nbytes(shape, dtype):
    return math.prod(shape) * jnp.dtype(dtype).itemsize


def _qkv_proj_kernel(x_ref, w_ref, o_ref, xb_ref, y_ref, y2_ref, *, dilation):
    rows = x_ref.shape[0]
    f1 = min(dilation, MAX_ROW_STRIDE)
    f2 = dilation // f1
    assert f1 * f2 == dilation and f2 <= MAX_ROW_STRIDE

    @pl.when(pl.program_id(2) == 0)
    def _():
        xb_ref[...] = x_ref[...].astype(BF16)

    y = jnp.dot(xb_ref[...], w_ref[...], preferred_element_type=F32)
    for head in range(HEADS_PER_GROUP):
        y_head = y[:, head * HEAD_DIM:(head + 1) * HEAD_DIM]
        if dilation == 1:
            o_ref[head, 0] = y_head.astype(o_ref.dtype)
            continue
        src = y_ref.at[head]
        src[...] = y_head
        if f2 > 1:
            for r0 in range(f1):
                y2_ref[head, r0 * (rows // f1):(r0 + 1) * (rows // f1), :] = src[pl.ds(r0, rows // f1, stride=f1), :]
            src = y2_ref.at[head]
        for r0 in range(f1):
            for r1 in range(f2):
                start = r0 * (rows // f1) + r1 if f2 > 1 else r0
                stride = f2 if f2 > 1 else f1
                o_ref[head, r1 * f1 + r0] = src[pl.ds(start, rows // dilation, stride=stride), :].astype(o_ref.dtype)


def _qkv_proj(x, w_in, group, batch, seq):
    dilation = DILATED_GROUPS[group][1]
    length = seq // dilation
    tiles = seq // PROJ_ROWS
    per_residue = PROJ_ROWS // dilation
    assert PROJ_COLS == GROUP_WIDTH
    sections = ATTN_WIDTH // PROJ_COLS
    blocks = (_nbytes((PROJ_ROWS, D_MODEL), F32) + _nbytes((D_MODEL, PROJ_COLS), BF16)
              + _nbytes((PROJ_ROWS, PROJ_COLS), BF16))
    out = pl.pallas_call(
        functools.partial(_qkv_proj_kernel, dilation=dilation),
        grid=(batch, tiles, 3),
        in_specs=[pl.BlockSpec((PROJ_ROWS, D_MODEL), lambda b, t, j: (b * tiles + t, 0)),
                  pl.BlockSpec((D_MODEL, PROJ_COLS), lambda b, t, j: (0, j * sections + group))],
        out_specs=pl.BlockSpec((HEADS_PER_GROUP, None, dilation, per_residue, HEAD_DIM),
                               lambda b, t, j: (j, b, 0, t, 0)),
        out_shape=jax.ShapeDtypeStruct((3 * HEADS_PER_GROUP, batch, dilation, length, HEAD_DIM), BF16),
        scratch_shapes=[pltpu.VMEM((PROJ_ROWS, D_MODEL), BF16),
                        pltpu.VMEM((HEADS_PER_GROUP, PROJ_ROWS, HEAD_DIM), F32),
                        pltpu.VMEM((HEADS_PER_GROUP, PROJ_ROWS, HEAD_DIM), F32)],
        compiler_params=pltpu.CompilerParams(
            dimension_semantics=("parallel", "parallel", "arbitrary"),
            vmem_limit_bytes=_vmem_limit(blocks, _nbytes((PROJ_ROWS, D_MODEL), BF16)
                                         + 2 * _nbytes((PROJ_ROWS, PROJ_COLS), F32))),
        name=f"qkv_proj_dilation{dilation}",
    )(x, w_in)
    return out.reshape(3 * HEADS_PER_GROUP, batch, seq, HEAD_DIM)


def _rest_proj_kernel(x_ref, w_ref, o_ref, xb_ref):
    @pl.when(pl.program_id(1) == 0)
    def _():
        xb_ref[...] = x_ref[...].astype(BF16)

    o_ref[...] = jnp.dot(xb_ref[...], w_ref[...], preferred_element_type=F32).astype(o_ref.dtype)


def _rest_proj(x, w_in):
    m = x.shape[0]
    col0 = REST_COL0 // PROJ_COLS
    blocks = (_nbytes((PROJ_ROWS, D_MODEL), F32) + _nbytes((D_MODEL, PROJ_COLS), BF16)
              + _nbytes((PROJ_ROWS, PROJ_COLS), BF16))
    return pl.pallas_call(
        _rest_proj_kernel,
        grid=(m // PROJ_ROWS, REST_COLS // PROJ_COLS),
        in_specs=[pl.BlockSpec((PROJ_ROWS, D_MODEL), lambda i, j: (i, 0)),
                  pl.BlockSpec((D_MODEL, PROJ_COLS), lambda i, j: (0, col0 + j))],
        out_specs=pl.BlockSpec((PROJ_ROWS, PROJ_COLS), lambda i, j: (i, j)),
        out_shape=jax.ShapeDtypeStruct((m, REST_COLS), BF16),
        scratch_shapes=[pltpu.VMEM((PROJ_ROWS, D_MODEL), BF16)],
        compiler_params=pltpu.CompilerParams(
            dimension_semantics=("parallel", "arbitrary"),
            vmem_limit_bytes=_vmem_limit(blocks, _nbytes((PROJ_ROWS, D_MODEL), BF16)
                                         + _nbytes((PROJ_ROWS, PROJ_COLS), F32))),
        name="rest_proj",
    )(x, w_in)


def _attn_kernel(q1_ref, k1_ref, v1_ref, q2_ref, k2_ref, v2_ref, q3_ref, k3_ref, v3_ref,
                 o_ref, o_nat_ref, lse_nat_ref, bias_ref):
    seq = q1_ref.shape[0]
    n_tiles = seq // BAND
    scale = HEAD_DIM ** -0.5
    qi = lax.broadcasted_iota(jnp.int32, (BAND, 2 * BAND), 0)
    kj = lax.broadcasted_iota(jnp.int32, (BAND, 2 * BAND), 1)
    visible = ((kj < BAND) & (qi <= kj)) | ((kj >= BAND) & (qi >= kj - BAND))
    bias_ref[...] = jnp.where(visible, 0.0, -jnp.inf)

    def probabilities(q_ref, k_ref, n, has_prev):
        q = q_ref[n * BAND:(n + 1) * BAND, :]
        keys = slice((n - 1) * BAND, (n + 1) * BAND) if has_prev else slice(n * BAND, (n + 1) * BAND)
        bias = bias_ref[...] if has_prev else bias_ref[:, BAND:]
        s = lax.dot_general(q, k_ref[keys, :], (((1,), (1,)), ((), ())),
                            preferred_element_type=F32) * scale + bias
        m = s.max(-1, keepdims=True)
        p = jnp.exp(s - m)
        l = p.sum(-1, keepdims=True)
        return p.astype(BF16), l, m + jnp.log(l), keys

    def finish(g, v_ref, n, p, l, lse, keys):
        dilation = DILATED_GROUPS[g][1]
        residue, t = divmod(n, n_tiles // dilation)
        rows = pl.ds(t * BAND * dilation + residue, BAND, stride=dilation)
        o_nat_ref[g, rows, :] = jnp.dot(p, v_ref[keys, :], preferred_element_type=F32) / l
        lse_nat_ref[g, rows, :] = jnp.broadcast_to(lse, (BAND, HEAD_DIM))

    groups = ((q1_ref, k1_ref, v1_ref), (q2_ref, k2_ref, v2_ref), (q3_ref, k3_ref, v3_ref))
    in_flight = []
    for g, (q_ref, k_ref, v_ref) in enumerate(groups):
        tiles_per_residue = n_tiles // DILATED_GROUPS[g][1]
        for n in range(n_tiles):
            has_prev = n % tiles_per_residue > 0
            in_flight.append((g, v_ref, n) + probabilities(q_ref, k_ref, n, has_prev))
            if len(in_flight) > ATTN_TILES_IN_FLIGHT:
                finish(*in_flight.pop(0))
    for pending in in_flight:
        finish(*pending)

    def mix(c, _):
        rows = pl.ds(pl.multiple_of(c * BAND, BAND), BAND)
        lse = [lse_nat_ref[g, rows, :] for g in range(N_GROUPS)]
        top = jnp.maximum(jnp.maximum(lse[0], lse[1]), lse[2])
        w = [jnp.exp(x - top) for x in lse]
        num = w[0] * o_nat_ref[0, rows, :] + w[1] * o_nat_ref[1, rows, :] + w[2] * o_nat_ref[2, rows, :]
        o_ref[rows, :] = (num / (w[0] + w[1] + w[2])).astype(o_ref.dtype)
        return 0

    lax.fori_loop(0, n_tiles, mix, 0)


def _attention(qkv, batch, seq):
    heads = HEADS_PER_GROUP

    def head_spec(section):
        return pl.BlockSpec((None, None, seq, HEAD_DIM), lambda b, h: (section * heads + h, b, 0, 0))

    qkv_specs = [head_spec(0), head_spec(1), head_spec(2)]
    scratch = 2 * _nbytes((N_GROUPS, seq, HEAD_DIM), F32)
    return pl.pallas_call(
        _attn_kernel,
        grid=(batch, heads),
        in_specs=qkv_specs * N_GROUPS,
        out_specs=pl.BlockSpec((seq, HEAD_DIM), lambda b, h: (b, h)),
        out_shape=jax.ShapeDtypeStruct((batch * seq, GROUP_WIDTH), BF16),
        scratch_shapes=[pltpu.VMEM((N_GROUPS, seq, HEAD_DIM), F32),
                        pltpu.VMEM((N_GROUPS, seq, HEAD_DIM), F32),
                        pltpu.VMEM((BAND, 2 * BAND), F32)],
        compiler_params=pltpu.CompilerParams(
            dimension_semantics=("parallel", "parallel"),
            vmem_limit_bytes=_vmem_limit(10 * _nbytes((seq, HEAD_DIM), BF16), scratch + 4 * 1024 * 1024)),
        name="dilated_attention",
    )(*[a for g in range(N_GROUPS) for a in (qkv[g],) * 3])


def _conv_kernel(u_ref, c_ref, b_ref, w_ref, o_ref):
    z = c_ref[...].astype(F32) * u_ref[...].astype(F32)
    w = w_ref[...]
    row = lax.broadcasted_iota(jnp.int32, z.shape, 0)
    y = w[0:1, :] * z
    for tap in range(1, CONV_K):
        shifted = jnp.where(row >= tap, pltpu.roll(z, tap, axis=0), 0.0)
        y = y + w[tap:tap + 1, :] * shifted
    o_ref[...] = (b_ref[...].astype(F32) * y).astype(o_ref.dtype)


def _gated_conv(rest, conv_w, batch, seq, *, tn):
    def col_spec(col0):
        return pl.BlockSpec((seq, tn), lambda b, j: (b, col0 // tn + j))

    blocks = 4 * _nbytes((seq, tn), BF16) + _nbytes((CONV_K, tn), F32)
    return pl.pallas_call(
        _conv_kernel,
        grid=(batch, CONV_WIDTH // tn),
        in_specs=[col_spec(REST_U), col_spec(REST_C), col_spec(REST_B),
                  pl.BlockSpec((CONV_K, tn), lambda b, j: (0, j))],
        out_specs=pl.BlockSpec((seq, tn), lambda b, j: (b, j)),
        out_shape=jax.ShapeDtypeStruct((batch * seq, CONV_WIDTH), BF16),
        compiler_params=pltpu.CompilerParams(
            dimension_semantics=("parallel", "parallel"),
            vmem_limit_bytes=_vmem_limit(blocks, 6 * _nbytes((seq, tn), F32))),
        name="gated_conv",
    )(rest, rest, rest, conv_w)


def _merge_kernel(attn_ref, conv_ref, ga_ref, gc_ref, wao_ref, wco_ref, o_ref):
    a = jnp.dot(attn_ref[...], wao_ref[...], preferred_element_type=F32)
    c = jnp.dot(conv_ref[...], wco_ref[...], preferred_element_type=F32)
    ga = jax.nn.sigmoid(ga_ref[...].astype(F32))
    gc = jax.nn.sigmoid(gc_ref[...].astype(F32))
    o_ref[...] = (ga * a + gc * c).astype(o_ref.dtype)


def _gated_merge(attn, conv, rest, w_attn_o, w_conv_o, *, tm, tn):
    m = attn.shape[0]
    blocks = (_nbytes((tm, GROUP_WIDTH), BF16) + _nbytes((tm, CONV_WIDTH), BF16)
              + 3 * _nbytes((tm, tn), BF16) + _nbytes((GROUP_WIDTH + CONV_WIDTH, tn), BF16))
    return pl.pallas_call(
        _merge_kernel,
        grid=(m // tm, D_MODEL // tn),
        in_specs=[pl.BlockSpec((tm, GROUP_WIDTH), lambda i, j: (i, 0)),
                  pl.BlockSpec((tm, CONV_WIDTH), lambda i, j: (i, 0)),
                  pl.BlockSpec((tm, tn), lambda i, j: (i, REST_GA // tn + j)),
                  pl.BlockSpec((tm, tn), lambda i, j: (i, REST_GC // tn + j)),
                  pl.BlockSpec((GROUP_WIDTH, tn), lambda i, j: (0, j)),
                  pl.BlockSpec((CONV_WIDTH, tn), lambda i, j: (0, j))],
        out_specs=pl.BlockSpec((tm, tn), lambda i, j: (i, j)),
        out_shape=jax.ShapeDtypeStruct((m, D_MODEL), BF16),
        compiler_params=pltpu.CompilerParams(
            dimension_semantics=("parallel", "arbitrary"),
            vmem_limit_bytes=_vmem_limit(blocks, 4 * _nbytes((tm, tn), F32))),
        name="gated_merge",
    )(attn, conv, rest, rest, w_attn_o, w_conv_o)


def _layer_norm(z, g, b):
    mu = jnp.mean(z, axis=-1, keepdims=True)
    zc = z - mu
    var = jnp.mean(zc * zc, axis=-1, keepdims=True)
    return zc * lax.rsqrt(var + LN_EPS) * g + b


def _out_ln_kernel(mi_ref, w_ref, x_ref, g_ref, b_ref, o_ref):
    y = jnp.dot(mi_ref[...], w_ref[...], preferred_element_type=F32)
    o_ref[...] = _layer_norm(ALPHA * x_ref[...] + y, g_ref[...], b_ref[...])


def _out_proj_ln(merged, w_out, x, g, b, *, tm):
    m = merged.shape[0]
    blocks = (_nbytes((tm, D_MODEL), BF16) + _nbytes((D_MODEL, D_MODEL), BF16)
              + 2 * _nbytes((tm, D_MODEL), F32))
    row = pl.BlockSpec((tm, D_MODEL), lambda i: (i, 0))
    vec = pl.BlockSpec((1, D_MODEL), lambda i: (0, 0))
    return pl.pallas_call(
        _out_ln_kernel,
        grid=(m // tm,),
        in_specs=[row, pl.BlockSpec((D_MODEL, D_MODEL), lambda i: (0, 0)), row, vec, vec],
        out_specs=row,
        out_shape=jax.ShapeDtypeStruct((m, D_MODEL), F32),
        compiler_params=pltpu.CompilerParams(
            dimension_semantics=("parallel",),
            vmem_limit_bytes=_vmem_limit(blocks, 3 * _nbytes((tm, D_MODEL), F32))),
        name="out_proj_ln",
    )(merged, w_out, x, g, b)


def _ffn_kernel(h_ref, wg_ref, wu_ref, wd_ref, g_ref, b_ref, o_ref, hb_ref):
    f = pl.program_id(1)

    @pl.when(f == 0)
    def _():
        hb_ref[...] = h_ref[...].astype(BF16)
        o_ref[...] = jnp.zeros_like(o_ref)

    hb = hb_ref[...]
    gate = jnp.dot(hb, wg_ref[...], preferred_element_type=F32)
    up = jnp.dot(hb, wu_ref[...], preferred_element_type=F32)
    hidden = (jax.nn.silu(gate) * up).astype(BF16)
    o_ref[...] += jnp.dot(hidden, wd_ref[...], preferred_element_type=F32)

    @pl.when(f == pl.num_programs(1) - 1)
    def _():
        o_ref[...] = _layer_norm(ALPHA * h_ref[...] + o_ref[...], g_ref[...], b_ref[...])


def _ffn_ln(h, w_gate, w_up, w_down, g, b, *, tm, tf):
    m = h.shape[0]
    blocks = (2 * _nbytes((tm, D_MODEL), F32) + 2 * _nbytes((D_MODEL, tf), BF16)
              + _nbytes((tf, D_MODEL), BF16))
    scratch = _nbytes((tm, D_MODEL), BF16)
    row = pl.BlockSpec((tm, D_MODEL), lambda i, f: (i, 0))
    vec = pl.BlockSpec((1, D_MODEL), lambda i, f: (0, 0))
    return pl.pallas_call(
        _ffn_kernel,
        grid=(m // tm, FFN_HIDDEN // tf),
        in_specs=[row,
                  pl.BlockSpec((D_MODEL, tf), lambda i, f: (0, f)),
                  pl.BlockSpec((D_MODEL, tf), lambda i, f: (0, f)),
                  pl.BlockSpec((tf, D_MODEL), lambda i, f: (f, 0)),
                  vec, vec],
        out_specs=row,
        out_shape=jax.ShapeDtypeStruct((m, D_MODEL), F32),
        scratch_shapes=[pltpu.VMEM((tm, D_MODEL), BF16)],
        compiler_params=pltpu.CompilerParams(
            dimension_semantics=("parallel", "arbitrary"),
            vmem_limit_bytes=_vmem_limit(blocks, scratch + 4 * _nbytes((tm, tf), F32)
                                         + _nbytes((tm, D_MODEL), F32))),
        name="ffn_ln",
    )(h, w_gate, w_up, w_down, g, b)


def kernel(x, w_in, conv_w, w_attn_o, w_conv_o, w_out, ln1_g, ln1_b,
           w_ffn_gate, w_ffn_up, w_ffn_down, ln2_g, ln2_b):
    batch, seq, d = x.shape
    assert d == D_MODEL and w_in.shape == (DEPTH, D_MODEL, IN_COLS)
    assert seq % PROJ_ROWS == 0 and all(PROJ_ROWS % (dil * 16) == 0 for _, dil in DILATED_GROUPS)
    assert all(seq % (dil * BAND) == 0 for _, dil in DILATED_GROUPS)
    h = x.reshape(batch * seq, d)
    for layer in range(DEPTH):
        w_in_b = w_in[layer].astype(BF16)
        qkv = [_qkv_proj(h, w_in_b, g, batch, seq) for g in range(N_GROUPS)]
        rest = _rest_proj(h, w_in_b)
        attn = _attention(qkv, batch, seq)
        conv = _gated_conv(rest, conv_w[layer], batch, seq, tn=512)
        merged = _gated_merge(attn, conv, rest, w_attn_o[layer].astype(BF16),
                              w_conv_o[layer].astype(BF16), tm=1024, tn=512)
        h = _out_proj_ln(merged, w_out[layer].astype(BF16), h,
                         ln1_g[layer][None, :], ln1_b[layer][None, :], tm=512)
        h = _ffn_ln(h, w_ffn_gate[layer].astype(BF16), w_ffn_up[layer].astype(BF16),
                    w_ffn_down[layer].astype(BF16), ln2_g[layer][None, :], ln2_b[layer][None, :],
                    tm=1024, tf=256)
    return h.reshape(batch, seq, d)
```

```python
import functools
import math

import jax
import jax.numpy as jnp
from jax import lax
from jax.experimental import pallas as pl
from jax.experimental.pallas import tpu as pltpu

D_MODEL = 2048
HEAD_DIM = 128
HEADS_PER_GROUP = 8
DILATED_GROUPS = ((128, 1), (512, 4), (2048, 16))
N_GROUPS = len(DILATED_GROUPS)
GROUP_WIDTH = HEADS_PER_GROUP * HEAD_DIM
ATTN_WIDTH = N_GROUPS * GROUP_WIDTH
CONV_WIDTH = D_MODEL
CONV_K = 3
FFN_HIDDEN = 5632
DEPTH = 1
ALPHA = (2 * DEPTH) ** 0.25
LN_EPS = 1e-5

COL_U = 3 * ATTN_WIDTH
COL_C = COL_U + CONV_WIDTH
COL_B = COL_C + CONV_WIDTH
COL_GA = COL_B + CONV_WIDTH
COL_GC = COL_GA + D_MODEL
IN_COLS = COL_GC + D_MODEL

V7X_VMEM_BYTES = 64 * 1024 * 1024
BAND = 128
ATTN_TILES_IN_FLIGHT = 8
MAX_ROW_STRIDE = 4
PROJ_ROWS = 1024
PROJ_COLS = 1024

F32 = jnp.float32
BF16 = jnp.bfloat16


def _vmem_limit(pipelined_block_bytes, resident_bytes):
    need = 2 * pipelined_block_bytes + resident_bytes
    return min(int(need * 1.25), V7X_VMEM_BYTES - 4 * 1024 * 1024)


def _nbytes(shape, dtype):
    return math.prod(shape) * jnp.dtype(dtype).itemsize


def _sigmoid(x):
    return 0.5 * (jnp.tanh(0.5 * x) + 1.0)


def _store_heads_by_residue(y, o_ref, y_ref, y2_ref, dilation):
    rows = y.shape[0]
    f1 = min(dilation, MAX_ROW_STRIDE)
    f2 = dilation // f1
    assert f1 * f2 == dilation and f2 <= MAX_ROW_STRIDE
    for head in range(HEADS_PER_GROUP):
        y_head = y[:, head * HEAD_DIM:(head + 1) * HEAD_DIM]
        if dilation == 1:
            o_ref[head, 0] = y_head.astype(o_ref.dtype)
            continue
        src = y_ref.at[head]
        src[...] = y_head
        if f2 > 1:
            for r0 in range(f1):
                y2_ref[head, r0 * (rows // f1):(r0 + 1) * (rows // f1), :] = src[pl.ds(r0, rows // f1, stride=f1), :]
            src = y2_ref.at[head]
        for r0 in range(f1):
            for r1 in range(f2):
                start = r0 * (rows // f1) + r1 if f2 > 1 else r0
                stride = f2 if f2 > 1 else f1
                o_ref[head, r1 * f1 + r0] = src[pl.ds(start, rows // dilation, stride=stride), :].astype(o_ref.dtype)


def _qkv_proj_first_kernel(x_ref, w_ref, o_ref, xb_ref):
    @pl.when(pl.program_id(2) == 0)
    def _():
        xb_ref[...] = x_ref[...].astype(BF16)

    y = jnp.dot(xb_ref[...], w_ref[...], preferred_element_type=F32)
    _store_heads_by_residue(y, o_ref, None, None, 1)


def _qkv_proj_kernel(xb_ref, w_ref, o_ref, y_ref, y2_ref, *, dilation):
    y = jnp.dot(xb_ref[...], w_ref[...], preferred_element_type=F32)
    _store_heads_by_residue(y, o_ref, y_ref, y2_ref, dilation)


def _qkv_proj(x, w_in, group, batch, seq):
    dilation = DILATED_GROUPS[group][1]
    first = x.dtype == F32
    assert first == (group == 0) and dilation == (1 if first else dilation)
    length = seq // dilation
    tiles = seq // PROJ_ROWS
    per_residue = PROJ_ROWS // dilation
    assert PROJ_COLS == GROUP_WIDTH
    sections = ATTN_WIDTH // PROJ_COLS
    row_spec = pl.BlockSpec((PROJ_ROWS, D_MODEL), lambda b, t, j: (b * tiles + t, 0))
    qkv_spec = pl.BlockSpec((HEADS_PER_GROUP, None, dilation, per_residue, HEAD_DIM),
                            lambda b, t, j: (j, b, 0, t, 0))
    qkv_shape = jax.ShapeDtypeStruct((3 * HEADS_PER_GROUP, batch, dilation, length, HEAD_DIM), BF16)
    blocks = (_nbytes((PROJ_ROWS, D_MODEL), x.dtype) + _nbytes((D_MODEL, PROJ_COLS), BF16)
              + _nbytes((PROJ_ROWS, PROJ_COLS), BF16) + (_nbytes((PROJ_ROWS, D_MODEL), BF16) if first else 0))
    regroup = [] if first else [pltpu.VMEM((HEADS_PER_GROUP, PROJ_ROWS, HEAD_DIM), F32)] * 2
    out = pl.pallas_call(
        _qkv_proj_first_kernel if first else functools.partial(_qkv_proj_kernel, dilation=dilation),
        grid=(batch, tiles, 3),
        in_specs=[row_spec,
                  pl.BlockSpec((D_MODEL, PROJ_COLS), lambda b, t, j: (0, j * sections + group))],
        out_specs=[qkv_spec, row_spec] if first else qkv_spec,
        out_shape=[qkv_shape, jax.ShapeDtypeStruct(x.shape, BF16)] if first else qkv_shape,
        scratch_shapes=regroup,
        compiler_params=pltpu.CompilerParams(
            dimension_semantics=("parallel", "parallel", "arbitrary"),
            vmem_limit_bytes=_vmem_limit(blocks, (0 if first else 2 * _nbytes((PROJ_ROWS, PROJ_COLS), F32))
                                         + 2 * _nbytes((PROJ_ROWS, PROJ_COLS), F32))),
        name=f"qkv_proj_dilation{dilation}",
    )(x, w_in)
    if first:
        return out[0].reshape(3 * HEADS_PER_GROUP, batch, seq, HEAD_DIM), out[1]
    return out.reshape(3 * HEADS_PER_GROUP, batch, seq, HEAD_DIM)


def _conv_proj_kernel(x_ref, wu_ref, wc_ref, wb_ref, cw_ref, o_ref):
    x = x_ref[...]
    z = (jnp.dot(x, wc_ref[...], preferred_element_type=F32)
         * jnp.dot(x, wu_ref[...], preferred_element_type=F32))
    cw = cw_ref[...]
    row = lax.broadcasted_iota(jnp.int32, z.shape, 0)
    y = cw[0:1, :] * z
    for tap in range(1, CONV_K):
        shifted = jnp.where(row >= tap, pltpu.roll(z, tap, axis=0), 0.0)
        y = y + cw[tap:tap + 1, :] * shifted
    o_ref[...] = (jnp.dot(x, wb_ref[...], preferred_element_type=F32) * y).astype(o_ref.dtype)


def _conv_proj(xb, w_in, conv_w, batch, seq, *, tn):
    def w_spec(col0):
        return pl.BlockSpec((D_MODEL, tn), lambda b, j: (0, col0 // tn + j))

    blocks = (_nbytes((seq, D_MODEL), BF16) + 3 * _nbytes((D_MODEL, tn), BF16)
              + _nbytes((CONV_K, tn), F32) + _nbytes((seq, tn), BF16))
    return pl.pallas_call(
        _conv_proj_kernel,
        grid=(batch, CONV_WIDTH // tn),
        in_specs=[pl.BlockSpec((seq, D_MODEL), lambda b, j: (b, 0)),
                  w_spec(COL_U), w_spec(COL_C), w_spec(COL_B),
                  pl.BlockSpec((CONV_K, tn), lambda b, j: (0, j))],
        out_specs=pl.BlockSpec((seq, tn), lambda b, j: (b, j)),
        out_shape=jax.ShapeDtypeStruct((batch * seq, CONV_WIDTH), BF16),
        compiler_params=pltpu.CompilerParams(
            dimension_semantics=("parallel", "arbitrary"),
            vmem_limit_bytes=_vmem_limit(blocks, 5 * _nbytes((seq, tn), F32))),
        name="conv_proj",
    )(xb, w_in, w_in, w_in, conv_w)


def _attn_kernel(q1_ref, k1_ref, v1_ref, q2_ref, k2_ref, v2_ref, q3_ref, k3_ref, v3_ref,
                 o_ref, o_nat_ref, lse_nat_ref, bias_ref):
    seq = q1_ref.shape[0]
    n_tiles = seq // BAND
    scale = HEAD_DIM ** -0.5
    qi = lax.broadcasted_iota(jnp.int32, (BAND, 2 * BAND), 0)
    kj = lax.broadcasted_iota(jnp.int32, (BAND, 2 * BAND), 1)
    visible = ((kj < BAND) & (qi <= kj)) | ((kj >= BAND) & (qi >= kj - BAND))
    bias_ref[...] = jnp.where(visible, 0.0, -jnp.inf)

    def probabilities(q_ref, k_ref, n, has_prev):
        q = q_ref[n * BAND:(n + 1) * BAND, :]
        keys = slice((n - 1) * BAND, (n + 1) * BAND) if has_prev else slice(n * BAND, (n + 1) * BAND)
        bias = bias_ref[...] if has_prev else bias_ref[:, BAND:]
        s = lax.dot_general(q, k_ref[keys, :], (((1,), (1,)), ((), ())),
                            preferred_element_type=F32) * scale + bias
        m = s.max(-1, keepdims=True)
        p = jnp.exp(s - m)
        l = p.sum(-1, keepdims=True)
        return p.astype(BF16), l, m + jnp.log(l), keys

    def finish(g, v_ref, n, p, l, lse, keys):
        dilation = DILATED_GROUPS[g][1]
        residue, t = divmod(n, n_tiles // dilation)
        rows = pl.ds(t * BAND * dilation + residue, BAND, stride=dilation)
        o_nat_ref[g, rows, :] = jnp.dot(p, v_ref[keys, :], preferred_element_type=F32) / l
        lse_nat_ref[g, rows, :] = jnp.broadcast_to(lse, (BAND, HEAD_DIM))

    groups = ((q1_ref, k1_ref, v1_ref), (q2_ref, k2_ref, v2_ref), (q3_ref, k3_ref, v3_ref))
    in_flight = []
    for g, (q_ref, k_ref, v_ref) in enumerate(groups):
        tiles_per_residue = n_tiles // DILATED_GROUPS[g][1]
        for n in range(n_tiles):
            has_prev = n % tiles_per_residue > 0
            in_flight.append((g, v_ref, n) + probabilities(q_ref, k_ref, n, has_prev))
            if len(in_flight) > ATTN_TILES_IN_FLIGHT:
                finish(*in_flight.pop(0))
    for pending in in_flight:
        finish(*pending)

    def mix(c, _):
        rows = pl.ds(pl.multiple_of(c * BAND, BAND), BAND)
        lse = [lse_nat_ref[g, rows, :] for g in range(N_GROUPS)]
        top = jnp.maximum(jnp.maximum(lse[0], lse[1]), lse[2])
        w = [jnp.exp(x - top) for x in lse]
        num = w[0] * o_nat_ref[0, rows, :] + w[1] * o_nat_ref[1, rows, :] + w[2] * o_nat_ref[2, rows, :]
        o_ref[rows, :] = (num / (w[0] + w[1] + w[2])).astype(o_ref.dtype)
        return 0

    lax.fori_loop(0, n_tiles, mix, 0)


def _attention(qkv, batch, seq):
    heads = HEADS_PER_GROUP

    def head_spec(section):
        return pl.BlockSpec((None, None, seq, HEAD_DIM), lambda b, h: (section * heads + h, b, 0, 0))

    qkv_specs = [head_spec(0), head_spec(1), head_spec(2)]
    scratch = 2 * _nbytes((N_GROUPS, seq, HEAD_DIM), F32)
    return pl.pallas_call(
        _attn_kernel,
        grid=(batch, heads),
        in_specs=qkv_specs * N_GROUPS,
        out_specs=pl.BlockSpec((seq, HEAD_DIM), lambda b, h: (b, h)),
        out_shape=jax.ShapeDtypeStruct((batch * seq, GROUP_WIDTH), BF16),
        scratch_shapes=[pltpu.VMEM((N_GROUPS, seq, HEAD_DIM), F32),
                        pltpu.VMEM((N_GROUPS, seq, HEAD_DIM), F32),
                        pltpu.VMEM((BAND, 2 * BAND), F32)],
        compiler_params=pltpu.CompilerParams(
            dimension_semantics=("parallel", "parallel"),
            vmem_limit_bytes=_vmem_limit(10 * _nbytes((seq, HEAD_DIM), BF16), scratch + 4 * 1024 * 1024)),
        name="dilated_attention",
    )(*[a for g in range(N_GROUPS) for a in (qkv[g],) * 3])


def _merge_kernel(attn_ref, conv_ref, x_ref, wao_ref, wco_ref, wga_ref, wgc_ref, o_ref):
    x = x_ref[...]
    a = jnp.dot(attn_ref[...], wao_ref[...], preferred_element_type=F32)
    ga = _sigmoid(jnp.dot(x, wga_ref[...], preferred_element_type=F32))
    c = jnp.dot(conv_ref[...], wco_ref[...], preferred_element_type=F32)
    gc = _sigmoid(jnp.dot(x, wgc_ref[...], preferred_element_type=F32))
    o_ref[...] = (ga * a + gc * c).astype(o_ref.dtype)


def _gated_merge(attn, conv, xb, w_attn_o, w_conv_o, w_in, *, tm, tn):
    m = attn.shape[0]
    blocks = (_nbytes((tm, GROUP_WIDTH), BF16) + _nbytes((tm, CONV_WIDTH), BF16)
              + _nbytes((tm, D_MODEL), BF16) + _nbytes((tm, tn), BF16)
              + _nbytes((GROUP_WIDTH + CONV_WIDTH + 2 * D_MODEL, tn), BF16))
    return pl.pallas_call(
        _merge_kernel,
        grid=(m // tm, D_MODEL // tn),
        in_specs=[pl.BlockSpec((tm, GROUP_WIDTH), lambda i, j: (i, 0)),
                  pl.BlockSpec((tm, CONV_WIDTH), lambda i, j: (i, 0)),
                  pl.BlockSpec((tm, D_MODEL), lambda i, j: (i, 0)),
                  pl.BlockSpec((GROUP_WIDTH, tn), lambda i, j: (0, j)),
                  pl.BlockSpec((CONV_WIDTH, tn), lambda i, j: (0, j)),
                  pl.BlockSpec((D_MODEL, tn), lambda i, j: (0, COL_GA // tn + j)),
                  pl.BlockSpec((D_MODEL, tn), lambda i, j: (0, COL_GC // tn + j))],
        out_specs=pl.BlockSpec((tm, tn), lambda i, j: (i, j)),
        out_shape=jax.ShapeDtypeStruct((m, D_MODEL), BF16),
        compiler_params=pltpu.CompilerParams(
            dimension_semantics=("parallel", "arbitrary"),
            vmem_limit_bytes=_vmem_limit(blocks, 5 * _nbytes((tm, tn), F32))),
        name="gated_merge",
    )(attn, conv, xb, w_attn_o, w_conv_o, w_in, w_in)


def _layer_norm(z, g, b):
    mu = jnp.mean(z, axis=-1, keepdims=True)
    zc = z - mu
    var = jnp.mean(zc * zc, axis=-1, keepdims=True)
    return zc * lax.rsqrt(var + LN_EPS) * g + b


def _out_ln_kernel(mi_ref, w_ref, x_ref, g_ref, b_ref, o_ref):
    y = jnp.dot(mi_ref[...], w_ref[...], preferred_element_type=F32)
    o_ref[...] = _layer_norm(ALPHA * x_ref[...] + y, g_ref[...], b_ref[...])


def _out_proj_ln(merged, w_out, x, g, b, *, tm):
    m = merged.shape[0]
    blocks = (_nbytes((tm, D_MODEL), BF16) + _nbytes((D_MODEL, D_MODEL), BF16)
              + 2 * _nbytes((tm, D_MODEL), F32))
    row = pl.BlockSpec((tm, D_MODEL), lambda i: (i, 0))
    vec = pl.BlockSpec((1, D_MODEL), lambda i: (0, 0))
    return pl.pallas_call(
        _out_ln_kernel,
        grid=(m // tm,),
        in_specs=[row, pl.BlockSpec((D_MODEL, D_MODEL), lambda i: (0, 0)), row, vec, vec],
        out_specs=row,
        out_shape=jax.ShapeDtypeStruct((m, D_MODEL), F32),
        compiler_params=pltpu.CompilerParams(
            dimension_semantics=("parallel",),
            vmem_limit_bytes=_vmem_limit(blocks, 3 * _nbytes((tm, D_MODEL), F32))),
        name="out_proj_ln",
    )(merged, w_out, x, g, b)


def _ffn_kernel(h_ref, wg_ref, wu_ref, wd_ref, g_ref, b_ref, o_ref, hb_ref):
    f = pl.program_id(1)

    @pl.when(f == 0)
    def _():
        hb_ref[...] = h_ref[...].astype(BF16)
        o_ref[...] = jnp.zeros_like(o_ref)

    hb = hb_ref[...]
    gate = jnp.dot(hb, wg_ref[...], preferred_element_type=F32)
    up = jnp.dot(hb, wu_ref[...], preferred_element_type=F32)
    hidden = (gate * _sigmoid(gate) * up).astype(BF16)
    o_ref[...] += jnp.dot(hidden, wd_ref[...], preferred_element_type=F32)

    @pl.when(f == pl.num_programs(1) - 1)
    def _():
        o_ref[...] = _layer_norm(ALPHA * h_ref[...] + o_ref[...], g_ref[...], b_ref[...])


def _ffn_ln(h, w_gate, w_up, w_down, g, b, *, tm, tf):
    m = h.shape[0]
    blocks = (2 * _nbytes((tm, D_MODEL), F32) + 2 * _nbytes((D_MODEL, tf), BF16)
              + _nbytes((tf, D_MODEL), BF16))
    scratch = _nbytes((tm, D_MODEL), BF16)
    row = pl.BlockSpec((tm, D_MODEL), lambda i, f: (i, 0))
    vec = pl.BlockSpec((1, D_MODEL), lambda i, f: (0, 0))
    return pl.pallas_call(
        _ffn_kernel,
        grid=(m // tm, FFN_HIDDEN // tf),
        in_specs=[row,
                  pl.BlockSpec((D_MODEL, tf), lambda i, f: (0, f)),
                  pl.BlockSpec((D_MODEL, tf), lambda i, f: (0, f)),
                  pl.BlockSpec((tf, D_MODEL), lambda i, f: (f, 0)),
                  vec, vec],
        out_specs=row,
        out_shape=jax.ShapeDtypeStruct((m, D_MODEL), F32),
        scratch_shapes=[pltpu.VMEM((tm, D_MODEL), BF16)],
        compiler_params=pltpu.CompilerParams(
            dimension_semantics=("parallel", "arbitrary"),
            vmem_limit_bytes=_vmem_limit(blocks, scratch + 4 * _nbytes((tm, tf), F32)
                                         + _nbytes((tm, D_MODEL), F32))),
        name="ffn_ln",
    )(h, w_gate, w_up, w_down, g, b)


def kernel(x, w_in, conv_w, w_attn_o, w_conv_o, w_out, ln1_g, ln1_b,
           w_ffn_gate, w_ffn_up, w_ffn_down, ln2_g, ln2_b):
    batch, seq, d = x.shape
    assert d == D_MODEL and w_in.shape == (DEPTH, D_MODEL, IN_COLS)
    assert seq % PROJ_ROWS == 0 and all(PROJ_ROWS % (dil * 16) == 0 for _, dil in DILATED_GROUPS)
    assert all(seq % (dil * BAND) == 0 for _, dil in DILATED_GROUPS)
    h = x.reshape(batch * seq, d)
    for layer in range(DEPTH):
        w_in_b = w_in[layer].astype(BF16)
        qkv0, hb = _qkv_proj(h, w_in_b, 0, batch, seq)
        qkv = [qkv0] + [_qkv_proj(hb, w_in_b, g, batch, seq) for g in range(1, N_GROUPS)]
        conv = _conv_proj(hb, w_in_b, conv_w[layer], batch, seq, tn=512)
        attn = _attention(qkv, batch, seq)
        merged = _gated_merge(attn, conv, hb, w_attn_o[layer].astype(BF16),
                              w_conv_o[layer].astype(BF16), w_in_b, tm=1024, tn=512)
        h = _out_proj_ln(merged, w_out[layer].astype(BF16), h,
                         ln1_g[layer][None, :], ln1_b[layer][None, :], tm=512)
        h = _ffn_ln(h, w_ffn_gate[layer].astype(BF16), w_ffn_up[layer].astype(BF16),
                    w_ffn_down[layer].astype(BF16), ln2_g[layer][None, :], ln2_b[layer][None, :],
                    tm=1024, tf=256)
    return h.reshape(batch, seq, d)
```

```python
import functools
import math

import jax
import jax.numpy as jnp
from jax import lax
from jax.experimental import pallas as pl
from jax.experimental.pallas import tpu as pltpu

D_MODEL = 2048
HEAD_DIM = 128
HEADS_PER_GROUP = 8
DILATED_GROUPS = ((128, 1), (512, 4), (2048, 16))
N_GROUPS = len(DILATED_GROUPS)
GROUP_WIDTH = HEADS_PER_GROUP * HEAD_DIM
ATTN_WIDTH = N_GROUPS * GROUP_WIDTH
CONV_WIDTH = D_MODEL
CONV_K = 3
FFN_HIDDEN = 5632
DEPTH = 1
ALPHA = (2 * DEPTH) ** 0.25
LN_EPS = 1e-5

COL_U = 3 * ATTN_WIDTH
COL_C = COL_U + CONV_WIDTH
COL_B = COL_C + CONV_WIDTH
COL_GA = COL_B + CONV_WIDTH
COL_GC = COL_GA + D_MODEL
IN_COLS = COL_GC + D_MODEL

V7X_VMEM_BYTES = 64 * 1024 * 1024
BAND = 128
ATTN_TILES_IN_FLIGHT = 8
MAX_ROW_STRIDE = 4
PROJ_ROWS = 1024
PROJ_COLS = 1024

F32 = jnp.float32
BF16 = jnp.bfloat16


def _vmem_limit(pipelined_block_bytes, resident_bytes):
    need = 2 * pipelined_block_bytes + resident_bytes
    return min(int(need * 1.25), V7X_VMEM_BYTES - 4 * 1024 * 1024)


def _nbytes(shape, dtype):
    return math.prod(shape) * jnp.dtype(dtype).itemsize


def _sigmoid(x):
    return 0.5 * (jnp.tanh(0.5 * x) + 1.0)


def _store_heads_by_residue(y, o_ref, y_ref, y2_ref, dilation):
    rows = y.shape[0]
    f1 = min(dilation, MAX_ROW_STRIDE)
    f2 = dilation // f1
    assert f1 * f2 == dilation and f2 <= MAX_ROW_STRIDE
    for head in range(HEADS_PER_GROUP):
        y_head = y[:, head * HEAD_DIM:(head + 1) * HEAD_DIM]
        if dilation == 1:
            o_ref[head, 0] = y_head.astype(o_ref.dtype)
            continue
        src = y_ref.at[head]
        src[...] = y_head
        if f2 > 1:
            for r0 in range(f1):
                y2_ref[head, r0 * (rows // f1):(r0 + 1) * (rows // f1), :] = src[pl.ds(r0, rows // f1, stride=f1), :]
            src = y2_ref.at[head]
        for r0 in range(f1):
            for r1 in range(f2):
                start = r0 * (rows // f1) + r1 if f2 > 1 else r0
                stride = f2 if f2 > 1 else f1
                o_ref[head, r1 * f1 + r0] = src[pl.ds(start, rows // dilation, stride=stride), :].astype(o_ref.dtype)


def _qkv_proj_first_kernel(x_ref, w_ref, o_ref, xb_ref):
    @pl.when(pl.program_id(2) == 0)
    def _():
        xb_ref[...] = x_ref[...].astype(BF16)

    y = jnp.dot(xb_ref[...], w_ref[...], preferred_element_type=F32)
    _store_heads_by_residue(y, o_ref, None, None, 1)


def _qkv_proj_kernel(xb_ref, w_ref, o_ref, y_ref, y2_ref, *, dilation):
    y = jnp.dot(xb_ref[...], w_ref[...], preferred_element_type=F32)
    _store_heads_by_residue(y, o_ref, y_ref, y2_ref, dilation)


def _qkv_proj(x, w_in, group, batch, seq):
    dilation = DILATED_GROUPS[group][1]
    first = x.dtype == F32
    assert first == (group == 0) and dilation == (1 if first else dilation)
    length = seq // dilation
    tiles = seq // PROJ_ROWS
    per_residue = PROJ_ROWS // dilation
    assert PROJ_COLS == GROUP_WIDTH
    sections = ATTN_WIDTH // PROJ_COLS
    row_spec = pl.BlockSpec((PROJ_ROWS, D_MODEL), lambda b, t, j: (b * tiles + t, 0))
    qkv_spec = pl.BlockSpec((HEADS_PER_GROUP, None, None, dilation, per_residue, HEAD_DIM),
                            lambda b, t, j: (0, b, j, 0, t, 0))
    qkv_shape = jax.ShapeDtypeStruct((HEADS_PER_GROUP, batch, 3, dilation, length, HEAD_DIM), BF16)
    blocks = (_nbytes((PROJ_ROWS, D_MODEL), x.dtype) + _nbytes((D_MODEL, PROJ_COLS), BF16)
              + _nbytes((PROJ_ROWS, PROJ_COLS), BF16) + (_nbytes((PROJ_ROWS, D_MODEL), BF16) if first else 0))
    regroup = [] if first else [pltpu.VMEM((HEADS_PER_GROUP, PROJ_ROWS, HEAD_DIM), F32)] * 2
    out = pl.pallas_call(
        _qkv_proj_first_kernel if first else functools.partial(_qkv_proj_kernel, dilation=dilation),
        grid=(batch, tiles, 3),
        in_specs=[row_spec,
                  pl.BlockSpec((D_MODEL, PROJ_COLS), lambda b, t, j: (0, j * sections + group))],
        out_specs=[qkv_spec, row_spec] if first else qkv_spec,
        out_shape=[qkv_shape, jax.ShapeDtypeStruct(x.shape, BF16)] if first else qkv_shape,
        scratch_shapes=regroup,
        compiler_params=pltpu.CompilerParams(
            dimension_semantics=("parallel", "parallel", "arbitrary"),
            vmem_limit_bytes=_vmem_limit(blocks, (0 if first else 2 * _nbytes((PROJ_ROWS, PROJ_COLS), F32))
                                         + 2 * _nbytes((PROJ_ROWS, PROJ_COLS), F32))),
        name=f"qkv_proj_dilation{dilation}",
    )(x, w_in)
    if first:
        return out[0].reshape(HEADS_PER_GROUP, batch, 3, seq, HEAD_DIM), out[1]
    return out.reshape(HEADS_PER_GROUP, batch, 3, seq, HEAD_DIM)


def _conv_proj_kernel(x_ref, wu_ref, wc_ref, wb_ref, cw_ref, o_ref):
    x = x_ref[...]
    z = (jnp.dot(x, wc_ref[...], preferred_element_type=F32)
         * jnp.dot(x, wu_ref[...], preferred_element_type=F32))
    cw = cw_ref[...]
    row = lax.broadcasted_iota(jnp.int32, z.shape, 0)
    y = cw[0:1, :] * z
    for tap in range(1, CONV_K):
        shifted = jnp.where(row >= tap, pltpu.roll(z, tap, axis=0), 0.0)
        y = y + cw[tap:tap + 1, :] * shifted
    o_ref[...] = (jnp.dot(x, wb_ref[...], preferred_element_type=F32) * y).astype(o_ref.dtype)


def _conv_proj(xb, w_in, conv_w, batch, seq, *, tn):
    def w_spec(col0):
        return pl.BlockSpec((D_MODEL, tn), lambda b, j: (0, col0 // tn + j))

    blocks = (_nbytes((seq, D_MODEL), BF16) + 3 * _nbytes((D_MODEL, tn), BF16)
              + _nbytes((CONV_K, tn), F32) + _nbytes((seq, tn), BF16))
    return pl.pallas_call(
        _conv_proj_kernel,
        grid=(batch, CONV_WIDTH // tn),
        in_specs=[pl.BlockSpec((seq, D_MODEL), lambda b, j: (b, 0)),
                  w_spec(COL_U), w_spec(COL_C), w_spec(COL_B),
                  pl.BlockSpec((CONV_K, tn), lambda b, j: (0, j))],
        out_specs=pl.BlockSpec((seq, tn), lambda b, j: (b, j)),
        out_shape=jax.ShapeDtypeStruct((batch * seq, CONV_WIDTH), BF16),
        compiler_params=pltpu.CompilerParams(
            dimension_semantics=("parallel", "arbitrary"),
            vmem_limit_bytes=_vmem_limit(blocks, 5 * _nbytes((seq, tn), F32))),
        name="conv_proj",
    )(xb, w_in, w_in, w_in, conv_w)


def _attn_kernel(qkv1_ref, qkv2_ref, qkv3_ref, o_ref, o_nat_ref, lse_nat_ref, bias_ref):
    seq = qkv1_ref.shape[1]
    n_tiles = seq // BAND
    scale = HEAD_DIM ** -0.5
    qi = lax.broadcasted_iota(jnp.int32, (BAND, 2 * BAND), 0)
    kj = lax.broadcasted_iota(jnp.int32, (BAND, 2 * BAND), 1)
    visible = ((kj < BAND) & (qi <= kj)) | ((kj >= BAND) & (qi >= kj - BAND))
    bias_ref[...] = jnp.where(visible, 0.0, -jnp.inf)

    def probabilities(q_ref, k_ref, n, has_prev):
        q = q_ref[n * BAND:(n + 1) * BAND, :]
        keys = slice((n - 1) * BAND, (n + 1) * BAND) if has_prev else slice(n * BAND, (n + 1) * BAND)
        bias = bias_ref[...] if has_prev else bias_ref[:, BAND:]
        s = lax.dot_general(q, k_ref[keys, :], (((1,), (1,)), ((), ())),
                            preferred_element_type=F32) * scale + bias
        m = s.max(-1, keepdims=True)
        p = jnp.exp(s - m)
        l = p.sum(-1, keepdims=True)
        return p.astype(BF16), l, m + jnp.log(l), keys

    def finish(g, v_ref, n, p, l, lse, keys):
        dilation = DILATED_GROUPS[g][1]
        residue, t = divmod(n, n_tiles // dilation)
        rows = pl.ds(t * BAND * dilation + residue, BAND, stride=dilation)
        o_nat_ref[g, rows, :] = jnp.dot(p, v_ref[keys, :], preferred_element_type=F32) / l
        lse_nat_ref[g, rows, :] = jnp.broadcast_to(lse, (BAND, HEAD_DIM))

    groups = [(r.at[0], r.at[1], r.at[2]) for r in (qkv1_ref, qkv2_ref, qkv3_ref)]
    in_flight = []
    for g, (q_ref, k_ref, v_ref) in enumerate(groups):
        tiles_per_residue = n_tiles // DILATED_GROUPS[g][1]
        for n in range(n_tiles):
            has_prev = n % tiles_per_residue > 0
            in_flight.append((g, v_ref, n) + probabilities(q_ref, k_ref, n, has_prev))
            if len(in_flight) > ATTN_TILES_IN_FLIGHT:
                finish(*in_flight.pop(0))
    for pending in in_flight:
        finish(*pending)

    def mix(c, _):
        rows = pl.ds(pl.multiple_of(c * BAND, BAND), BAND)
        lse = [lse_nat_ref[g, rows, :] for g in range(N_GROUPS)]
        top = jnp.maximum(jnp.maximum(lse[0], lse[1]), lse[2])
        w = [jnp.exp(x - top) for x in lse]
        num = w[0] * o_nat_ref[0, rows, :] + w[1] * o_nat_ref[1, rows, :] + w[2] * o_nat_ref[2, rows, :]
        o_ref[rows, :] = (num / (w[0] + w[1] + w[2])).astype(o_ref.dtype)
        return 0

    lax.fori_loop(0, n_tiles, mix, 0)


def _attention(qkv, batch, seq):
    heads = HEADS_PER_GROUP
    qkv_spec = pl.BlockSpec((None, None, 3, seq, HEAD_DIM), lambda b, h: (h, b, 0, 0, 0))
    scratch = 2 * _nbytes((N_GROUPS, seq, HEAD_DIM), F32)
    return pl.pallas_call(
        _attn_kernel,
        grid=(batch, heads),
        in_specs=[qkv_spec] * N_GROUPS,
        out_specs=pl.BlockSpec((seq, HEAD_DIM), lambda b, h: (b, h)),
        out_shape=jax.ShapeDtypeStruct((batch * seq, GROUP_WIDTH), BF16),
        scratch_shapes=[pltpu.VMEM((N_GROUPS, seq, HEAD_DIM), F32),
                        pltpu.VMEM((N_GROUPS, seq, HEAD_DIM), F32),
                        pltpu.VMEM((BAND, 2 * BAND), F32)],
        compiler_params=pltpu.CompilerParams(
            dimension_semantics=("parallel", "parallel"),
            vmem_limit_bytes=_vmem_limit(10 * _nbytes((seq, HEAD_DIM), BF16), scratch + 4 * 1024 * 1024)),
        name="dilated_attention",
    )(*qkv)


def _merge_kernel(attn_ref, conv_ref, x_ref, wao_ref, wco_ref, wga_ref, wgc_ref, o_ref):
    x = x_ref[...]
    a = jnp.dot(attn_ref[...], wao_ref[...], preferred_element_type=F32)
    ga = _sigmoid(jnp.dot(x, wga_ref[...], preferred_element_type=F32))
    c = jnp.dot(conv_ref[...], wco_ref[...], preferred_element_type=F32)
    gc = _sigmoid(jnp.dot(x, wgc_ref[...], preferred_element_type=F32))
    o_ref[...] = (ga * a + gc * c).astype(o_ref.dtype)


def _gated_merge(attn, conv, xb, w_attn_o, w_conv_o, w_in, *, tm, tn):
    m = attn.shape[0]
    blocks = (_nbytes((tm, GROUP_WIDTH), BF16) + _nbytes((tm, CONV_WIDTH), BF16)
              + _nbytes((tm, D_MODEL), BF16) + _nbytes((tm, tn), BF16)
              + _nbytes((GROUP_WIDTH + CONV_WIDTH + 2 * D_MODEL, tn), BF16))
    return pl.pallas_call(
        _merge_kernel,
        grid=(m // tm, D_MODEL // tn),
        in_specs=[pl.BlockSpec((tm, GROUP_WIDTH), lambda i, j: (i, 0)),
                  pl.BlockSpec((tm, CONV_WIDTH), lambda i, j: (i, 0)),
                  pl.BlockSpec((tm, D_MODEL), lambda i, j: (i, 0)),
                  pl.BlockSpec((GROUP_WIDTH, tn), lambda i, j: (0, j)),
                  pl.BlockSpec((CONV_WIDTH, tn), lambda i, j: (0, j)),
                  pl.BlockSpec((D_MODEL, tn), lambda i, j: (0, COL_GA // tn + j)),
                  pl.BlockSpec((D_MODEL, tn), lambda i, j: (0, COL_GC // tn + j))],
        out_specs=pl.BlockSpec((tm, tn), lambda i, j: (i, j)),
        out_shape=jax.ShapeDtypeStruct((m, D_MODEL), BF16),
        compiler_params=pltpu.CompilerParams(
            dimension_semantics=("parallel", "arbitrary"),
            vmem_limit_bytes=_vmem_limit(blocks, 5 * _nbytes((tm, tn), F32))),
        name="gated_merge",
    )(attn, conv, xb, w_attn_o, w_conv_o, w_in, w_in)


def _layer_norm(z, g, b):
    mu = jnp.mean(z, axis=-1, keepdims=True)
    zc = z - mu
    var = jnp.mean(zc * zc, axis=-1, keepdims=True)
    return zc * lax.rsqrt(var + LN_EPS) * g + b


def _out_ln_kernel(mi_ref, w_ref, x_ref, g_ref, b_ref, o_ref, ob_ref):
    y = jnp.dot(mi_ref[...], w_ref[...], preferred_element_type=F32)
    h = _layer_norm(ALPHA * x_ref[...] + y, g_ref[...], b_ref[...])
    o_ref[...] = h
    ob_ref[...] = h.astype(ob_ref.dtype)


def _out_proj_ln(merged, w_out, x, g, b, *, tm):
    m = merged.shape[0]
    blocks = (2 * _nbytes((tm, D_MODEL), BF16) + _nbytes((D_MODEL, D_MODEL), BF16)
              + 2 * _nbytes((tm, D_MODEL), F32))
    row = pl.BlockSpec((tm, D_MODEL), lambda i: (i, 0))
    vec = pl.BlockSpec((1, D_MODEL), lambda i: (0, 0))
    return pl.pallas_call(
        _out_ln_kernel,
        grid=(m // tm,),
        in_specs=[row, pl.BlockSpec((D_MODEL, D_MODEL), lambda i: (0, 0)), row, vec, vec],
        out_specs=[row, row],
        out_shape=[jax.ShapeDtypeStruct((m, D_MODEL), F32), jax.ShapeDtypeStruct((m, D_MODEL), BF16)],
        compiler_params=pltpu.CompilerParams(
            dimension_semantics=("parallel",),
            vmem_limit_bytes=_vmem_limit(blocks, 3 * _nbytes((tm, D_MODEL), F32))),
        name="out_proj_ln",
    )(merged, w_out, x, g, b)


def _ffn_kernel(hb_ref, h_hbm_ref, wg_ref, wu_ref, wd_ref, g_ref, b_ref, o_ref, res_ref, res_sem):
    i, f = pl.program_id(0), pl.program_id(1)
    tm = o_ref.shape[0]
    residual_copy = pltpu.make_async_copy(
        h_hbm_ref.at[pl.ds(pl.multiple_of(i * tm, tm), tm), :], res_ref, res_sem)

    @pl.when(f == 0)
    def _():
        residual_copy.start()
        o_ref[...] = jnp.zeros_like(o_ref)

    hb = hb_ref[...]
    gate = jnp.dot(hb, wg_ref[...], preferred_element_type=F32)
    up = jnp.dot(hb, wu_ref[...], preferred_element_type=F32)
    hidden = (gate * _sigmoid(gate) * up).astype(BF16)
    o_ref[...] += jnp.dot(hidden, wd_ref[...], preferred_element_type=F32)

    @pl.when(f == pl.num_programs(1) - 1)
    def _():
        residual_copy.wait()
        o_ref[...] = _layer_norm(ALPHA * res_ref[...] + o_ref[...], g_ref[...], b_ref[...])


def _ffn_ln(h, hb, w_gate, w_up, w_down, g, b, *, tm, tf):
    m = h.shape[0]
    blocks = (_nbytes((tm, D_MODEL), BF16) + _nbytes((tm, D_MODEL), F32)
              + 2 * _nbytes((D_MODEL, tf), BF16) + _nbytes((tf, D_MODEL), BF16))
    scratch = _nbytes((tm, D_MODEL), F32)
    row = pl.BlockSpec((tm, D_MODEL), lambda i, f: (i, 0))
    vec = pl.BlockSpec((1, D_MODEL), lambda i, f: (0, 0))
    return pl.pallas_call(
        _ffn_kernel,
        grid=(m // tm, FFN_HIDDEN // tf),
        in_specs=[row,
                  pl.BlockSpec(memory_space=pl.ANY),
                  pl.BlockSpec((D_MODEL, tf), lambda i, f: (0, f)),
                  pl.BlockSpec((D_MODEL, tf), lambda i, f: (0, f)),
                  pl.BlockSpec((tf, D_MODEL), lambda i, f: (f, 0)),
                  vec, vec],
        out_specs=row,
        out_shape=jax.ShapeDtypeStruct((m, D_MODEL), F32),
        scratch_shapes=[pltpu.VMEM((tm, D_MODEL), F32), pltpu.SemaphoreType.DMA],
        compiler_params=pltpu.CompilerParams(
            dimension_semantics=("arbitrary", "arbitrary"),
            vmem_limit_bytes=_vmem_limit(blocks, scratch + 6 * _nbytes((tm, tf), F32))),
        name="ffn_ln",
    )(hb, h, w_gate, w_up, w_down, g, b)


def kernel(x, w_in, conv_w, w_attn_o, w_conv_o, w_out, ln1_g, ln1_b,
           w_ffn_gate, w_ffn_up, w_ffn_down, ln2_g, ln2_b):
    batch, seq, d = x.shape
    assert d == D_MODEL and w_in.shape == (DEPTH, D_MODEL, IN_COLS)
    assert seq % PROJ_ROWS == 0 and all(PROJ_ROWS % (dil * 16) == 0 for _, dil in DILATED_GROUPS)
    assert all(seq % (dil * BAND) == 0 for _, dil in DILATED_GROUPS)
    h = x.reshape(batch * seq, d)
    for layer in range(DEPTH):
        w_in_b = w_in[layer].astype(BF16)
        qkv0, hb = _qkv_proj(h, w_in_b, 0, batch, seq)
        qkv = [qkv0] + [_qkv_proj(hb, w_in_b, g, batch, seq) for g in range(1, N_GROUPS)]
        conv = _conv_proj(hb, w_in_b, conv_w[layer], batch, seq, tn=512)
        attn = _attention(qkv, batch, seq)
        merged = _gated_merge(attn, conv, hb, w_attn_o[layer].astype(BF16),
                              w_conv_o[layer].astype(BF16), w_in_b, tm=1024, tn=512)
        h, hb = _out_proj_ln(merged, w_out[layer].astype(BF16), h,
                             ln1_g[layer][None, :], ln1_b[layer][None, :], tm=512)
        h = _ffn_ln(h, hb, w_ffn_gate[layer].astype(BF16), w_ffn_up[layer].astype(BF16),
                    w_ffn_down[layer].astype(BF16), ln2_g[layer][None, :], ln2_b[layer][None, :],
                    tm=1024, tf=512)
    return h.reshape(batch, seq, d)
```

```python
import functools
import math

import jax
import jax.numpy as jnp
from jax import lax
from jax.experimental import pallas as pl
from jax.experimental.pallas import tpu as pltpu

D_MODEL = 2048
HEAD_DIM = 128
HEADS_PER_GROUP = 8
DILATED_GROUPS = ((128, 1), (512, 4), (2048, 16))
N_GROUPS = len(DILATED_GROUPS)
GROUP_WIDTH = HEADS_PER_GROUP * HEAD_DIM
ATTN_WIDTH = N_GROUPS * GROUP_WIDTH
CONV_WIDTH = D_MODEL
CONV_K = 3
FFN_HIDDEN = 5632
DEPTH = 1
ALPHA = (2 * DEPTH) ** 0.25
LN_EPS = 1e-5

COL_U = 3 * ATTN_WIDTH
COL_C = COL_U + CONV_WIDTH
COL_B = COL_C + CONV_WIDTH
COL_GA = COL_B + CONV_WIDTH
COL_GC = COL_GA + D_MODEL
IN_COLS = COL_GC + D_MODEL

V7X_VMEM_BYTES = 64 * 1024 * 1024
BAND = 128
ATTN_TILES_IN_FLIGHT = 8
MAX_ROW_STRIDE = 4
CAST_RIDER_STEPS = 32
PROJ_ROWS = 1024
PROJ_COLS = 1024

F32 = jnp.float32
BF16 = jnp.bfloat16


def _vmem_limit(pipelined_block_bytes, resident_bytes):
    need = 2 * pipelined_block_bytes + resident_bytes
    return min(int(need * 1.25), V7X_VMEM_BYTES - 4 * 1024 * 1024)


def _nbytes(shape, dtype):
    return math.prod(shape) * jnp.dtype(dtype).itemsize


def _sigmoid(x):
    return 0.5 * (jnp.tanh(0.5 * x) + 1.0)


def _store_heads_by_residue(y, o_ref, y_ref, y2_ref, dilation):
    rows = y.shape[0]
    f1 = min(dilation, MAX_ROW_STRIDE)
    f2 = dilation // f1
    assert f1 * f2 == dilation and f2 <= MAX_ROW_STRIDE
    for head in range(HEADS_PER_GROUP):
        y_head = y[:, head * HEAD_DIM:(head + 1) * HEAD_DIM]
        if dilation == 1:
            o_ref[head, 0] = y_head.astype(o_ref.dtype)
            continue
        src = y_ref.at[head]
        src[...] = y_head
        if f2 > 1:
            for r0 in range(f1):
                y2_ref[head, r0 * (rows // f1):(r0 + 1) * (rows // f1), :] = src[pl.ds(r0, rows // f1, stride=f1), :]
            src = y2_ref.at[head]
        for r0 in range(f1):
            for r1 in range(f2):
                start = r0 * (rows // f1) + r1 if f2 > 1 else r0
                stride = f2 if f2 > 1 else f1
                o_ref[head, r1 * f1 + r0] = src[pl.ds(start, rows // dilation, stride=stride), :].astype(o_ref.dtype)


def _qkv_proj_first_kernel(x_ref, w_ref, o_ref, xb_ref):
    @pl.when(pl.program_id(2) == 0)
    def _():
        xb_ref[...] = x_ref[...].astype(BF16)

    y = jnp.dot(xb_ref[...], w_ref[...], preferred_element_type=F32)
    _store_heads_by_residue(y, o_ref, None, None, 1)


def _qkv_proj_kernel(xb_ref, w_ref, rider_ref, o_ref, rider_out_ref, y_ref, y2_ref, *, dilation):
    rider_out_ref[...] = rider_ref[...].astype(rider_out_ref.dtype)
    y = jnp.dot(xb_ref[...], w_ref[...], preferred_element_type=F32)
    _store_heads_by_residue(y, o_ref, y_ref, y2_ref, dilation)


def _cast_rider_spec(w, step_of):
    rows = w.shape[0] // CAST_RIDER_STEPS
    assert rows * CAST_RIDER_STEPS == w.shape[0] and rows % 16 == 0
    spec = pl.BlockSpec((rows, w.shape[1]),
                        lambda *idx: (jnp.minimum(step_of(*idx), CAST_RIDER_STEPS - 1), 0))
    return spec, jax.ShapeDtypeStruct(w.shape, BF16), _nbytes((rows, w.shape[1]), F32) * 3 // 2


def _qkv_proj(x, w_in, group, batch, seq, rider=None):
    dilation = DILATED_GROUPS[group][1]
    first = x.dtype == F32
    assert first == (group == 0) == (rider is None) and dilation == (1 if first else dilation)
    length = seq // dilation
    tiles = seq // PROJ_ROWS
    per_residue = PROJ_ROWS // dilation
    assert PROJ_COLS == GROUP_WIDTH
    sections = ATTN_WIDTH // PROJ_COLS
    row_spec = pl.BlockSpec((PROJ_ROWS, D_MODEL), lambda b, t, j: (b * tiles + t, 0))
    qkv_spec = pl.BlockSpec((HEADS_PER_GROUP, None, None, dilation, per_residue, HEAD_DIM),
                            lambda b, t, j: (0, b, j, 0, t, 0))
    qkv_shape = jax.ShapeDtypeStruct((HEADS_PER_GROUP, batch, 3, dilation, length, HEAD_DIM), BF16)
    blocks = (_nbytes((PROJ_ROWS, D_MODEL), x.dtype) + _nbytes((D_MODEL, PROJ_COLS), BF16)
              + _nbytes((PROJ_ROWS, PROJ_COLS), BF16) + (_nbytes((PROJ_ROWS, D_MODEL), BF16) if first else 0))
    regroup = [] if first else [pltpu.VMEM((HEADS_PER_GROUP, PROJ_ROWS, HEAD_DIM), F32)] * 2
    w_spec = pl.BlockSpec((D_MODEL, PROJ_COLS), lambda b, t, j: (0, j * sections + group))
    if first:
        in_specs, args = [row_spec, w_spec], (x, w_in)
        out_specs, out_shape = [qkv_spec, row_spec], [qkv_shape, jax.ShapeDtypeStruct(x.shape, BF16)]
    else:
        rider_spec, rider_shape, rider_bytes = _cast_rider_spec(rider, lambda b, t, j: (b * tiles + t) * 3 + j)
        in_specs, args = [row_spec, w_spec, rider_spec], (x, w_in, rider)
        out_specs, out_shape = [qkv_spec, rider_spec], [qkv_shape, rider_shape]
        blocks += rider_bytes
    out = pl.pallas_call(
        _qkv_proj_first_kernel if first else functools.partial(_qkv_proj_kernel, dilation=dilation),
        grid=(batch, tiles, 3),
        in_specs=in_specs,
        out_specs=out_specs,
        out_shape=out_shape,
        scratch_shapes=regroup,
        compiler_params=pltpu.CompilerParams(
            dimension_semantics=("arbitrary", "arbitrary", "arbitrary"),
            vmem_limit_bytes=_vmem_limit(blocks, (0 if first else 2 * _nbytes((PROJ_ROWS, PROJ_COLS), F32))
                                         + 2 * _nbytes((PROJ_ROWS, PROJ_COLS), F32))),
        name=f"qkv_proj_dilation{dilation}",
    )(*args)
    return out[0].reshape(HEADS_PER_GROUP, batch, 3, seq, HEAD_DIM), out[1]


def _conv_proj_kernel(x_ref, wu_ref, wc_ref, wb_ref, cw_ref, rider_ref, o_ref, rider_out_ref):
    rider_out_ref[...] = rider_ref[...].astype(rider_out_ref.dtype)
    x = x_ref[...]
    z = (jnp.dot(x, wc_ref[...], preferred_element_type=F32)
         * jnp.dot(x, wu_ref[...], preferred_element_type=F32))
    cw = cw_ref[...]
    row = lax.broadcasted_iota(jnp.int32, z.shape, 0)
    y = cw[0:1, :] * z
    for tap in range(1, CONV_K):
        shifted = jnp.where(row >= tap, pltpu.roll(z, tap, axis=0), 0.0)
        y = y + cw[tap:tap + 1, :] * shifted
    o_ref[...] = (jnp.dot(x, wb_ref[...], preferred_element_type=F32) * y).astype(o_ref.dtype)


def _conv_proj(xb, w_in, conv_w, batch, seq, rider, *, tn):
    def w_spec(col0):
        return pl.BlockSpec((D_MODEL, tn), lambda b, j: (0, col0 // tn + j))

    col_blocks = CONV_WIDTH // tn
    rider_spec, rider_shape, rider_bytes = _cast_rider_spec(rider, lambda b, j: b * col_blocks + j)
    blocks = (_nbytes((seq, D_MODEL), BF16) + 3 * _nbytes((D_MODEL, tn), BF16)
              + _nbytes((CONV_K, tn), F32) + _nbytes((seq, tn), BF16) + rider_bytes)
    return pl.pallas_call(
        _conv_proj_kernel,
        grid=(batch, col_blocks),
        in_specs=[pl.BlockSpec((seq, D_MODEL), lambda b, j: (b, 0)),
                  w_spec(COL_U), w_spec(COL_C), w_spec(COL_B),
                  pl.BlockSpec((CONV_K, tn), lambda b, j: (0, j)),
                  rider_spec],
        out_specs=[pl.BlockSpec((seq, tn), lambda b, j: (b, j)), rider_spec],
        out_shape=[jax.ShapeDtypeStruct((batch * seq, CONV_WIDTH), BF16), rider_shape],
        compiler_params=pltpu.CompilerParams(
            dimension_semantics=("arbitrary", "arbitrary"),
            vmem_limit_bytes=_vmem_limit(blocks, 5 * _nbytes((seq, tn), F32))),
        name="conv_proj",
    )(xb, w_in, w_in, w_in, conv_w, rider)


def _attn_kernel(qkv1_ref, qkv2_ref, qkv3_ref, o_ref, o_nat_ref, lse_nat_ref, bias_ref):
    seq = qkv1_ref.shape[1]
    n_tiles = seq // BAND
    scale = HEAD_DIM ** -0.5
    qi = lax.broadcasted_iota(jnp.int32, (BAND, 2 * BAND), 0)
    kj = lax.broadcasted_iota(jnp.int32, (BAND, 2 * BAND), 1)
    visible = ((kj < BAND) & (qi <= kj)) | ((kj >= BAND) & (qi >= kj - BAND))
    bias_ref[...] = jnp.where(visible, 0.0, -jnp.inf)

    def probabilities(q_ref, k_ref, n, has_prev):
        q = q_ref[n * BAND:(n + 1) * BAND, :]
        keys = slice((n - 1) * BAND, (n + 1) * BAND) if has_prev else slice(n * BAND, (n + 1) * BAND)
        bias = bias_ref[...] if has_prev else bias_ref[:, BAND:]
        s = lax.dot_general(q, k_ref[keys, :], (((1,), (1,)), ((), ())),
                            preferred_element_type=F32) * scale + bias
        m = s.max(-1, keepdims=True)
        p = jnp.exp(s - m)
        l = p.sum(-1, keepdims=True)
        return p.astype(BF16), l, m + jnp.log(l), keys

    assert DILATED_GROUPS[0][1] == 1 and all(d % MAX_ROW_STRIDE == 0 for _, d in DILATED_GROUPS[1:])
    quarter = seq // MAX_ROW_STRIDE

    def result_rows(g, n):
        dilation = DILATED_GROUPS[g][1]
        if dilation == 1:
            return pl.ds(n * BAND, BAND)
        residue, t = divmod(n, n_tiles // dilation)
        step = dilation // MAX_ROW_STRIDE
        start = (residue % MAX_ROW_STRIDE) * quarter + step * t * BAND + residue // MAX_ROW_STRIDE
        return pl.ds(start, BAND, stride=step)

    def finish(g, v_ref, n, p, l, lse, keys):
        rows = result_rows(g, n)
        o_nat_ref[g, rows, :] = jnp.dot(p, v_ref[keys, :], preferred_element_type=F32) / l
        lse_nat_ref[g, rows, :] = jnp.broadcast_to(lse, (BAND, HEAD_DIM))

    groups = [(r.at[0], r.at[1], r.at[2]) for r in (qkv1_ref, qkv2_ref, qkv3_ref)]
    in_flight = []
    for g, (q_ref, k_ref, v_ref) in enumerate(groups):
        tiles_per_residue = n_tiles // DILATED_GROUPS[g][1]
        for n in range(n_tiles):
            has_prev = n % tiles_per_residue > 0
            in_flight.append((g, v_ref, n) + probabilities(q_ref, k_ref, n, has_prev))
            if len(in_flight) > ATTN_TILES_IN_FLIGHT:
                finish(*in_flight.pop(0))
    for pending in in_flight:
        finish(*pending)

    chunks_per_quarter = quarter // BAND

    def mix(c, _):
        mixed = pl.ds(pl.multiple_of(c * BAND, BAND), BAND)
        natural = pl.ds(MAX_ROW_STRIDE * (c % chunks_per_quarter) * BAND + c // chunks_per_quarter,
                        BAND, stride=MAX_ROW_STRIDE)
        rows = [natural] + [mixed] * (N_GROUPS - 1)
        lse = [lse_nat_ref[g, rows[g], :] for g in range(N_GROUPS)]
        top = jnp.maximum(jnp.maximum(lse[0], lse[1]), lse[2])
        w = [jnp.exp(x - top) for x in lse]
        num = (w[0] * o_nat_ref[0, rows[0], :] + w[1] * o_nat_ref[1, rows[1], :]
               + w[2] * o_nat_ref[2, rows[2], :])
        o_nat_ref[0, natural, :] = num / (w[0] + w[1] + w[2])
        return 0

    lax.fori_loop(0, n_tiles, mix, 0)
    o_ref[...] = o_nat_ref[0].astype(o_ref.dtype)


def _attention(qkv, batch, seq):
    heads = HEADS_PER_GROUP
    qkv_spec = pl.BlockSpec((None, None, 3, seq, HEAD_DIM), lambda b, h: (h, b, 0, 0, 0))
    scratch = 2 * _nbytes((N_GROUPS, seq, HEAD_DIM), F32)
    return pl.pallas_call(
        _attn_kernel,
        grid=(batch, heads),
        in_specs=[qkv_spec] * N_GROUPS,
        out_specs=pl.BlockSpec((seq, HEAD_DIM), lambda b, h: (b, h)),
        out_shape=jax.ShapeDtypeStruct((batch * seq, GROUP_WIDTH), BF16),
        scratch_shapes=[pltpu.VMEM((N_GROUPS, seq, HEAD_DIM), F32),
                        pltpu.VMEM((N_GROUPS, seq, HEAD_DIM), F32),
                        pltpu.VMEM((BAND, 2 * BAND), F32)],
        compiler_params=pltpu.CompilerParams(
            dimension_semantics=("parallel", "parallel"),
            vmem_limit_bytes=_vmem_limit(10 * _nbytes((seq, HEAD_DIM), BF16), scratch + 4 * 1024 * 1024)),
        name="dilated_attention",
    )(*qkv)


def _merge_kernel(attn_ref, conv_ref, x_ref, wao_ref, wco_ref, wga_ref, wgc_ref, o_ref):
    x = x_ref[...]
    a = jnp.dot(attn_ref[...], wao_ref[...], preferred_element_type=F32)
    ga = _sigmoid(jnp.dot(x, wga_ref[...], preferred_element_type=F32))
    c = jnp.dot(conv_ref[...], wco_ref[...], preferred_element_type=F32)
    gc = _sigmoid(jnp.dot(x, wgc_ref[...], preferred_element_type=F32))
    o_ref[...] = (ga * a + gc * c).astype(o_ref.dtype)


def _gated_merge(attn, conv, xb, w_attn_o, w_conv_o, w_in, *, tm, tn):
    m = attn.shape[0]
    blocks = (_nbytes((tm, GROUP_WIDTH), BF16) + _nbytes((tm, CONV_WIDTH), BF16)
              + _nbytes((tm, D_MODEL), BF16) + _nbytes((tm, tn), BF16)
              + _nbytes((GROUP_WIDTH + CONV_WIDTH + 2 * D_MODEL, tn), BF16))
    return pl.pallas_call(
        _merge_kernel,
        grid=(m // tm, D_MODEL // tn),
        in_specs=[pl.BlockSpec((tm, GROUP_WIDTH), lambda i, j: (i, 0)),
                  pl.BlockSpec((tm, CONV_WIDTH), lambda i, j: (i, 0)),
                  pl.BlockSpec((tm, D_MODEL), lambda i, j: (i, 0)),
                  pl.BlockSpec((GROUP_WIDTH, tn), lambda i, j: (0, j)),
                  pl.BlockSpec((CONV_WIDTH, tn), lambda i, j: (0, j)),
                  pl.BlockSpec((D_MODEL, tn), lambda i, j: (0, COL_GA // tn + j)),
                  pl.BlockSpec((D_MODEL, tn), lambda i, j: (0, COL_GC // tn + j))],
        out_specs=pl.BlockSpec((tm, tn), lambda i, j: (i, j)),
        out_shape=jax.ShapeDtypeStruct((m, D_MODEL), BF16),
        compiler_params=pltpu.CompilerParams(
            dimension_semantics=("parallel", "arbitrary"),
            vmem_limit_bytes=_vmem_limit(blocks, 5 * _nbytes((tm, tn), F32))),
        name="gated_merge",
    )(attn, conv, xb, w_attn_o, w_conv_o, w_in, w_in)


def _layer_norm(z, g, b):
    mu = jnp.mean(z, axis=-1, keepdims=True)
    zc = z - mu
    var = jnp.mean(zc * zc, axis=-1, keepdims=True)
    return zc * lax.rsqrt(var + LN_EPS) * g + b


def _out_ln_kernel(mi_ref, w_ref, x_ref, g_ref, b_ref, o_ref, ob_ref):
    y = jnp.dot(mi_ref[...], w_ref[...], preferred_element_type=F32)
    h = _layer_norm(ALPHA * x_ref[...] + y, g_ref[...], b_ref[...])
    o_ref[...] = h
    ob_ref[...] = h.astype(ob_ref.dtype)


def _out_proj_ln(merged, w_out, x, g, b, *, tm):
    m = merged.shape[0]
    blocks = (2 * _nbytes((tm, D_MODEL), BF16) + _nbytes((D_MODEL, D_MODEL), BF16)
              + 2 * _nbytes((tm, D_MODEL), F32))
    row = pl.BlockSpec((tm, D_MODEL), lambda i: (i, 0))
    vec = pl.BlockSpec((1, D_MODEL), lambda i: (0, 0))
    return pl.pallas_call(
        _out_ln_kernel,
        grid=(m // tm,),
        in_specs=[row, pl.BlockSpec((D_MODEL, D_MODEL), lambda i: (0, 0)), row, vec, vec],
        out_specs=[row, row],
        out_shape=[jax.ShapeDtypeStruct((m, D_MODEL), F32), jax.ShapeDtypeStruct((m, D_MODEL), BF16)],
        compiler_params=pltpu.CompilerParams(
            dimension_semantics=("parallel",),
            vmem_limit_bytes=_vmem_limit(blocks, 3 * _nbytes((tm, D_MODEL), F32))),
        name="out_proj_ln",
    )(merged, w_out, x, g, b)


def _ffn_kernel(hb_ref, h_hbm_ref, wg_ref, wu_ref, wd_ref, g_ref, b_ref, o_ref, res_ref, res_sem):
    i, f = pl.program_id(0), pl.program_id(1)
    tm = o_ref.shape[0]
    residual_copy = pltpu.make_async_copy(
        h_hbm_ref.at[pl.ds(pl.multiple_of(i * tm, tm), tm), :], res_ref, res_sem)

    @pl.when(f == 0)
    def _():
        residual_copy.start()
        o_ref[...] = jnp.zeros_like(o_ref)

    hb = hb_ref[...]
    gate = jnp.dot(hb, wg_ref[...], preferred_element_type=F32)
    up = jnp.dot(hb, wu_ref[...], preferred_element_type=F32)
    hidden = (gate * _sigmoid(gate) * up).astype(BF16)
    o_ref[...] += jnp.dot(hidden, wd_ref[...], preferred_element_type=F32)

    @pl.when(f == pl.num_programs(1) - 1)
    def _():
        residual_copy.wait()
        o_ref[...] = _layer_norm(ALPHA * res_ref[...] + o_ref[...], g_ref[...], b_ref[...])


def _ffn_ln(h, hb, w_gate, w_up, w_down, g, b, *, tm, tf):
    m = h.shape[0]
    blocks = (_nbytes((tm, D_MODEL), BF16) + _nbytes((tm, D_MODEL), F32)
              + 2 * _nbytes((D_MODEL, tf), BF16) + _nbytes((tf, D_MODEL), BF16))
    scratch = _nbytes((tm, D_MODEL), F32)
    row = pl.BlockSpec((tm, D_MODEL), lambda i, f: (i, 0))
    vec = pl.BlockSpec((1, D_MODEL), lambda i, f: (0, 0))
    return pl.pallas_call(
        _ffn_kernel,
        grid=(m // tm, FFN_HIDDEN // tf),
        in_specs=[row,
                  pl.BlockSpec(memory_space=pl.ANY),
                  pl.BlockSpec((D_MODEL, tf), lambda i, f: (0, f)),
                  pl.BlockSpec((D_MODEL, tf), lambda i, f: (0, f)),
                  pl.BlockSpec((tf, D_MODEL), lambda i, f: (f, 0)),
                  vec, vec],
        out_specs=row,
        out_shape=jax.ShapeDtypeStruct((m, D_MODEL), F32),
        scratch_shapes=[pltpu.VMEM((tm, D_MODEL), F32), pltpu.SemaphoreType.DMA],
        compiler_params=pltpu.CompilerParams(
            dimension_semantics=("arbitrary", "arbitrary"),
            vmem_limit_bytes=_vmem_limit(blocks, scratch + 6 * _nbytes((tm, tf), F32))),
        name="ffn_ln",
    )(hb, h, w_gate, w_up, w_down, g, b)


def kernel(x, w_in, conv_w, w_attn_o, w_conv_o, w_out, ln1_g, ln1_b,
           w_ffn_gate, w_ffn_up, w_ffn_down, ln2_g, ln2_b):
    batch, seq, d = x.shape
    assert d == D_MODEL and w_in.shape == (DEPTH, D_MODEL, IN_COLS)
    assert seq % PROJ_ROWS == 0 and all(PROJ_ROWS % (dil * 16) == 0 for _, dil in DILATED_GROUPS)
    assert all(seq % (dil * BAND) == 0 for _, dil in DILATED_GROUPS)
    h = x.reshape(batch * seq, d)
    for layer in range(DEPTH):
        w_in_b = w_in[layer].astype(BF16)
        assert N_GROUPS == 3
        qkv0, hb = _qkv_proj(h, w_in_b, 0, batch, seq)
        qkv1, w_gate_b = _qkv_proj(hb, w_in_b, 1, batch, seq, rider=w_ffn_gate[layer])
        qkv2, w_up_b = _qkv_proj(hb, w_in_b, 2, batch, seq, rider=w_ffn_up[layer])
        conv, w_down_b = _conv_proj(hb, w_in_b, conv_w[layer], batch, seq, w_ffn_down[layer], tn=512)
        attn = _attention([qkv0, qkv1, qkv2], batch, seq)
        merged = _gated_merge(attn, conv, hb, w_attn_o[layer].astype(BF16),
                              w_conv_o[layer].astype(BF16), w_in_b, tm=1024, tn=512)
        h, hb = _out_proj_ln(merged, w_out[layer].astype(BF16), h,
                             ln1_g[layer][None, :], ln1_b[layer][None, :], tm=512)
        h = _ffn_ln(h, hb, w_gate_b, w_up_b, w_down_b, ln2_g[layer][None, :], ln2_b[layer][None, :],
                    tm=1024, tf=512)
    return h.reshape(batch, seq, d)
```

```python
import functools
import math
from typing import Callable, NamedTuple

import jax
import jax.numpy as jnp
from jax import lax
from jax.experimental import pallas as pl
from jax.experimental.pallas import tpu as pltpu

D_MODEL = 2048
HEAD_DIM = 128
HEADS_PER_GROUP = 8
DILATED_GROUPS = ((128, 1), (512, 4), (2048, 16))
N_GROUPS = len(DILATED_GROUPS)
GROUP_WIDTH = HEADS_PER_GROUP * HEAD_DIM
ATTN_WIDTH = N_GROUPS * GROUP_WIDTH
CONV_WIDTH = D_MODEL
CONV_K = 3
FFN_HIDDEN = 5632
DEPTH = 1
ALPHA = (2 * DEPTH) ** 0.25
LN_EPS = 1e-5

COL_U = 3 * ATTN_WIDTH
COL_C = COL_U + CONV_WIDTH
COL_B = COL_C + CONV_WIDTH
COL_GA = COL_B + CONV_WIDTH
COL_GC = COL_GA + D_MODEL
IN_COLS = COL_GC + D_MODEL

V7X_VMEM_BYTES = 64 * 1024 * 1024
BAND = 128
ATTN_TILES_IN_FLIGHT = 8
MAX_ROW_STRIDE = 4
CAST_RIDER_STEPS = 32
PROJ_ROWS = 1024
PROJ_COLS = 1024
MERGE_SPLIT = 2

F32 = jnp.float32
BF16 = jnp.bfloat16


def _vmem_limit(pipelined_block_bytes, resident_bytes):
    need = 2 * pipelined_block_bytes + resident_bytes
    return min(int(need * 1.25), V7X_VMEM_BYTES - 4 * 1024 * 1024)


def _nbytes(shape, dtype):
    return math.prod(shape) * jnp.dtype(dtype).itemsize


def _sigmoid(x):
    return 0.5 * (jnp.tanh(0.5 * x) + 1.0)


def _store_heads_by_residue(y, o_ref, y_ref, y2_ref, dilation):
    if dilation == 1:
        for head in range(HEADS_PER_GROUP):
            o_ref[head, 0] = y[:, head * HEAD_DIM:(head + 1) * HEAD_DIM].astype(o_ref.dtype)
        return
    f1 = min(dilation, MAX_ROW_STRIDE)
    f2 = dilation // f1
    assert f1 * f2 == dilation and f2 <= MAX_ROW_STRIDE
    rows = y_ref.shape[1]
    per_residue = rows // dilation
    for chunk in range(y.shape[0] // rows):
        dst = slice(chunk * per_residue, (chunk + 1) * per_residue)
        for head in range(HEADS_PER_GROUP):
            src = y_ref.at[head]
            src[...] = y[chunk * rows:(chunk + 1) * rows, head * HEAD_DIM:(head + 1) * HEAD_DIM]
            if f2 > 1:
                for r0 in range(f1):
                    y2_ref[head, r0 * (rows // f1):(r0 + 1) * (rows // f1), :] = src[pl.ds(r0, rows // f1, stride=f1), :]
                src = y2_ref.at[head]
            for r0 in range(f1):
                for r1 in range(f2):
                    start = r0 * (rows // f1) + r1 if f2 > 1 else r0
                    stride = f2 if f2 > 1 else f1
                    o_ref[head, r1 * f1 + r0, dst, :] = (
                        src[pl.ds(start, per_residue, stride=stride), :].astype(o_ref.dtype))


def _cast_riders(rider_refs, rider_out_refs):
    for src, dst in zip(rider_refs, rider_out_refs, strict=True):
        dst[...] = src[...].astype(dst.dtype)


def _qkv_proj_first_kernel(*refs, n_riders):
    x_ref, w_ref = refs[:2]
    rider_refs = refs[2:2 + n_riders]
    o_ref, xb_ref = refs[2 + n_riders:4 + n_riders]
    _cast_riders(rider_refs, refs[4 + n_riders:])

    @pl.when(pl.program_id(2) == 0)
    def _():
        xb_ref[...] = x_ref[...].astype(BF16)

    y = jnp.dot(xb_ref[...], w_ref[...], preferred_element_type=F32)
    _store_heads_by_residue(y, o_ref, None, None, 1)


def _qkv_proj_kernel(*refs, n_riders, dilation):
    xb_ref, w_ref = refs[:2]
    rider_refs = refs[2:2 + n_riders]
    o_ref = refs[2 + n_riders]
    rider_out_refs = refs[3 + n_riders:3 + 2 * n_riders]
    y_ref, y2_ref = refs[3 + 2 * n_riders:]
    _cast_riders(rider_refs, rider_out_refs)
    y = jnp.dot(xb_ref[...], w_ref[...], preferred_element_type=F32)
    _store_heads_by_residue(y, o_ref, y_ref, y2_ref, dilation)


class _CastRider(NamedTuple):
    array: jax.Array
    block: tuple
    in_index: Callable
    out_shape: tuple
    out_index: Callable


def _row_block_rider(w, step_of):
    rows = w.shape[0] // CAST_RIDER_STEPS
    assert rows * CAST_RIDER_STEPS == w.shape[0] and rows % 16 == 0

    def index(*idx):
        return jnp.minimum(step_of(*idx), CAST_RIDER_STEPS - 1), 0

    return _CastRider(w, (rows, w.shape[1]), index, w.shape, index)


def _cast_rider_specs(riders):
    in_specs = [pl.BlockSpec(r.block, r.in_index) for r in riders]
    out_specs = [pl.BlockSpec(r.block, r.out_index) for r in riders]
    shapes = [jax.ShapeDtypeStruct(r.out_shape, BF16) for r in riders]
    return in_specs, out_specs, shapes, sum(_nbytes(r.block, F32) * 3 // 2 for r in riders)


def _qkv_proj(x, w_qkv, group, batch, seq, make_riders):
    dilation = DILATED_GROUPS[group][1]
    first = x.dtype == F32
    assert first == (group == 0) and dilation == (1 if first else dilation)
    length = seq // dilation
    tiles = seq // PROJ_ROWS
    per_residue = PROJ_ROWS // dilation
    assert PROJ_COLS == GROUP_WIDTH and w_qkv.shape == (D_MODEL, 3 * GROUP_WIDTH)
    row_spec = pl.BlockSpec((PROJ_ROWS, D_MODEL), lambda b, t, j: (b * tiles + t, 0))
    qkv_spec = pl.BlockSpec((HEADS_PER_GROUP, None, None, dilation, per_residue, HEAD_DIM),
                            lambda b, t, j: (0, b, j, 0, t, 0))
    qkv_shape = jax.ShapeDtypeStruct((HEADS_PER_GROUP, batch, 3, dilation, length, HEAD_DIM), BF16)
    blocks = (_nbytes((PROJ_ROWS, D_MODEL), x.dtype) + _nbytes((D_MODEL, PROJ_COLS), BF16)
              + _nbytes((PROJ_ROWS, PROJ_COLS), BF16) + (_nbytes((PROJ_ROWS, D_MODEL), BF16) if first else 0))
    regroup = [] if first else [pltpu.VMEM((HEADS_PER_GROUP, PROJ_ROWS, HEAD_DIM), F32)] * 2
    w_spec = pl.BlockSpec((D_MODEL, PROJ_COLS), lambda b, t, j: (0, j))
    riders = make_riders(lambda b, t, j: (b * tiles + t) * 3 + j, lambda b, t, j: b * tiles + t)
    rider_in_specs, rider_specs, rider_shapes, rider_bytes = _cast_rider_specs(riders)
    own_out_specs, own_out_shapes = [qkv_spec], [qkv_shape]
    if first:
        own_out_specs.append(row_spec)
        own_out_shapes.append(jax.ShapeDtypeStruct(x.shape, BF16))
        body = functools.partial(_qkv_proj_first_kernel, n_riders=len(riders))
    else:
        body = functools.partial(_qkv_proj_kernel, n_riders=len(riders), dilation=dilation)
    out = pl.pallas_call(
        body,
        grid=(batch, tiles, 3),
        in_specs=[row_spec, w_spec] + rider_in_specs,
        out_specs=own_out_specs + rider_specs,
        out_shape=own_out_shapes + rider_shapes,
        scratch_shapes=regroup,
        compiler_params=pltpu.CompilerParams(
            dimension_semantics=("arbitrary", "arbitrary", "arbitrary"),
            vmem_limit_bytes=_vmem_limit(blocks + rider_bytes,
                                         (0 if first else 2 * _nbytes((PROJ_ROWS, PROJ_COLS), F32))
                                         + 2 * _nbytes((PROJ_ROWS, PROJ_COLS), F32))),
        name=f"qkv_proj_dilation{dilation}",
    )(x, w_qkv, *[r.array for r in riders])
    return (out[0].reshape(HEADS_PER_GROUP, batch, 3, seq, HEAD_DIM),) + tuple(out[1:])


def _conv_proj_kernel(*refs, n_riders):
    x_ref, wu_ref, wc_ref, wb_ref, cw_ref = refs[:5]
    o_ref = refs[5 + n_riders]
    _cast_riders(refs[5:5 + n_riders], refs[6 + n_riders:])
    x = x_ref[...]
    z = (jnp.dot(x, wc_ref[...], preferred_element_type=F32)
         * jnp.dot(x, wu_ref[...], preferred_element_type=F32))
    cw = cw_ref[...]
    row = lax.broadcasted_iota(jnp.int32, z.shape, 0)
    y = cw[0:1, :] * z
    for tap in range(1, CONV_K):
        shifted = jnp.where(row >= tap, pltpu.roll(z, tap, axis=0), 0.0)
        y = y + cw[tap:tap + 1, :] * shifted
    o_ref[...] = (jnp.dot(x, wb_ref[...], preferred_element_type=F32) * y).astype(o_ref.dtype)


def _conv_proj(xb, w_ucb, conv_w, batch, seq, make_riders, *, tn):
    def w_spec(section):
        return pl.BlockSpec((D_MODEL, tn), lambda b, j: (0, section * (CONV_WIDTH // tn) + j))

    col_blocks = CONV_WIDTH // tn
    assert w_ucb.shape == (D_MODEL, 3 * CONV_WIDTH)
    riders = make_riders(lambda b, j: b * col_blocks + j)
    rider_in_specs, rider_specs, rider_shapes, rider_bytes = _cast_rider_specs(riders)
    blocks = (_nbytes((seq, D_MODEL), BF16) + 3 * _nbytes((D_MODEL, tn), BF16)
              + _nbytes((CONV_K, tn), F32) + _nbytes((seq, tn), BF16) + rider_bytes)
    return pl.pallas_call(
        functools.partial(_conv_proj_kernel, n_riders=len(riders)),
        grid=(batch, col_blocks),
        in_specs=[pl.BlockSpec((seq, D_MODEL), lambda b, j: (b, 0)),
                  w_spec(0), w_spec(1), w_spec(2),
                  pl.BlockSpec((CONV_K, tn), lambda b, j: (0, j))] + rider_in_specs,
        out_specs=[pl.BlockSpec((seq, tn), lambda b, j: (b, j))] + rider_specs,
        out_shape=[jax.ShapeDtypeStruct((batch * seq, CONV_WIDTH), BF16)] + rider_shapes,
        compiler_params=pltpu.CompilerParams(
            dimension_semantics=("arbitrary", "arbitrary"),
            vmem_limit_bytes=_vmem_limit(blocks, 5 * _nbytes((seq, tn), F32))),
        name="conv_proj",
    )(xb, w_ucb, w_ucb, w_ucb, conv_w, *[r.array for r in riders])


def _attn_kernel(qkv1_ref, qkv2_ref, qkv3_ref, o_ref, o_nat_ref, lse_nat_ref, bias_ref):
    seq = qkv1_ref.shape[1]
    n_tiles = seq // BAND
    scale = HEAD_DIM ** -0.5
    qi = lax.broadcasted_iota(jnp.int32, (BAND, 2 * BAND), 0)
    kj = lax.broadcasted_iota(jnp.int32, (BAND, 2 * BAND), 1)
    visible = ((kj < BAND) & (qi <= kj)) | ((kj >= BAND) & (qi >= kj - BAND))
    bias_ref[...] = jnp.where(visible, 0.0, -jnp.inf)

    def probabilities(q_ref, k_ref, n, has_prev):
        q = q_ref[n * BAND:(n + 1) * BAND, :]
        keys = slice((n - 1) * BAND, (n + 1) * BAND) if has_prev else slice(n * BAND, (n + 1) * BAND)
        bias = bias_ref[...] if has_prev else bias_ref[:, BAND:]
        s = lax.dot_general(q, k_ref[keys, :], (((1,), (1,)), ((), ())),
                            preferred_element_type=F32) * scale + bias
        m = s.max(-1, keepdims=True)
        p = jnp.exp(s - m)
        l = p.sum(-1, keepdims=True)
        return p.astype(BF16), l, m + jnp.log(l), keys

    assert DILATED_GROUPS[0][1] == 1 and all(d % MAX_ROW_STRIDE == 0 for _, d in DILATED_GROUPS[1:])
    quarter = seq // MAX_ROW_STRIDE

    def result_rows(g, n):
        dilation = DILATED_GROUPS[g][1]
        if dilation == 1:
            return pl.ds(n * BAND, BAND)
        residue, t = divmod(n, n_tiles // dilation)
        step = dilation // MAX_ROW_STRIDE
        start = (residue % MAX_ROW_STRIDE) * quarter + step * t * BAND + residue // MAX_ROW_STRIDE
        return pl.ds(start, BAND, stride=step)

    def finish(g, v_ref, n, p, l, lse, keys):
        rows = result_rows(g, n)
        o_nat_ref[g, rows, :] = jnp.dot(p, v_ref[keys, :], preferred_element_type=F32) / l
        lse_nat_ref[g, rows, :] = jnp.broadcast_to(lse, (BAND, HEAD_DIM))

    groups = [(r.at[0], r.at[1], r.at[2]) for r in (qkv1_ref, qkv2_ref, qkv3_ref)]
    in_flight = []
    for g, (q_ref, k_ref, v_ref) in enumerate(groups):
        tiles_per_residue = n_tiles // DILATED_GROUPS[g][1]
        for n in range(n_tiles):
            has_prev = n % tiles_per_residue > 0
            in_flight.append((g, v_ref, n) + probabilities(q_ref, k_ref, n, has_prev))
            if len(in_flight) > ATTN_TILES_IN_FLIGHT:
                finish(*in_flight.pop(0))
    for pending in in_flight:
        finish(*pending)

    chunks_per_quarter = quarter // BAND

    def mix(c, _):
        mixed = pl.ds(pl.multiple_of(c * BAND, BAND), BAND)
        natural = pl.ds(MAX_ROW_STRIDE * (c % chunks_per_quarter) * BAND + c // chunks_per_quarter,
                        BAND, stride=MAX_ROW_STRIDE)
        rows = [natural] + [mixed] * (N_GROUPS - 1)
        lse = [lse_nat_ref[g, rows[g], :] for g in range(N_GROUPS)]
        top = jnp.maximum(jnp.maximum(lse[0], lse[1]), lse[2])
        w = [jnp.exp(x - top) for x in lse]
        num = (w[0] * o_nat_ref[0, rows[0], :] + w[1] * o_nat_ref[1, rows[1], :]
               + w[2] * o_nat_ref[2, rows[2], :])
        o_nat_ref[0, natural, :] = num / (w[0] + w[1] + w[2])
        return 0

    lax.fori_loop(0, n_tiles, mix, 0)
    o_ref[...] = o_nat_ref[0].astype(o_ref.dtype)


def _attention(qkv, batch, seq):
    heads = HEADS_PER_GROUP
    qkv_spec = pl.BlockSpec((None, None, 3, seq, HEAD_DIM), lambda b, h: (h, b, 0, 0, 0))
    scratch = 2 * _nbytes((N_GROUPS, seq, HEAD_DIM), F32)
    return pl.pallas_call(
        _attn_kernel,
        grid=(batch, heads),
        in_specs=[qkv_spec] * N_GROUPS,
        out_specs=pl.BlockSpec((seq, HEAD_DIM), lambda b, h: (b, h)),
        out_shape=jax.ShapeDtypeStruct((batch * seq, GROUP_WIDTH), BF16),
        scratch_shapes=[pltpu.VMEM((N_GROUPS, seq, HEAD_DIM), F32),
                        pltpu.VMEM((N_GROUPS, seq, HEAD_DIM), F32),
                        pltpu.VMEM((BAND, 2 * BAND), F32)],
        compiler_params=pltpu.CompilerParams(
            dimension_semantics=("parallel", "parallel"),
            vmem_limit_bytes=_vmem_limit(10 * _nbytes((seq, HEAD_DIM), BF16), scratch + 4 * 1024 * 1024)),
        name="dilated_attention",
    )(*qkv)


def _merge_kernel(attn_ref, conv_ref, x_ref, wao_ref, wco_ref, wga_ref, wgc_ref, o_ref):
    x, attn, conv = x_ref[...], attn_ref[...], conv_ref[...]
    slab = o_ref.shape[1] // MERGE_SPLIT
    for k in range(MERGE_SPLIT):
        cols = slice(k * slab, (k + 1) * slab)
        a = jnp.dot(attn, wao_ref[:, cols], preferred_element_type=F32)
        ga = _sigmoid(jnp.dot(x, wga_ref[:, cols], preferred_element_type=F32))
        c = jnp.dot(conv, wco_ref[:, cols], preferred_element_type=F32)
        gc = _sigmoid(jnp.dot(x, wgc_ref[:, cols], preferred_element_type=F32))
        o_ref[:, cols] = (ga * a + gc * c).astype(o_ref.dtype)


def _gated_merge(attn, conv, xb, w_attn_o, w_conv_o, w_gates, *, tm, tn):
    m = attn.shape[0]
    assert w_gates.shape == (D_MODEL, 2 * D_MODEL)
    blocks = (_nbytes((tm, GROUP_WIDTH), BF16) + _nbytes((tm, CONV_WIDTH), BF16)
              + _nbytes((tm, D_MODEL), BF16) + _nbytes((tm, tn), BF16)
              + _nbytes((GROUP_WIDTH + CONV_WIDTH + 2 * D_MODEL, tn), BF16))
    return pl.pallas_call(
        _merge_kernel,
        grid=(m // tm, D_MODEL // tn),
        in_specs=[pl.BlockSpec((tm, GROUP_WIDTH), lambda i, j: (i, 0)),
                  pl.BlockSpec((tm, CONV_WIDTH), lambda i, j: (i, 0)),
                  pl.BlockSpec((tm, D_MODEL), lambda i, j: (i, 0)),
                  pl.BlockSpec((GROUP_WIDTH, tn), lambda i, j: (0, j)),
                  pl.BlockSpec((CONV_WIDTH, tn), lambda i, j: (0, j)),
                  pl.BlockSpec((D_MODEL, tn), lambda i, j: (0, j)),
                  pl.BlockSpec((D_MODEL, tn), lambda i, j: (0, D_MODEL // tn + j))],
        out_specs=pl.BlockSpec((tm, tn), lambda i, j: (i, j)),
        out_shape=jax.ShapeDtypeStruct((m, D_MODEL), BF16),
        compiler_params=pltpu.CompilerParams(
            dimension_semantics=("parallel", "arbitrary"),
            vmem_limit_bytes=_vmem_limit(blocks, 5 * _nbytes((tm, tn), F32))),
        name="gated_merge",
    )(attn, conv, xb, w_attn_o, w_conv_o, w_gates, w_gates)


def _layer_norm(z, g, b):
    mu = jnp.mean(z, axis=-1, keepdims=True)
    zc = z - mu
    var = jnp.mean(zc * zc, axis=-1, keepdims=True)
    return zc * lax.rsqrt(var + LN_EPS) * g + b


def _out_ln_kernel(mi_ref, w_ref, x_ref, g_ref, b_ref, o_ref, ob_ref):
    y = jnp.dot(mi_ref[...], w_ref[...], preferred_element_type=F32)
    h = _layer_norm(ALPHA * x_ref[...] + y, g_ref[...], b_ref[...])
    o_ref[...] = h
    ob_ref[...] = h.astype(ob_ref.dtype)


def _out_proj_ln(merged, w_out, x, g, b, *, tm):
    m = merged.shape[0]
    blocks = (2 * _nbytes((tm, D_MODEL), BF16) + _nbytes((D_MODEL, D_MODEL), BF16)
              + 2 * _nbytes((tm, D_MODEL), F32))
    row = pl.BlockSpec((tm, D_MODEL), lambda i: (i, 0))
    vec = pl.BlockSpec((1, D_MODEL), lambda i: (0, 0))
    return pl.pallas_call(
        _out_ln_kernel,
        grid=(m // tm,),
        in_specs=[row, pl.BlockSpec((D_MODEL, D_MODEL), lambda i: (0, 0)), row, vec, vec],
        out_specs=[row, row],
        out_shape=[jax.ShapeDtypeStruct((m, D_MODEL), F32), jax.ShapeDtypeStruct((m, D_MODEL), BF16)],
        compiler_params=pltpu.CompilerParams(
            dimension_semantics=("parallel",),
            vmem_limit_bytes=_vmem_limit(blocks, 3 * _nbytes((tm, D_MODEL), F32))),
        name="out_proj_ln",
    )(merged, w_out, x, g, b)


def _ffn_kernel(hb_ref, h_hbm_ref, wg_ref, wu_ref, wd_ref, g_ref, b_ref, o_ref, res_ref, res_sem):
    i, f = pl.program_id(0), pl.program_id(1)
    tm = o_ref.shape[0]
    residual_copy = pltpu.make_async_copy(
        h_hbm_ref.at[pl.ds(pl.multiple_of(i * tm, tm), tm), :], res_ref, res_sem)

    @pl.when(f == 0)
    def _():
        residual_copy.start()
        o_ref[...] = jnp.zeros_like(o_ref)

    hb = hb_ref[...]
    gate = jnp.dot(hb, wg_ref[...], preferred_element_type=F32)
    up = jnp.dot(hb, wu_ref[...], preferred_element_type=F32)
    hidden = (gate * _sigmoid(gate) * up).astype(BF16)
    o_ref[...] += jnp.dot(hidden, wd_ref[...], preferred_element_type=F32)

    @pl.when(f == pl.num_programs(1) - 1)
    def _():
        residual_copy.wait()
        o_ref[...] = _layer_norm(ALPHA * res_ref[...] + o_ref[...], g_ref[...], b_ref[...])


def _ffn_ln(h, hb, w_gate, w_up, w_down, g, b, *, tm, tf):
    m = h.shape[0]
    blocks = (_nbytes((tm, D_MODEL), BF16) + _nbytes((tm, D_MODEL), F32)
              + 2 * _nbytes((D_MODEL, tf), BF16) + _nbytes((tf, D_MODEL), BF16))
    scratch = _nbytes((tm, D_MODEL), F32)
    row = pl.BlockSpec((tm, D_MODEL), lambda i, f: (i, 0))
    vec = pl.BlockSpec((1, D_MODEL), lambda i, f: (0, 0))
    return pl.pallas_call(
        _ffn_kernel,
        grid=(m // tm, FFN_HIDDEN // tf),
        in_specs=[row,
                  pl.BlockSpec(memory_space=pl.ANY),
                  pl.BlockSpec((D_MODEL, tf), lambda i, f: (0, f)),
                  pl.BlockSpec((D_MODEL, tf), lambda i, f: (0, f)),
                  pl.BlockSpec((tf, D_MODEL), lambda i, f: (f, 0)),
                  vec, vec],
        out_specs=row,
        out_shape=jax.ShapeDtypeStruct((m, D_MODEL), F32),
        scratch_shapes=[pltpu.VMEM((tm, D_MODEL), F32), pltpu.SemaphoreType.DMA],
        compiler_params=pltpu.CompilerParams(
            dimension_semantics=("arbitrary", "arbitrary"),
            vmem_limit_bytes=_vmem_limit(blocks, scratch + 6 * _nbytes((tm, tf), F32))),
        name="ffn_ln",
    )(hb, h, w_gate, w_up, w_down, g, b)


def kernel(x, w_in, conv_w, w_attn_o, w_conv_o, w_out, ln1_g, ln1_b,
           w_ffn_gate, w_ffn_up, w_ffn_down, ln2_g, ln2_b):
    batch, seq, d = x.shape
    assert d == D_MODEL and w_in.shape == (DEPTH, D_MODEL, IN_COLS)
    assert seq % PROJ_ROWS == 0 and all(PROJ_ROWS % (dil * 16) == 0 for _, dil in DILATED_GROUPS)
    assert all(seq % (dil * BAND) == 0 for _, dil in DILATED_GROUPS)
    h = x.reshape(batch * seq, d)
    for layer in range(DEPTH):
        assert N_GROUPS == 3
        w_in_l = w_in[layer]
        sections = ATTN_WIDTH // PROJ_COLS
        row_blocks = D_MODEL // (batch * seq // PROJ_ROWS)
        w_qkv0_b = jnp.concatenate(
            [w_in_l[:, s * ATTN_WIDTH:s * ATTN_WIDTH + GROUP_WIDTH] for s in range(3)], axis=1).astype(BF16)

        def group_weight_rider(group, row_tile_of):
            return _CastRider(w_in_l, (row_blocks, PROJ_COLS),
                              lambda b, t, j: (row_tile_of(b, t, j), j * sections + group),
                              (D_MODEL, 3 * GROUP_WIDTH), lambda b, t, j: (row_tile_of(b, t, j), j))

        def conv_weight_rider(row_tile_of):
            half = 3 * CONV_WIDTH // 2
            return _CastRider(w_in_l, (row_blocks, half),
                              lambda b, t, j: (row_tile_of(b, t, j), COL_U // half + jnp.minimum(j, 1)),
                              (D_MODEL, 3 * CONV_WIDTH), lambda b, t, j: (row_tile_of(b, t, j), jnp.minimum(j, 1)))

        def gate_weight_rider(tn):
            return _CastRider(w_in_l, (D_MODEL // batch, 2 * D_MODEL // (CONV_WIDTH // tn)),
                              lambda b, j: (b, COL_GA // (2 * D_MODEL // (CONV_WIDTH // tn)) + j),
                              (D_MODEL, 2 * D_MODEL), lambda b, j: (b, j))

        qkv0, hb, w_attn_o_b, w_conv_o_b, w_qkv1_b = _qkv_proj(
            h, w_qkv0_b, 0, batch, seq,
            lambda step_of, row_tile_of: [_row_block_rider(w_attn_o[layer], step_of),
                                          _row_block_rider(w_conv_o[layer], step_of),
                                          group_weight_rider(1, row_tile_of)])
        qkv1, w_gate_b, w_qkv2_b = _qkv_proj(
            hb, w_qkv1_b, 1, batch, seq,
            lambda step_of, row_tile_of: [_row_block_rider(w_ffn_gate[layer], step_of),
                                          group_weight_rider(2, row_tile_of)])
        qkv2, w_up_b, w_ucb_b = _qkv_proj(
            hb, w_qkv2_b, 2, batch, seq,
            lambda step_of, row_tile_of: [_row_block_rider(w_ffn_up[layer], step_of),
                                          conv_weight_rider(row_tile_of)])
        conv_tn = 512
        conv, w_down_b, w_out_b, w_gates_b = _conv_proj(
            hb, w_ucb_b, conv_w[layer], batch, seq,
            lambda step_of: [_row_block_rider(w_ffn_down[layer], step_of),
                             _row_block_rider(w_out[layer], step_of),
                             gate_weight_rider(conv_tn)],
            tn=conv_tn)
        attn = _attention([qkv0, qkv1, qkv2], batch, seq)
        merged = _gated_merge(attn, conv, hb, w_attn_o_b, w_conv_o_b, w_gates_b, tm=1024, tn=512)
        h, hb = _out_proj_ln(merged, w_out_b, h, ln1_g[layer][None, :], ln1_b[layer][None, :], tm=512)
        h = _ffn_ln(h, hb, w_gate_b, w_up_b, w_down_b, ln2_g[layer][None, :], ln2_b[layer][None, :],
                    tm=1024, tf=512)
    return h.reshape(batch, seq, d)
```

```python
import functools
import math
from typing import Callable, NamedTuple

import jax
import jax.numpy as jnp
from jax import lax
from jax.experimental import pallas as pl
from jax.experimental.pallas import tpu as pltpu

D_MODEL = 2048
HEAD_DIM = 128
HEADS_PER_GROUP = 8
DILATED_GROUPS = ((128, 1), (512, 4), (2048, 16))
N_GROUPS = len(DILATED_GROUPS)
GROUP_WIDTH = HEADS_PER_GROUP * HEAD_DIM
ATTN_WIDTH = N_GROUPS * GROUP_WIDTH
CONV_WIDTH = D_MODEL
CONV_K = 3
FFN_HIDDEN = 5632
DEPTH = 1
ALPHA = (2 * DEPTH) ** 0.25
LN_EPS = 1e-5

COL_U = 3 * ATTN_WIDTH
COL_C = COL_U + CONV_WIDTH
COL_B = COL_C + CONV_WIDTH
COL_GA = COL_B + CONV_WIDTH
COL_GC = COL_GA + D_MODEL
IN_COLS = COL_GC + D_MODEL

V7X_VMEM_BYTES = 64 * 1024 * 1024
BAND = 128
ATTN_TILES_IN_FLIGHT = 8
MAX_ROW_STRIDE = 4
CAST_RIDER_STEPS = 32
PROJ_ROWS = 1024
PROJ_COLS = 1024
OUT_LN_SPLIT = 4
MERGE_SPLIT = 2

F32 = jnp.float32
BF16 = jnp.bfloat16


def _vmem_limit(pipelined_block_bytes, resident_bytes):
    need = 2 * pipelined_block_bytes + resident_bytes
    return min(int(need * 1.25), V7X_VMEM_BYTES - 4 * 1024 * 1024)


def _nbytes(shape, dtype):
    return math.prod(shape) * jnp.dtype(dtype).itemsize


def _sigmoid(x):
    return 0.5 * (jnp.tanh(0.5 * x) + 1.0)


def _store_heads_by_residue(y, o_ref, y_ref, y2_ref, dilation):
    if dilation == 1:
        for head in range(HEADS_PER_GROUP):
            o_ref[head, 0] = y[:, head * HEAD_DIM:(head + 1) * HEAD_DIM].astype(o_ref.dtype)
        return
    f1 = min(dilation, MAX_ROW_STRIDE)
    f2 = dilation // f1
    assert f1 * f2 == dilation and f2 <= MAX_ROW_STRIDE
    rows = y_ref.shape[1]
    per_residue = rows // dilation
    for chunk in range(y.shape[0] // rows):
        dst = slice(chunk * per_residue, (chunk + 1) * per_residue)
        for head in range(HEADS_PER_GROUP):
            src = y_ref.at[head]
            src[...] = y[chunk * rows:(chunk + 1) * rows, head * HEAD_DIM:(head + 1) * HEAD_DIM]
            if f2 > 1:
                for r0 in range(f1):
                    y2_ref[head, r0 * (rows // f1):(r0 + 1) * (rows // f1), :] = src[pl.ds(r0, rows // f1, stride=f1), :]
                src = y2_ref.at[head]
            for r0 in range(f1):
                for r1 in range(f2):
                    start = r0 * (rows // f1) + r1 if f2 > 1 else r0
                    stride = f2 if f2 > 1 else f1
                    o_ref[head, r1 * f1 + r0, dst, :] = (
                        src[pl.ds(start, per_residue, stride=stride), :].astype(o_ref.dtype))


def _cast_riders(rider_refs, rider_out_refs):
    for src, dst in zip(rider_refs, rider_out_refs, strict=True):
        dst[...] = src[...].astype(dst.dtype)


def _qkv_proj_first_kernel(*refs, n_riders):
    x_ref, w_ref = refs[:2]
    rider_refs = refs[2:2 + n_riders]
    o_ref, xb_ref = refs[2 + n_riders:4 + n_riders]
    _cast_riders(rider_refs, refs[4 + n_riders:])

    @pl.when(pl.program_id(2) == 0)
    def _():
        xb_ref[...] = x_ref[...].astype(BF16)

    y = jnp.dot(xb_ref[...], w_ref[...], preferred_element_type=F32)
    _store_heads_by_residue(y, o_ref, None, None, 1)


def _qkv_proj_kernel(*refs, n_riders, dilation):
    xb_ref, w_ref = refs[:2]
    rider_refs = refs[2:2 + n_riders]
    o_ref = refs[2 + n_riders]
    rider_out_refs = refs[3 + n_riders:3 + 2 * n_riders]
    y_ref, y2_ref = refs[3 + 2 * n_riders:]
    _cast_riders(rider_refs, rider_out_refs)
    y = jnp.dot(xb_ref[...], w_ref[...], preferred_element_type=F32)
    _store_heads_by_residue(y, o_ref, y_ref, y2_ref, dilation)


class _CastRider(NamedTuple):
    array: jax.Array
    block: tuple
    in_index: Callable
    out_shape: tuple
    out_index: Callable


def _cast_kernel(w_ref, o_ref):
    o_ref[...] = w_ref[...].astype(o_ref.dtype)


def _cast_group_weights(w_in, group, *, rows):
    sections = ATTN_WIDTH // GROUP_WIDTH
    blocks = _nbytes((rows, GROUP_WIDTH), F32) + _nbytes((rows, GROUP_WIDTH), BF16)
    return pl.pallas_call(
        _cast_kernel,
        grid=(3, D_MODEL // rows),
        in_specs=[pl.BlockSpec((rows, GROUP_WIDTH), lambda j, i: (i, j * sections + group))],
        out_specs=pl.BlockSpec((rows, GROUP_WIDTH), lambda j, i: (i, j)),
        out_shape=jax.ShapeDtypeStruct((D_MODEL, 3 * GROUP_WIDTH), BF16),
        compiler_params=pltpu.CompilerParams(
            dimension_semantics=("parallel", "parallel"),
            vmem_limit_bytes=_vmem_limit(blocks, 0)),
        name="cast_group_weights",
    )(w_in)


def _row_block_rider(w, step_of):
    rows = w.shape[0] // CAST_RIDER_STEPS
    assert rows * CAST_RIDER_STEPS == w.shape[0] and rows % 16 == 0

    def index(*idx):
        return jnp.minimum(step_of(*idx), CAST_RIDER_STEPS - 1), 0

    return _CastRider(w, (rows, w.shape[1]), index, w.shape, index)


def _cast_rider_specs(riders):
    in_specs = [pl.BlockSpec(r.block, r.in_index) for r in riders]
    out_specs = [pl.BlockSpec(r.block, r.out_index) for r in riders]
    shapes = [jax.ShapeDtypeStruct(r.out_shape, BF16) for r in riders]
    return in_specs, out_specs, shapes, sum(_nbytes(r.block, F32) * 3 // 2 for r in riders)


def _qkv_proj(x, w_qkv, group, batch, seq, make_riders):
    dilation = DILATED_GROUPS[group][1]
    first = x.dtype == F32
    assert first == (group == 0) and dilation == (1 if first else dilation)
    length = seq // dilation
    tiles = seq // PROJ_ROWS
    per_residue = PROJ_ROWS // dilation
    assert PROJ_COLS == GROUP_WIDTH and w_qkv.shape == (D_MODEL, 3 * GROUP_WIDTH)
    row_spec = pl.BlockSpec((PROJ_ROWS, D_MODEL), lambda b, t, j: (b * tiles + t, 0))
    qkv_spec = pl.BlockSpec((HEADS_PER_GROUP, None, None, dilation, per_residue, HEAD_DIM),
                            lambda b, t, j: (0, b, j, 0, t, 0))
    qkv_shape = jax.ShapeDtypeStruct((HEADS_PER_GROUP, batch, 3, dilation, length, HEAD_DIM), BF16)
    blocks = (_nbytes((PROJ_ROWS, D_MODEL), x.dtype) + _nbytes((D_MODEL, PROJ_COLS), BF16)
              + _nbytes((PROJ_ROWS, PROJ_COLS), BF16) + (_nbytes((PROJ_ROWS, D_MODEL), BF16) if first else 0))
    regroup = [] if first else [pltpu.VMEM((HEADS_PER_GROUP, PROJ_ROWS, HEAD_DIM), F32)] * 2
    w_spec = pl.BlockSpec((D_MODEL, PROJ_COLS), lambda b, t, j: (0, j))
    riders = make_riders(lambda b, t, j: (b * tiles + t) * 3 + j, lambda b, t, j: b * tiles + t)
    rider_in_specs, rider_specs, rider_shapes, rider_bytes = _cast_rider_specs(riders)
    own_out_specs, own_out_shapes = [qkv_spec], [qkv_shape]
    if first:
        own_out_specs.append(row_spec)
        own_out_shapes.append(jax.ShapeDtypeStruct(x.shape, BF16))
        body = functools.partial(_qkv_proj_first_kernel, n_riders=len(riders))
    else:
        body = functools.partial(_qkv_proj_kernel, n_riders=len(riders), dilation=dilation)
    out = pl.pallas_call(
        body,
        grid=(batch, tiles, 3),
        in_specs=[row_spec, w_spec] + rider_in_specs,
        out_specs=own_out_specs + rider_specs,
        out_shape=own_out_shapes + rider_shapes,
        scratch_shapes=regroup,
        compiler_params=pltpu.CompilerParams(
            dimension_semantics=("arbitrary", "arbitrary", "arbitrary"),
            vmem_limit_bytes=_vmem_limit(blocks + rider_bytes,
                                         (0 if first else 2 * _nbytes((PROJ_ROWS, PROJ_COLS), F32))
                                         + 2 * _nbytes((PROJ_ROWS, PROJ_COLS), F32))),
        name=f"qkv_proj_dilation{dilation}",
    )(x, w_qkv, *[r.array for r in riders])
    return (out[0].reshape(HEADS_PER_GROUP, batch, 3, seq, HEAD_DIM),) + tuple(out[1:])


def _conv_proj_kernel(*refs, n_riders):
    x_ref, wu_ref, wc_ref, wb_ref, cw_ref = refs[:5]
    o_ref = refs[5 + n_riders]
    _cast_riders(refs[5:5 + n_riders], refs[6 + n_riders:])
    x = x_ref[...]
    z = (jnp.dot(x, wc_ref[...], preferred_element_type=F32)
         * jnp.dot(x, wu_ref[...], preferred_element_type=F32))
    cw = cw_ref[...]
    row = lax.broadcasted_iota(jnp.int32, z.shape, 0)
    y = cw[0:1, :] * z
    for tap in range(1, CONV_K):
        shifted = jnp.where(row >= tap, pltpu.roll(z, tap, axis=0), 0.0)
        y = y + cw[tap:tap + 1, :] * shifted
    o_ref[...] = (jnp.dot(x, wb_ref[...], preferred_element_type=F32) * y).astype(o_ref.dtype)


def _conv_proj(xb, w_ucb, conv_w, batch, seq, make_riders, *, tn):
    def w_spec(section):
        return pl.BlockSpec((D_MODEL, tn), lambda b, j: (0, section * (CONV_WIDTH // tn) + j))

    col_blocks = CONV_WIDTH // tn
    assert w_ucb.shape == (D_MODEL, 3 * CONV_WIDTH)
    riders = make_riders(lambda b, j: b * col_blocks + j)
    rider_in_specs, rider_specs, rider_shapes, rider_bytes = _cast_rider_specs(riders)
    blocks = (_nbytes((seq, D_MODEL), BF16) + 3 * _nbytes((D_MODEL, tn), BF16)
              + _nbytes((CONV_K, tn), F32) + _nbytes((seq, tn), BF16) + rider_bytes)
    return pl.pallas_call(
        functools.partial(_conv_proj_kernel, n_riders=len(riders)),
        grid=(batch, col_blocks),
        in_specs=[pl.BlockSpec((seq, D_MODEL), lambda b, j: (b, 0)),
                  w_spec(0), w_spec(1), w_spec(2),
                  pl.BlockSpec((CONV_K, tn), lambda b, j: (0, j))] + rider_in_specs,
        out_specs=[pl.BlockSpec((seq, tn), lambda b, j: (b, j))] + rider_specs,
        out_shape=[jax.ShapeDtypeStruct((batch * seq, CONV_WIDTH), BF16)] + rider_shapes,
        compiler_params=pltpu.CompilerParams(
            dimension_semantics=("arbitrary", "arbitrary"),
            vmem_limit_bytes=_vmem_limit(blocks, 5 * _nbytes((seq, tn), F32))),
        name="conv_proj",
    )(xb, w_ucb, w_ucb, w_ucb, conv_w, *[r.array for r in riders])


def _attn_kernel(qkv1_ref, qkv2_ref, qkv3_ref, o_ref, o_nat_ref, lse_nat_ref, bias_ref):
    seq = qkv1_ref.shape[1]
    n_tiles = seq // BAND
    scale = HEAD_DIM ** -0.5
    qi = lax.broadcasted_iota(jnp.int32, (BAND, 2 * BAND), 0)
    kj = lax.broadcasted_iota(jnp.int32, (BAND, 2 * BAND), 1)
    visible = ((kj < BAND) & (qi <= kj)) | ((kj >= BAND) & (qi >= kj - BAND))
    bias_ref[...] = jnp.where(visible, 0.0, -jnp.inf)

    def probabilities(q_ref, k_ref, n, has_prev):
        q = q_ref[n * BAND:(n + 1) * BAND, :]
        keys = slice((n - 1) * BAND, (n + 1) * BAND) if has_prev else slice(n * BAND, (n + 1) * BAND)
        bias = bias_ref[...] if has_prev else bias_ref[:, BAND:]
        s = lax.dot_general(q, k_ref[keys, :], (((1,), (1,)), ((), ())),
                            preferred_element_type=F32) * scale + bias
        m = s.max(-1, keepdims=True)
        p = jnp.exp(s - m)
        l = p.sum(-1, keepdims=True)
        return p.astype(BF16), l, m + jnp.log(l), keys

    assert DILATED_GROUPS[0][1] == 1 and all(d % MAX_ROW_STRIDE == 0 for _, d in DILATED_GROUPS[1:])
    quarter = seq // MAX_ROW_STRIDE

    def result_rows(g, n):
        dilation = DILATED_GROUPS[g][1]
        if dilation == 1:
            return pl.ds(n * BAND, BAND)
        residue, t = divmod(n, n_tiles // dilation)
        step = dilation // MAX_ROW_STRIDE
        start = (residue % MAX_ROW_STRIDE) * quarter + step * t * BAND + residue // MAX_ROW_STRIDE
        return pl.ds(start, BAND, stride=step)

    def finish(g, v_ref, n, p, l, lse, keys):
        rows = result_rows(g, n)
        o_nat_ref[g, rows, :] = jnp.dot(p, v_ref[keys, :], preferred_element_type=F32) / l
        lse_nat_ref[g, rows, :] = jnp.broadcast_to(lse, (BAND, HEAD_DIM))

    groups = [(r.at[0], r.at[1], r.at[2]) for r in (qkv1_ref, qkv2_ref, qkv3_ref)]
    in_flight = []
    for g, (q_ref, k_ref, v_ref) in enumerate(groups):
        tiles_per_residue = n_tiles // DILATED_GROUPS[g][1]
        for n in range(n_tiles):
            has_prev = n % tiles_per_residue > 0
            in_flight.append((g, v_ref, n) + probabilities(q_ref, k_ref, n, has_prev))
            if len(in_flight) > ATTN_TILES_IN_FLIGHT:
                finish(*in_flight.pop(0))
    for pending in in_flight:
        finish(*pending)

    chunks_per_quarter = quarter // BAND

    def mix(c, _):
        mixed = pl.ds(pl.multiple_of(c * BAND, BAND), BAND)
        natural = pl.ds(MAX_ROW_STRIDE * (c % chunks_per_quarter) * BAND + c // chunks_per_quarter,
                        BAND, stride=MAX_ROW_STRIDE)
        rows = [natural] + [mixed] * (N_GROUPS - 1)
        lse = [lse_nat_ref[g, rows[g], :] for g in range(N_GROUPS)]
        top = jnp.maximum(jnp.maximum(lse[0], lse[1]), lse[2])
        w = [jnp.exp(x - top) for x in lse]
        num = (w[0] * o_nat_ref[0, rows[0], :] + w[1] * o_nat_ref[1, rows[1], :]
               + w[2] * o_nat_ref[2, rows[2], :])
        o_nat_ref[0, natural, :] = num / (w[0] + w[1] + w[2])
        return 0

    lax.fori_loop(0, n_tiles, mix, 0)
    o_ref[...] = o_nat_ref[0].astype(o_ref.dtype)


def _attention(qkv, batch, seq):
    heads = HEADS_PER_GROUP
    qkv_spec = pl.BlockSpec((None, None, 3, seq, HEAD_DIM), lambda b, h: (h, b, 0, 0, 0))
    scratch = 2 * _nbytes((N_GROUPS, seq, HEAD_DIM), F32)
    return pl.pallas_call(
        _attn_kernel,
        grid=(batch, heads),
        in_specs=[qkv_spec] * N_GROUPS,
        out_specs=pl.BlockSpec((seq, HEAD_DIM), lambda b, h: (b, h)),
        out_shape=jax.ShapeDtypeStruct((batch * seq, GROUP_WIDTH), BF16),
        scratch_shapes=[pltpu.VMEM((N_GROUPS, seq, HEAD_DIM), F32),
                        pltpu.VMEM((N_GROUPS, seq, HEAD_DIM), F32),
                        pltpu.VMEM((BAND, 2 * BAND), F32)],
        compiler_params=pltpu.CompilerParams(
            dimension_semantics=("parallel", "parallel"),
            vmem_limit_bytes=_vmem_limit(10 * _nbytes((seq, HEAD_DIM), BF16), scratch + 4 * 1024 * 1024)),
        name="dilated_attention",
    )(*qkv)


def _merge_kernel(attn_ref, conv_ref, x_ref, wao_ref, wco_ref, wga_ref, wgc_ref, o_ref):
    x, attn, conv = x_ref[...], attn_ref[...], conv_ref[...]
    slab = o_ref.shape[1] // MERGE_SPLIT
    for k in range(MERGE_SPLIT):
        cols = slice(k * slab, (k + 1) * slab)
        a = jnp.dot(attn, wao_ref[:, cols], preferred_element_type=F32)
        ga = _sigmoid(jnp.dot(x, wga_ref[:, cols], preferred_element_type=F32))
        c = jnp.dot(conv, wco_ref[:, cols], preferred_element_type=F32)
        gc = _sigmoid(jnp.dot(x, wgc_ref[:, cols], preferred_element_type=F32))
        o_ref[:, cols] = (ga * a + gc * c).astype(o_ref.dtype)


def _gated_merge(attn, conv, xb, w_attn_o, w_conv_o, w_gates, *, tm, tn):
    m = attn.shape[0]
    assert w_gates.shape == (D_MODEL, 2 * D_MODEL)
    blocks = (_nbytes((tm, GROUP_WIDTH), BF16) + _nbytes((tm, CONV_WIDTH), BF16)
              + _nbytes((tm, D_MODEL), BF16) + _nbytes((tm, tn), BF16)
              + _nbytes((GROUP_WIDTH + CONV_WIDTH + 2 * D_MODEL, tn), BF16))
    return pl.pallas_call(
        _merge_kernel,
        grid=(m // tm, D_MODEL // tn),
        in_specs=[pl.BlockSpec((tm, GROUP_WIDTH), lambda i, j: (i, 0)),
                  pl.BlockSpec((tm, CONV_WIDTH), lambda i, j: (i, 0)),
                  pl.BlockSpec((tm, D_MODEL), lambda i, j: (i, 0)),
                  pl.BlockSpec((GROUP_WIDTH, tn), lambda i, j: (0, j)),
                  pl.BlockSpec((CONV_WIDTH, tn), lambda i, j: (0, j)),
                  pl.BlockSpec((D_MODEL, tn), lambda i, j: (0, j)),
                  pl.BlockSpec((D_MODEL, tn), lambda i, j: (0, D_MODEL // tn + j))],
        out_specs=pl.BlockSpec((tm, tn), lambda i, j: (i, j)),
        out_shape=jax.ShapeDtypeStruct((m, D_MODEL), BF16),
        compiler_params=pltpu.CompilerParams(
            dimension_semantics=("parallel", "arbitrary"),
            vmem_limit_bytes=_vmem_limit(blocks, 5 * _nbytes((tm, tn), F32))),
        name="gated_merge",
    )(attn, conv, xb, w_attn_o, w_conv_o, w_gates, w_gates)


def _layer_norm(z, g, b):
    mu = jnp.mean(z, axis=-1, keepdims=True)
    zc = z - mu
    var = jnp.mean(zc * zc, axis=-1, keepdims=True)
    return zc * lax.rsqrt(var + LN_EPS) * g + b


def _out_ln_kernel(mi_ref, w_ref, x_ref, g_ref, b_ref, o_ref, ob_ref):
    part = mi_ref.shape[0] // OUT_LN_SPLIT
    for rows in (slice(k * part, (k + 1) * part) for k in range(OUT_LN_SPLIT)):
        y = jnp.dot(mi_ref[rows, :], w_ref[...], preferred_element_type=F32)
        h = _layer_norm(ALPHA * x_ref[rows, :] + y, g_ref[...], b_ref[...])
        o_ref[rows, :] = h
        ob_ref[rows, :] = h.astype(ob_ref.dtype)


def _out_proj_ln(merged, w_out, x, g, b, *, tm):
    m = merged.shape[0]
    blocks = (2 * _nbytes((tm, D_MODEL), BF16) + _nbytes((D_MODEL, D_MODEL), BF16)
              + 2 * _nbytes((tm, D_MODEL), F32))
    row = pl.BlockSpec((tm, D_MODEL), lambda i: (i, 0))
    vec = pl.BlockSpec((1, D_MODEL), lambda i: (0, 0))
    return pl.pallas_call(
        _out_ln_kernel,
        grid=(m // tm,),
        in_specs=[row, pl.BlockSpec((D_MODEL, D_MODEL), lambda i: (0, 0)), row, vec, vec],
        out_specs=[row, row],
        out_shape=[jax.ShapeDtypeStruct((m, D_MODEL), F32), jax.ShapeDtypeStruct((m, D_MODEL), BF16)],
        compiler_params=pltpu.CompilerParams(
            dimension_semantics=("parallel",),
            vmem_limit_bytes=_vmem_limit(blocks, 3 * _nbytes((tm, D_MODEL), F32))),
        name="out_proj_ln",
    )(merged, w_out, x, g, b)


def _ffn_kernel(hb_ref, h_hbm_ref, wg_ref, wu_ref, wd_ref, g_ref, b_ref, o_ref, res_ref, res_sem):
    i, f = pl.program_id(0), pl.program_id(1)
    tm = o_ref.shape[0]
    residual_copy = pltpu.make_async_copy(
        h_hbm_ref.at[pl.ds(pl.multiple_of(i * tm, tm), tm), :], res_ref, res_sem)

    @pl.when(f == 0)
    def _():
        residual_copy.start()
        o_ref[...] = jnp.zeros_like(o_ref)

    hb = hb_ref[...]
    gate = jnp.dot(hb, wg_ref[...], preferred_element_type=F32)
    up = jnp.dot(hb, wu_ref[...], preferred_element_type=F32)
    hidden = (gate * _sigmoid(gate) * up).astype(BF16)
    o_ref[...] += jnp.dot(hidden, wd_ref[...], preferred_element_type=F32)

    @pl.when(f == pl.num_programs(1) - 1)
    def _():
        residual_copy.wait()
        o_ref[...] = _layer_norm(ALPHA * res_ref[...] + o_ref[...], g_ref[...], b_ref[...])


def _ffn_ln(h, hb, w_gate, w_up, w_down, g, b, *, tm, tf):
    m = h.shape[0]
    blocks = (_nbytes((tm, D_MODEL), BF16) + _nbytes((tm, D_MODEL), F32)
              + 2 * _nbytes((D_MODEL, tf), BF16) + _nbytes((tf, D_MODEL), BF16))
    scratch = _nbytes((tm, D_MODEL), F32)
    row = pl.BlockSpec((tm, D_MODEL), lambda i, f: (i, 0))
    vec = pl.BlockSpec((1, D_MODEL), lambda i, f: (0, 0))
    return pl.pallas_call(
        _ffn_kernel,
        grid=(m // tm, FFN_HIDDEN // tf),
        in_specs=[row,
                  pl.BlockSpec(memory_space=pl.ANY),
                  pl.BlockSpec((D_MODEL, tf), lambda i, f: (0, f)),
                  pl.BlockSpec((D_MODEL, tf), lambda i, f: (0, f)),
                  pl.BlockSpec((tf, D_MODEL), lambda i, f: (f, 0)),
                  vec, vec],
        out_specs=row,
        out_shape=jax.ShapeDtypeStruct((m, D_MODEL), F32),
        scratch_shapes=[pltpu.VMEM((tm, D_MODEL), F32), pltpu.SemaphoreType.DMA],
        compiler_params=pltpu.CompilerParams(
            dimension_semantics=("arbitrary", "arbitrary"),
            vmem_limit_bytes=_vmem_limit(blocks, scratch + 6 * _nbytes((tm, tf), F32))),
        name="ffn_ln",
    )(hb, h, w_gate, w_up, w_down, g, b)


def kernel(x, w_in, conv_w, w_attn_o, w_conv_o, w_out, ln1_g, ln1_b,
           w_ffn_gate, w_ffn_up, w_ffn_down, ln2_g, ln2_b):
    batch, seq, d = x.shape
    assert d == D_MODEL and w_in.shape == (DEPTH, D_MODEL, IN_COLS)
    assert seq % PROJ_ROWS == 0 and all(PROJ_ROWS % (dil * 16) == 0 for _, dil in DILATED_GROUPS)
    assert all(seq % (dil * BAND) == 0 for _, dil in DILATED_GROUPS)
    h = x.reshape(batch * seq, d)
    for layer in range(DEPTH):
        assert N_GROUPS == 3
        w_in_l = w_in[layer]
        sections = ATTN_WIDTH // PROJ_COLS
        row_blocks = D_MODEL // (batch * seq // PROJ_ROWS)
        w_qkv0_b = _cast_group_weights(w_in_l, 0, rows=512)

        def group_weight_rider(group, row_tile_of):
            return _CastRider(w_in_l, (row_blocks, PROJ_COLS),
                              lambda b, t, j: (row_tile_of(b, t, j), j * sections + group),
                              (D_MODEL, 3 * GROUP_WIDTH), lambda b, t, j: (row_tile_of(b, t, j), j))

        def conv_weight_rider(row_tile_of):
            half = 3 * CONV_WIDTH // 2
            return _CastRider(w_in_l, (row_blocks, half),
                              lambda b, t, j: (row_tile_of(b, t, j), COL_U // half + jnp.minimum(j, 1)),
                              (D_MODEL, 3 * CONV_WIDTH), lambda b, t, j: (row_tile_of(b, t, j), jnp.minimum(j, 1)))

        def gate_weight_rider(tn):
            return _CastRider(w_in_l, (D_MODEL // batch, 2 * D_MODEL // (CONV_WIDTH // tn)),
                              lambda b, j: (b, COL_GA // (2 * D_MODEL // (CONV_WIDTH // tn)) + j),
                              (D_MODEL, 2 * D_MODEL), lambda b, j: (b, j))

        qkv0, hb, w_attn_o_b, w_conv_o_b, w_qkv1_b = _qkv_proj(
            h, w_qkv0_b, 0, batch, seq,
            lambda step_of, row_tile_of: [_row_block_rider(w_attn_o[layer], step_of),
                                          _row_block_rider(w_conv_o[layer], step_of),
                                          group_weight_rider(1, row_tile_of)])
        qkv1, w_gate_b, w_qkv2_b = _qkv_proj(
            hb, w_qkv1_b, 1, batch, seq,
            lambda step_of, row_tile_of: [_row_block_rider(w_ffn_gate[layer], step_of),
                                          group_weight_rider(2, row_tile_of)])
        qkv2, w_up_b, w_ucb_b = _qkv_proj(
            hb, w_qkv2_b, 2, batch, seq,
            lambda step_of, row_tile_of: [_row_block_rider(w_ffn_up[layer], step_of),
                                          conv_weight_rider(row_tile_of)])
        conv_tn = 512
        conv, w_down_b, w_out_b, w_gates_b = _conv_proj(
            hb, w_ucb_b, conv_w[layer], batch, seq,
            lambda step_of: [_row_block_rider(w_ffn_down[layer], step_of),
                             _row_block_rider(w_out[layer], step_of),
                             gate_weight_rider(conv_tn)],
            tn=conv_tn)
        attn = _attention([qkv0, qkv1, qkv2], batch, seq)
        merged = _gated_merge(attn, conv, hb, w_attn_o_b, w_conv_o_b, w_gates_b, tm=1024, tn=512)
        h, hb = _out_proj_ln(merged, w_out_b, h, ln1_g[layer][None, :], ln1_b[layer][None, :], tm=512)
        h = _ffn_ln(h, hb, w_gate_b, w_up_b, w_down_b, ln2_g[layer][None, :], ln2_b[layer][None, :],
                    tm=1024, tf=512)
    return h.reshape(batch, seq, d)
```

```python
import functools
import math
from typing import Callable, NamedTuple

import jax
import jax.numpy as jnp
from jax import lax
from jax.experimental import pallas as pl
from jax.experimental.pallas import tpu as pltpu

D_MODEL = 2048
HEAD_DIM = 128
HEADS_PER_GROUP = 8
DILATED_GROUPS = ((128, 1), (512, 4), (2048, 16))
N_GROUPS = len(DILATED_GROUPS)
GROUP_WIDTH = HEADS_PER_GROUP * HEAD_DIM
ATTN_WIDTH = N_GROUPS * GROUP_WIDTH
CONV_WIDTH = D_MODEL
CONV_K = 3
FFN_HIDDEN = 5632
DEPTH = 1
ALPHA = (2 * DEPTH) ** 0.25
LN_EPS = 1e-5

COL_U = 3 * ATTN_WIDTH
COL_C = COL_U + CONV_WIDTH
COL_B = COL_C + CONV_WIDTH
COL_GA = COL_B + CONV_WIDTH
COL_GC = COL_GA + D_MODEL
IN_COLS = COL_GC + D_MODEL

V7X_VMEM_BYTES = 64 * 1024 * 1024
BAND = 128
ATTN_TILES_IN_FLIGHT = 8
MAX_ROW_STRIDE = 4
CAST_RIDER_STEPS = 32
PROJ_ROWS = 1024
PROJ_COLS = 1024
OUT_LN_SPLIT = 4
FFN_LN_SPLIT = 4
MERGE_SPLIT = 2

F32 = jnp.float32
BF16 = jnp.bfloat16


def _vmem_limit(pipelined_block_bytes, resident_bytes):
    need = 2 * pipelined_block_bytes + resident_bytes
    return min(int(need * 1.25), V7X_VMEM_BYTES - 4 * 1024 * 1024)


def _nbytes(shape, dtype):
    return math.prod(shape) * jnp.dtype(dtype).itemsize


def _sigmoid(x):
    return 0.5 * (jnp.tanh(0.5 * x) + 1.0)


def _store_heads_by_residue(y, o_ref, y_ref, y2_ref, dilation):
    if dilation == 1:
        for head in range(HEADS_PER_GROUP):
            o_ref[head, 0] = y[:, head * HEAD_DIM:(head + 1) * HEAD_DIM].astype(o_ref.dtype)
        return
    f1 = min(dilation, MAX_ROW_STRIDE)
    f2 = dilation // f1
    assert f1 * f2 == dilation and f2 <= MAX_ROW_STRIDE
    rows = y_ref.shape[1]
    per_residue = rows // dilation
    for chunk in range(y.shape[0] // rows):
        dst = slice(chunk * per_residue, (chunk + 1) * per_residue)
        for head in range(HEADS_PER_GROUP):
            src = y_ref.at[head]
            src[...] = y[chunk * rows:(chunk + 1) * rows, head * HEAD_DIM:(head + 1) * HEAD_DIM]
            if f2 > 1:
                for r0 in range(f1):
                    y2_ref[head, r0 * (rows // f1):(r0 + 1) * (rows // f1), :] = src[pl.ds(r0, rows // f1, stride=f1), :]
                src = y2_ref.at[head]
            for r0 in range(f1):
                for r1 in range(f2):
                    start = r0 * (rows // f1) + r1 if f2 > 1 else r0
                    stride = f2 if f2 > 1 else f1
                    o_ref[head, r1 * f1 + r0, dst, :] = (
                        src[pl.ds(start, per_residue, stride=stride), :].astype(o_ref.dtype))


def _cast_riders(rider_refs, rider_out_refs):
    for src, dst in zip(rider_refs, rider_out_refs, strict=True):
        dst[...] = src[...].astype(dst.dtype)


def _qkv_proj_first_kernel(*refs, n_riders):
    x_ref, w_ref = refs[:2]
    rider_refs = refs[2:2 + n_riders]
    o_ref, xb_ref = refs[2 + n_riders:4 + n_riders]
    _cast_riders(rider_refs, refs[4 + n_riders:])

    @pl.when(pl.program_id(2) == 0)
    def _():
        xb_ref[...] = x_ref[...].astype(BF16)

    y = jnp.dot(xb_ref[...], w_ref[...], preferred_element_type=F32)
    _store_heads_by_residue(y, o_ref, None, None, 1)


def _qkv_proj_kernel(*refs, n_riders, dilation):
    xb_ref, w_ref = refs[:2]
    rider_refs = refs[2:2 + n_riders]
    o_ref = refs[2 + n_riders]
    rider_out_refs = refs[3 + n_riders:3 + 2 * n_riders]
    y_ref, y2_ref = refs[3 + 2 * n_riders:]
    _cast_riders(rider_refs, rider_out_refs)
    y = jnp.dot(xb_ref[...], w_ref[...], preferred_element_type=F32)
    _store_heads_by_residue(y, o_ref, y_ref, y2_ref, dilation)


class _CastRider(NamedTuple):
    array: jax.Array
    block: tuple
    in_index: Callable
    out_shape: tuple
    out_index: Callable


def _cast_kernel(w_ref, o_ref):
    o_ref[...] = w_ref[...].astype(o_ref.dtype)


def _cast_group_weights(w_in, group, *, rows):
    sections = ATTN_WIDTH // GROUP_WIDTH
    blocks = _nbytes((rows, GROUP_WIDTH), F32) + _nbytes((rows, GROUP_WIDTH), BF16)
    return pl.pallas_call(
        _cast_kernel,
        grid=(3, D_MODEL // rows),
        in_specs=[pl.BlockSpec((rows, GROUP_WIDTH), lambda j, i: (i, j * sections + group))],
        out_specs=pl.BlockSpec((rows, GROUP_WIDTH), lambda j, i: (i, j)),
        out_shape=jax.ShapeDtypeStruct((D_MODEL, 3 * GROUP_WIDTH), BF16),
        compiler_params=pltpu.CompilerParams(
            dimension_semantics=("parallel", "parallel"),
            vmem_limit_bytes=_vmem_limit(blocks, 0)),
        name="cast_group_weights",
    )(w_in)


def _row_block_rider(w, step_of):
    rows = w.shape[0] // CAST_RIDER_STEPS
    assert rows * CAST_RIDER_STEPS == w.shape[0] and rows % 16 == 0

    def index(*idx):
        return jnp.minimum(step_of(*idx), CAST_RIDER_STEPS - 1), 0

    return _CastRider(w, (rows, w.shape[1]), index, w.shape, index)


def _cast_rider_specs(riders):
    in_specs = [pl.BlockSpec(r.block, r.in_index) for r in riders]
    out_specs = [pl.BlockSpec(r.block, r.out_index) for r in riders]
    shapes = [jax.ShapeDtypeStruct(r.out_shape, BF16) for r in riders]
    return in_specs, out_specs, shapes, sum(_nbytes(r.block, F32) * 3 // 2 for r in riders)


def _qkv_proj(x, w_qkv, group, batch, seq, make_riders):
    dilation = DILATED_GROUPS[group][1]
    first = x.dtype == F32
    assert first == (group == 0) and dilation == (1 if first else dilation)
    length = seq // dilation
    tiles = seq // PROJ_ROWS
    per_residue = PROJ_ROWS // dilation
    assert PROJ_COLS == GROUP_WIDTH and w_qkv.shape == (D_MODEL, 3 * GROUP_WIDTH)
    row_spec = pl.BlockSpec((PROJ_ROWS, D_MODEL), lambda b, t, j: (b * tiles + t, 0))
    qkv_spec = pl.BlockSpec((HEADS_PER_GROUP, None, None, dilation, per_residue, HEAD_DIM),
                            lambda b, t, j: (0, b, j, 0, t, 0))
    qkv_shape = jax.ShapeDtypeStruct((HEADS_PER_GROUP, batch, 3, dilation, length, HEAD_DIM), BF16)
    blocks = (_nbytes((PROJ_ROWS, D_MODEL), x.dtype) + _nbytes((D_MODEL, PROJ_COLS), BF16)
              + _nbytes((PROJ_ROWS, PROJ_COLS), BF16) + (_nbytes((PROJ_ROWS, D_MODEL), BF16) if first else 0))
    regroup = [] if first else [pltpu.VMEM((HEADS_PER_GROUP, PROJ_ROWS, HEAD_DIM), F32)] * 2
    w_spec = pl.BlockSpec((D_MODEL, PROJ_COLS), lambda b, t, j: (0, j))
    riders = make_riders(lambda b, t, j: (b * tiles + t) * 3 + j, lambda b, t, j: b * tiles + t)
    rider_in_specs, rider_specs, rider_shapes, rider_bytes = _cast_rider_specs(riders)
    own_out_specs, own_out_shapes = [qkv_spec], [qkv_shape]
    if first:
        own_out_specs.append(row_spec)
        own_out_shapes.append(jax.ShapeDtypeStruct(x.shape, BF16))
        body = functools.partial(_qkv_proj_first_kernel, n_riders=len(riders))
    else:
        body = functools.partial(_qkv_proj_kernel, n_riders=len(riders), dilation=dilation)
    out = pl.pallas_call(
        body,
        grid=(batch, tiles, 3),
        in_specs=[row_spec, w_spec] + rider_in_specs,
        out_specs=own_out_specs + rider_specs,
        out_shape=own_out_shapes + rider_shapes,
        scratch_shapes=regroup,
        compiler_params=pltpu.CompilerParams(
            dimension_semantics=("arbitrary", "arbitrary", "arbitrary"),
            vmem_limit_bytes=_vmem_limit(blocks + rider_bytes,
                                         (0 if first else 2 * _nbytes((PROJ_ROWS, PROJ_COLS), F32))
                                         + 2 * _nbytes((PROJ_ROWS, PROJ_COLS), F32))),
        name=f"qkv_proj_dilation{dilation}",
    )(x, w_qkv, *[r.array for r in riders])
    return (out[0].reshape(HEADS_PER_GROUP, batch, 3, seq, HEAD_DIM),) + tuple(out[1:])


def _conv_proj_kernel(*refs, n_riders):
    x_ref, wu_ref, wc_ref, wb_ref, cw_ref = refs[:5]
    o_ref = refs[5 + n_riders]
    _cast_riders(refs[5:5 + n_riders], refs[6 + n_riders:])
    x = x_ref[...]
    z = (jnp.dot(x, wc_ref[...], preferred_element_type=F32)
         * jnp.dot(x, wu_ref[...], preferred_element_type=F32))
    cw = cw_ref[...]
    row = lax.broadcasted_iota(jnp.int32, z.shape, 0)
    y = cw[0:1, :] * z
    for tap in range(1, CONV_K):
        shifted = jnp.where(row >= tap, pltpu.roll(z, tap, axis=0), 0.0)
        y = y + cw[tap:tap + 1, :] * shifted
    o_ref[...] = (jnp.dot(x, wb_ref[...], preferred_element_type=F32) * y).astype(o_ref.dtype)


def _conv_proj(xb, w_ucb, conv_w, batch, seq, make_riders, *, tn):
    def w_spec(section):
        return pl.BlockSpec((D_MODEL, tn), lambda b, j: (0, section * (CONV_WIDTH // tn) + j))

    col_blocks = CONV_WIDTH // tn
    assert w_ucb.shape == (D_MODEL, 3 * CONV_WIDTH)
    riders = make_riders(lambda b, j: b * col_blocks + j)
    rider_in_specs, rider_specs, rider_shapes, rider_bytes = _cast_rider_specs(riders)
    blocks = (_nbytes((seq, D_MODEL), BF16) + 3 * _nbytes((D_MODEL, tn), BF16)
              + _nbytes((CONV_K, tn), F32) + _nbytes((seq, tn), BF16) + rider_bytes)
    return pl.pallas_call(
        functools.partial(_conv_proj_kernel, n_riders=len(riders)),
        grid=(batch, col_blocks),
        in_specs=[pl.BlockSpec((seq, D_MODEL), lambda b, j: (b, 0)),
                  w_spec(0), w_spec(1), w_spec(2),
                  pl.BlockSpec((CONV_K, tn), lambda b, j: (0, j))] + rider_in_specs,
        out_specs=[pl.BlockSpec((seq, tn), lambda b, j: (b, j))] + rider_specs,
        out_shape=[jax.ShapeDtypeStruct((batch * seq, CONV_WIDTH), BF16)] + rider_shapes,
        compiler_params=pltpu.CompilerParams(
            dimension_semantics=("arbitrary", "arbitrary"),
            vmem_limit_bytes=_vmem_limit(blocks, 5 * _nbytes((seq, tn), F32))),
        name="conv_proj",
    )(xb, w_ucb, w_ucb, w_ucb, conv_w, *[r.array for r in riders])


def _attn_kernel(qkv1_ref, qkv2_ref, qkv3_ref, o_ref, o_nat_ref, lse_nat_ref, bias_ref):
    seq = qkv1_ref.shape[1]
    n_tiles = seq // BAND
    scale = HEAD_DIM ** -0.5
    qi = lax.broadcasted_iota(jnp.int32, (BAND, 2 * BAND), 0)
    kj = lax.broadcasted_iota(jnp.int32, (BAND, 2 * BAND), 1)
    visible = ((kj < BAND) & (qi <= kj)) | ((kj >= BAND) & (qi >= kj - BAND))
    bias_ref[...] = jnp.where(visible, 0.0, -jnp.inf)

    def probabilities(q_ref, k_ref, n, has_prev):
        q = q_ref[n * BAND:(n + 1) * BAND, :]
        keys = slice((n - 1) * BAND, (n + 1) * BAND) if has_prev else slice(n * BAND, (n + 1) * BAND)
        bias = bias_ref[...] if has_prev else bias_ref[:, BAND:]
        s = lax.dot_general(q, k_ref[keys, :], (((1,), (1,)), ((), ())),
                            preferred_element_type=F32) * scale + bias
        m = s.max(-1, keepdims=True)
        return jnp.exp(s - m).astype(BF16), m, keys

    assert DILATED_GROUPS[0][1] == 1 and all(d % MAX_ROW_STRIDE == 0 for _, d in DILATED_GROUPS[1:])
    quarter = seq // MAX_ROW_STRIDE

    def result_rows(g, n):
        dilation = DILATED_GROUPS[g][1]
        if dilation == 1:
            return pl.ds(n * BAND, BAND)
        residue, t = divmod(n, n_tiles // dilation)
        step = dilation // MAX_ROW_STRIDE
        start = (residue % MAX_ROW_STRIDE) * quarter + step * t * BAND + residue // MAX_ROW_STRIDE
        return pl.ds(start, BAND, stride=step)

    def finish(g, v_ref, n, p, m, keys):
        rows = result_rows(g, n)
        v = v_ref[keys, :]
        acc = jnp.dot(p, jnp.concatenate([v, jnp.ones_like(v)], axis=1), preferred_element_type=F32)
        denom = acc[:, HEAD_DIM:]
        o_nat_ref[g, rows, :] = acc[:, :HEAD_DIM] / denom
        lse_nat_ref[g, rows, :] = m + jnp.log(denom)

    groups = [(r.at[0], r.at[1], r.at[2]) for r in (qkv1_ref, qkv2_ref, qkv3_ref)]
    in_flight = []
    for g, (q_ref, k_ref, v_ref) in enumerate(groups):
        tiles_per_residue = n_tiles // DILATED_GROUPS[g][1]
        for n in range(n_tiles):
            has_prev = n % tiles_per_residue > 0
            in_flight.append((g, v_ref, n) + probabilities(q_ref, k_ref, n, has_prev))
            if len(in_flight) > ATTN_TILES_IN_FLIGHT:
                finish(*in_flight.pop(0))
    for pending in in_flight:
        finish(*pending)

    chunks_per_quarter = quarter // BAND

    def mix(c, _):
        mixed = pl.ds(pl.multiple_of(c * BAND, BAND), BAND)
        natural = pl.ds(MAX_ROW_STRIDE * (c % chunks_per_quarter) * BAND + c // chunks_per_quarter,
                        BAND, stride=MAX_ROW_STRIDE)
        rows = [natural] + [mixed] * (N_GROUPS - 1)
        lse = [lse_nat_ref[g, rows[g], :] for g in range(N_GROUPS)]
        top = jnp.maximum(jnp.maximum(lse[0], lse[1]), lse[2])
        w = [jnp.exp(x - top) for x in lse]
        num = (w[0] * o_nat_ref[0, rows[0], :] + w[1] * o_nat_ref[1, rows[1], :]
               + w[2] * o_nat_ref[2, rows[2], :])
        o_nat_ref[0, natural, :] = num / (w[0] + w[1] + w[2])
        return 0

    lax.fori_loop(0, n_tiles, mix, 0)
    o_ref[...] = o_nat_ref[0].astype(o_ref.dtype)


def _attention(qkv, batch, seq):
    heads = HEADS_PER_GROUP
    qkv_spec = pl.BlockSpec((None, None, 3, seq, HEAD_DIM), lambda b, h: (h, b, 0, 0, 0))
    scratch = 2 * _nbytes((N_GROUPS, seq, HEAD_DIM), F32)
    return pl.pallas_call(
        _attn_kernel,
        grid=(batch, heads),
        in_specs=[qkv_spec] * N_GROUPS,
        out_specs=pl.BlockSpec((seq, HEAD_DIM), lambda b, h: (b, h)),
        out_shape=jax.ShapeDtypeStruct((batch * seq, GROUP_WIDTH), BF16),
        scratch_shapes=[pltpu.VMEM((N_GROUPS, seq, HEAD_DIM), F32),
                        pltpu.VMEM((N_GROUPS, seq, HEAD_DIM), F32),
                        pltpu.VMEM((BAND, 2 * BAND), F32)],
        compiler_params=pltpu.CompilerParams(
            dimension_semantics=("parallel", "parallel"),
            vmem_limit_bytes=_vmem_limit(10 * _nbytes((seq, HEAD_DIM), BF16), scratch + 4 * 1024 * 1024)),
        name="dilated_attention",
    )(*qkv)


def _merge_kernel(attn_ref, conv_ref, x_ref, wao_ref, wco_ref, wga_ref, wgc_ref, o_ref):
    x, attn, conv = x_ref[...], attn_ref[...], conv_ref[...]
    slab = o_ref.shape[1] // MERGE_SPLIT
    for k in range(MERGE_SPLIT):
        cols = slice(k * slab, (k + 1) * slab)
        a = jnp.dot(attn, wao_ref[:, cols], preferred_element_type=F32)
        ga = _sigmoid(jnp.dot(x, wga_ref[:, cols], preferred_element_type=F32))
        c = jnp.dot(conv, wco_ref[:, cols], preferred_element_type=F32)
        gc = _sigmoid(jnp.dot(x, wgc_ref[:, cols], preferred_element_type=F32))
        o_ref[:, cols] = (ga * a + gc * c).astype(o_ref.dtype)


def _gated_merge(attn, conv, xb, w_attn_o, w_conv_o, w_gates, *, tm, tn):
    m = attn.shape[0]
    assert w_gates.shape == (D_MODEL, 2 * D_MODEL)
    blocks = (_nbytes((tm, GROUP_WIDTH), BF16) + _nbytes((tm, CONV_WIDTH), BF16)
              + _nbytes((tm, D_MODEL), BF16) + _nbytes((tm, tn), BF16)
              + _nbytes((GROUP_WIDTH + CONV_WIDTH + 2 * D_MODEL, tn), BF16))
    return pl.pallas_call(
        _merge_kernel,
        grid=(m // tm, D_MODEL // tn),
        in_specs=[pl.BlockSpec((tm, GROUP_WIDTH), lambda i, j: (i, 0)),
                  pl.BlockSpec((tm, CONV_WIDTH), lambda i, j: (i, 0)),
                  pl.BlockSpec((tm, D_MODEL), lambda i, j: (i, 0)),
                  pl.BlockSpec((GROUP_WIDTH, tn), lambda i, j: (0, j)),
                  pl.BlockSpec((CONV_WIDTH, tn), lambda i, j: (0, j)),
                  pl.BlockSpec((D_MODEL, tn), lambda i, j: (0, j)),
                  pl.BlockSpec((D_MODEL, tn), lambda i, j: (0, D_MODEL // tn + j))],
        out_specs=pl.BlockSpec((tm, tn), lambda i, j: (i, j)),
        out_shape=jax.ShapeDtypeStruct((m, D_MODEL), BF16),
        compiler_params=pltpu.CompilerParams(
            dimension_semantics=("parallel", "arbitrary"),
            vmem_limit_bytes=_vmem_limit(blocks, 5 * _nbytes((tm, tn), F32))),
        name="gated_merge",
    )(attn, conv, xb, w_attn_o, w_conv_o, w_gates, w_gates)


def _layer_norm(z, g, b):
    mu = jnp.mean(z, axis=-1, keepdims=True)
    zc = z - mu
    var = jnp.mean(zc * zc, axis=-1, keepdims=True)
    return zc * lax.rsqrt(var + LN_EPS) * g + b


def _out_ln_kernel(mi_ref, w_ref, x_ref, g_ref, b_ref, o_ref, ob_ref):
    part = mi_ref.shape[0] // OUT_LN_SPLIT
    for rows in (slice(k * part, (k + 1) * part) for k in range(OUT_LN_SPLIT)):
        y = jnp.dot(mi_ref[rows, :], w_ref[...], preferred_element_type=F32)
        h = _layer_norm(ALPHA * x_ref[rows, :] + y, g_ref[...], b_ref[...])
        o_ref[rows, :] = h
        ob_ref[rows, :] = h.astype(ob_ref.dtype)


def _out_proj_ln(merged, w_out, x, g, b, *, tm):
    m = merged.shape[0]
    blocks = (2 * _nbytes((tm, D_MODEL), BF16) + _nbytes((D_MODEL, D_MODEL), BF16)
              + 2 * _nbytes((tm, D_MODEL), F32))
    row = pl.BlockSpec((tm, D_MODEL), lambda i: (i, 0))
    vec = pl.BlockSpec((1, D_MODEL), lambda i: (0, 0))
    return pl.pallas_call(
        _out_ln_kernel,
        grid=(m // tm,),
        in_specs=[row, pl.BlockSpec((D_MODEL, D_MODEL), lambda i: (0, 0)), row, vec, vec],
        out_specs=[row, row],
        out_shape=[jax.ShapeDtypeStruct((m, D_MODEL), F32), jax.ShapeDtypeStruct((m, D_MODEL), BF16)],
        compiler_params=pltpu.CompilerParams(
            dimension_semantics=("parallel",),
            vmem_limit_bytes=_vmem_limit(blocks, 3 * _nbytes((tm, D_MODEL), F32))),
        name="out_proj_ln",
    )(merged, w_out, x, g, b)


def _ffn_kernel(hb_ref, h_hbm_ref, wg_ref, wu_ref, wd_ref, g_ref, b_ref, o_ref, res_ref, res_sem):
    i, f, last = pl.program_id(0), pl.program_id(1), pl.num_programs(1) - 1
    tm = o_ref.shape[0]
    residual_copy = pltpu.make_async_copy(
        h_hbm_ref.at[pl.ds(pl.multiple_of(i * tm, tm), tm), :], res_ref, res_sem)

    def hidden_block_term(rows):
        hb = hb_ref[rows, :]
        gate = jnp.dot(hb, wg_ref[...], preferred_element_type=F32)
        up = jnp.dot(hb, wu_ref[...], preferred_element_type=F32)
        hidden = (gate * _sigmoid(gate) * up).astype(BF16)
        return jnp.dot(hidden, wd_ref[...], preferred_element_type=F32)

    @pl.when(f == 0)
    def _():
        residual_copy.start()
        o_ref[...] = hidden_block_term(slice(None))

    @pl.when((f > 0) & (f < last))
    def _():
        o_ref[...] += hidden_block_term(slice(None))

    @pl.when(f == last)
    def _():
        residual_copy.wait()
        part = tm // FFN_LN_SPLIT
        for rows in (slice(k * part, (k + 1) * part) for k in range(FFN_LN_SPLIT)):
            acc = o_ref[rows, :] + hidden_block_term(rows)
            o_ref[rows, :] = _layer_norm(ALPHA * res_ref[rows, :] + acc, g_ref[...], b_ref[...])


def _ffn_ln(h, hb, w_gate, w_up, w_down, g, b, *, tm, tf):
    m = h.shape[0]
    blocks = (_nbytes((tm, D_MODEL), BF16) + _nbytes((tm, D_MODEL), F32)
              + 2 * _nbytes((D_MODEL, tf), BF16) + _nbytes((tf, D_MODEL), BF16))
    scratch = _nbytes((tm, D_MODEL), F32)
    row = pl.BlockSpec((tm, D_MODEL), lambda i, f: (i, 0))
    vec = pl.BlockSpec((1, D_MODEL), lambda i, f: (0, 0))
    return pl.pallas_call(
        _ffn_kernel,
        grid=(m // tm, FFN_HIDDEN // tf),
        in_specs=[row,
                  pl.BlockSpec(memory_space=pl.ANY),
                  pl.BlockSpec((D_MODEL, tf), lambda i, f: (0, f)),
                  pl.BlockSpec((D_MODEL, tf), lambda i, f: (0, f)),
                  pl.BlockSpec((tf, D_MODEL), lambda i, f: (f, 0)),
                  vec, vec],
        out_specs=row,
        out_shape=jax.ShapeDtypeStruct((m, D_MODEL), F32),
        scratch_shapes=[pltpu.VMEM((tm, D_MODEL), F32), pltpu.SemaphoreType.DMA],
        compiler_params=pltpu.CompilerParams(
            dimension_semantics=("arbitrary", "arbitrary"),
            vmem_limit_bytes=_vmem_limit(blocks, scratch + 6 * _nbytes((tm, tf), F32))),
        name="ffn_ln",
    )(hb, h, w_gate, w_up, w_down, g, b)


def kernel(x, w_in, conv_w, w_attn_o, w_conv_o, w_out, ln1_g, ln1_b,
           w_ffn_gate, w_ffn_up, w_ffn_down, ln2_g, ln2_b):
    batch, seq, d = x.shape
    assert d == D_MODEL and w_in.shape == (DEPTH, D_MODEL, IN_COLS)
    assert seq % PROJ_ROWS == 0 and all(PROJ_ROWS % (dil * 16) == 0 for _, dil in DILATED_GROUPS)
    assert all(seq % (dil * BAND) == 0 for _, dil in DILATED_GROUPS)
    h = x.reshape(batch * seq, d)
    for layer in range(DEPTH):
        assert N_GROUPS == 3
        w_in_l = w_in[layer]
        sections = ATTN_WIDTH // PROJ_COLS
        row_blocks = D_MODEL // (batch * seq // PROJ_ROWS)
        w_qkv0_b = _cast_group_weights(w_in_l, 0, rows=512)

        def group_weight_rider(group, row_tile_of):
            return _CastRider(w_in_l, (row_blocks, PROJ_COLS),
                              lambda b, t, j: (row_tile_of(b, t, j), j * sections + group),
                              (D_MODEL, 3 * GROUP_WIDTH), lambda b, t, j: (row_tile_of(b, t, j), j))

        def conv_weight_rider(row_tile_of):
            half = 3 * CONV_WIDTH // 2
            return _CastRider(w_in_l, (row_blocks, half),
                              lambda b, t, j: (row_tile_of(b, t, j), COL_U // half + jnp.minimum(j, 1)),
                              (D_MODEL, 3 * CONV_WIDTH), lambda b, t, j: (row_tile_of(b, t, j), jnp.minimum(j, 1)))

        def gate_weight_rider(tn):
            return _CastRider(w_in_l, (D_MODEL // batch, 2 * D_MODEL // (CONV_WIDTH // tn)),
                              lambda b, j: (b, COL_GA // (2 * D_MODEL // (CONV_WIDTH // tn)) + j),
                              (D_MODEL, 2 * D_MODEL), lambda b, j: (b, j))

        qkv0, hb, w_attn_o_b, w_conv_o_b, w_qkv1_b = _qkv_proj(
            h, w_qkv0_b, 0, batch, seq,
            lambda step_of, row_tile_of: [_row_block_rider(w_attn_o[layer], step_of),
                                          _row_block_rider(w_conv_o[layer], step_of),
                                          group_weight_rider(1, row_tile_of)])
        qkv1, w_gate_b, w_qkv2_b = _qkv_proj(
            hb, w_qkv1_b, 1, batch, seq,
            lambda step_of, row_tile_of: [_row_block_rider(w_ffn_gate[layer], step_of),
                                          group_weight_rider(2, row_tile_of)])
        qkv2, w_up_b, w_ucb_b = _qkv_proj(
            hb, w_qkv2_b, 2, batch, seq,
            lambda step_of, row_tile_of: [_row_block_rider(w_ffn_up[layer], step_of),
                                          conv_weight_rider(row_tile_of)])
        conv_tn = 512
        conv, w_down_b, w_out_b, w_gates_b = _conv_proj(
            hb, w_ucb_b, conv_w[layer], batch, seq,
            lambda step_of: [_row_block_rider(w_ffn_down[layer], step_of),
                             _row_block_rider(w_out[layer], step_of),
                             gate_weight_rider(conv_tn)],
            tn=conv_tn)
        attn = _attention([qkv0, qkv1, qkv2], batch, seq)
        merged = _gated_merge(attn, conv, hb, w_attn_o_b, w_conv_o_b, w_gates_b, tm=1024, tn=512)
        h, hb = _out_proj_ln(merged, w_out_b, h, ln1_g[layer][None, :], ln1_b[layer][None, :], tm=512)
        h = _ffn_ln(h, hb, w_gate_b, w_up_b, w_down_b, ln2_g[layer][None, :], ln2_b[layer][None, :],
                    tm=1024, tf=512)
    return h.reshape(batch, seq, d)
```

```python
import functools
import math
from typing import Callable, NamedTuple

import jax
import jax.numpy as jnp
from jax import lax
from jax.experimental import pallas as pl
from jax.experimental.pallas import tpu as pltpu

D_MODEL = 2048
HEAD_DIM = 128
HEADS_PER_GROUP = 8
DILATED_GROUPS = ((128, 1), (512, 4), (2048, 16))
N_GROUPS = len(DILATED_GROUPS)
GROUP_WIDTH = HEADS_PER_GROUP * HEAD_DIM
ATTN_WIDTH = N_GROUPS * GROUP_WIDTH
CONV_WIDTH = D_MODEL
CONV_K = 3
FFN_HIDDEN = 5632
DEPTH = 1
ALPHA = (2 * DEPTH) ** 0.25
LN_EPS = 1e-5

COL_U = 3 * ATTN_WIDTH
COL_C = COL_U + CONV_WIDTH
COL_B = COL_C + CONV_WIDTH
COL_GA = COL_B + CONV_WIDTH
COL_GC = COL_GA + D_MODEL
IN_COLS = COL_GC + D_MODEL

V7X_VMEM_BYTES = 64 * 1024 * 1024
BAND = 128
ATTN_TILES_IN_FLIGHT = 4
MAX_ROW_STRIDE = 4
CAST_RIDER_STEPS = 32
PROJ_ROWS = 1024
PROJ_COLS = 1024
OUT_LN_SPLIT = 4
FFN_LN_SPLIT = 4
MERGE_SPLIT = 2

F32 = jnp.float32
BF16 = jnp.bfloat16


def _vmem_limit(pipelined_block_bytes, resident_bytes):
    need = 2 * pipelined_block_bytes + resident_bytes
    return min(int(need * 1.25), V7X_VMEM_BYTES - 4 * 1024 * 1024)


def _nbytes(shape, dtype):
    return math.prod(shape) * jnp.dtype(dtype).itemsize


def _sigmoid(x):
    return 0.5 * (jnp.tanh(0.5 * x) + 1.0)


def _store_heads_by_residue(y, o_ref, y_ref, y2_ref, dilation):
    if dilation == 1:
        for head in range(HEADS_PER_GROUP):
            o_ref[head, 0] = y[:, head * HEAD_DIM:(head + 1) * HEAD_DIM].astype(o_ref.dtype)
        return
    f1 = min(dilation, MAX_ROW_STRIDE)
    f2 = dilation // f1
    assert f1 * f2 == dilation and f2 <= MAX_ROW_STRIDE
    rows = y_ref.shape[1]
    per_residue = rows // dilation
    for chunk in range(y.shape[0] // rows):
        dst = slice(chunk * per_residue, (chunk + 1) * per_residue)
        for head in range(HEADS_PER_GROUP):
            src = y_ref.at[head]
            src[...] = y[chunk * rows:(chunk + 1) * rows, head * HEAD_DIM:(head + 1) * HEAD_DIM]
            if f2 > 1:
                for r0 in range(f1):
                    y2_ref[head, r0 * (rows // f1):(r0 + 1) * (rows // f1), :] = src[pl.ds(r0, rows // f1, stride=f1), :]
                src = y2_ref.at[head]
            for r0 in range(f1):
                for r1 in range(f2):
                    start = r0 * (rows // f1) + r1 if f2 > 1 else r0
                    stride = f2 if f2 > 1 else f1
                    o_ref[head, r1 * f1 + r0, dst, :] = (
                        src[pl.ds(start, per_residue, stride=stride), :].astype(o_ref.dtype))


def _cast_riders(rider_refs, rider_out_refs):
    for src, dst in zip(rider_refs, rider_out_refs, strict=True):
        dst[...] = src[...].astype(dst.dtype)


def _qkv_proj_first_kernel(*refs, n_riders):
    x_ref, w_ref = refs[:2]
    rider_refs = refs[2:2 + n_riders]
    o_ref, xb_ref = refs[2 + n_riders:4 + n_riders]
    _cast_riders(rider_refs, refs[4 + n_riders:])

    @pl.when(pl.program_id(2) == 0)
    def _():
        xb_ref[...] = x_ref[...].astype(BF16)

    y = jnp.dot(xb_ref[...], w_ref[...], preferred_element_type=F32)
    _store_heads_by_residue(y, o_ref, None, None, 1)


def _qkv_proj_kernel(*refs, n_riders, dilation):
    xb_ref, w_ref = refs[:2]
    rider_refs = refs[2:2 + n_riders]
    o_ref = refs[2 + n_riders]
    rider_out_refs = refs[3 + n_riders:3 + 2 * n_riders]
    y_ref, y2_ref = refs[3 + 2 * n_riders:]
    _cast_riders(rider_refs, rider_out_refs)
    y = jnp.dot(xb_ref[...], w_ref[...], preferred_element_type=F32)
    _store_heads_by_residue(y, o_ref, y_ref, y2_ref, dilation)


class _CastRider(NamedTuple):
    array: jax.Array
    block: tuple
    in_index: Callable
    out_shape: tuple
    out_index: Callable


def _cast_kernel(w_ref, o_ref):
    o_ref[...] = w_ref[...].astype(o_ref.dtype)


def _cast_group_weights(w_in, group, *, rows):
    sections = ATTN_WIDTH // GROUP_WIDTH
    blocks = _nbytes((rows, GROUP_WIDTH), F32) + _nbytes((rows, GROUP_WIDTH), BF16)
    return pl.pallas_call(
        _cast_kernel,
        grid=(3, D_MODEL // rows),
        in_specs=[pl.BlockSpec((rows, GROUP_WIDTH), lambda j, i: (i, j * sections + group))],
        out_specs=pl.BlockSpec((rows, GROUP_WIDTH), lambda j, i: (i, j)),
        out_shape=jax.ShapeDtypeStruct((D_MODEL, 3 * GROUP_WIDTH), BF16),
        compiler_params=pltpu.CompilerParams(
            dimension_semantics=("parallel", "parallel"),
            vmem_limit_bytes=_vmem_limit(blocks, 0)),
        name="cast_group_weights",
    )(w_in)


def _row_block_rider(w, step_of):
    rows = w.shape[0] // CAST_RIDER_STEPS
    assert rows * CAST_RIDER_STEPS == w.shape[0] and rows % 16 == 0

    def index(*idx):
        return jnp.minimum(step_of(*idx), CAST_RIDER_STEPS - 1), 0

    return _CastRider(w, (rows, w.shape[1]), index, w.shape, index)


def _cast_rider_specs(riders):
    in_specs = [pl.BlockSpec(r.block, r.in_index) for r in riders]
    out_specs = [pl.BlockSpec(r.block, r.out_index) for r in riders]
    shapes = [jax.ShapeDtypeStruct(r.out_shape, BF16) for r in riders]
    return in_specs, out_specs, shapes, sum(_nbytes(r.block, F32) * 3 // 2 for r in riders)


def _qkv_proj(x, w_qkv, group, batch, seq, make_riders):
    dilation = DILATED_GROUPS[group][1]
    first = x.dtype == F32
    assert first == (group == 0) and dilation == (1 if first else dilation)
    length = seq // dilation
    rows = PROJ_ROWS if first else min(seq, 2 * PROJ_ROWS)
    tiles = seq // rows
    per_residue = rows // dilation
    assert PROJ_COLS == GROUP_WIDTH and w_qkv.shape == (D_MODEL, 3 * GROUP_WIDTH) and rows * tiles == seq
    row_spec = pl.BlockSpec((rows, D_MODEL), lambda b, t, j: (b * tiles + t, 0))
    qkv_spec = pl.BlockSpec((HEADS_PER_GROUP, None, None, dilation, per_residue, HEAD_DIM),
                            lambda b, t, j: (0, b, j, 0, t, 0))
    qkv_shape = jax.ShapeDtypeStruct((HEADS_PER_GROUP, batch, 3, dilation, length, HEAD_DIM), BF16)
    blocks = (_nbytes((rows, D_MODEL), x.dtype) + _nbytes((D_MODEL, PROJ_COLS), BF16)
              + _nbytes((rows, PROJ_COLS), BF16) + (_nbytes((rows, D_MODEL), BF16) if first else 0))
    regroup = [] if first else [pltpu.VMEM((HEADS_PER_GROUP, PROJ_ROWS, HEAD_DIM), F32)] * 2
    w_spec = pl.BlockSpec((D_MODEL, PROJ_COLS), lambda b, t, j: (0, j))
    riders = make_riders(lambda b, t, j: (b * tiles + t) * 3 + j, lambda b, t, j: b * tiles + t,
                         batch * tiles)
    rider_in_specs, rider_specs, rider_shapes, rider_bytes = _cast_rider_specs(riders)
    own_out_specs, own_out_shapes = [qkv_spec], [qkv_shape]
    if first:
        own_out_specs.append(row_spec)
        own_out_shapes.append(jax.ShapeDtypeStruct(x.shape, BF16))
        body = functools.partial(_qkv_proj_first_kernel, n_riders=len(riders))
    else:
        body = functools.partial(_qkv_proj_kernel, n_riders=len(riders), dilation=dilation)
    out = pl.pallas_call(
        body,
        grid=(batch, tiles, 3),
        in_specs=[row_spec, w_spec] + rider_in_specs,
        out_specs=own_out_specs + rider_specs,
        out_shape=own_out_shapes + rider_shapes,
        scratch_shapes=regroup,
        compiler_params=pltpu.CompilerParams(
            dimension_semantics=("arbitrary", "arbitrary", "arbitrary"),
            vmem_limit_bytes=_vmem_limit(blocks + rider_bytes,
                                         (0 if first else 2 * _nbytes((PROJ_ROWS, PROJ_COLS), F32))
                                         + 2 * _nbytes((rows, PROJ_COLS), F32))),
        name=f"qkv_proj_dilation{dilation}",
    )(x, w_qkv, *[r.array for r in riders])
    return (out[0].reshape(HEADS_PER_GROUP, batch, 3, seq, HEAD_DIM),) + tuple(out[1:])


def _conv_proj_kernel(*refs, n_riders):
    x_ref, wu_ref, wc_ref, wb_ref, cw_ref = refs[:5]
    o_ref = refs[5 + n_riders]
    _cast_riders(refs[5:5 + n_riders], refs[6 + n_riders:])
    x = x_ref[...]
    z = (jnp.dot(x, wc_ref[...], preferred_element_type=F32)
         * jnp.dot(x, wu_ref[...], preferred_element_type=F32))
    cw = cw_ref[...]
    row = lax.broadcasted_iota(jnp.int32, z.shape, 0)
    y = cw[0:1, :] * z
    for tap in range(1, CONV_K):
        shifted = jnp.where(row >= tap, pltpu.roll(z, tap, axis=0), 0.0)
        y = y + cw[tap:tap + 1, :] * shifted
    o_ref[...] = (jnp.dot(x, wb_ref[...], preferred_element_type=F32) * y).astype(o_ref.dtype)


def _conv_proj(xb, w_ucb, conv_w, batch, seq, make_riders, *, tn):
    def w_spec(section):
        return pl.BlockSpec((D_MODEL, tn), lambda b, j: (0, section * (CONV_WIDTH // tn) + j))

    col_blocks = CONV_WIDTH // tn
    assert w_ucb.shape == (D_MODEL, 3 * CONV_WIDTH)
    riders = make_riders(lambda b, j: b * col_blocks + j)
    rider_in_specs, rider_specs, rider_shapes, rider_bytes = _cast_rider_specs(riders)
    blocks = (_nbytes((seq, D_MODEL), BF16) + 3 * _nbytes((D_MODEL, tn), BF16)
              + _nbytes((CONV_K, tn), F32) + _nbytes((seq, tn), BF16) + rider_bytes)
    return pl.pallas_call(
        functools.partial(_conv_proj_kernel, n_riders=len(riders)),
        grid=(batch, col_blocks),
        in_specs=[pl.BlockSpec((seq, D_MODEL), lambda b, j: (b, 0)),
                  w_spec(0), w_spec(1), w_spec(2),
                  pl.BlockSpec((CONV_K, tn), lambda b, j: (0, j))] + rider_in_specs,
        out_specs=[pl.BlockSpec((seq, tn), lambda b, j: (b, j))] + rider_specs,
        out_shape=[jax.ShapeDtypeStruct((batch * seq, CONV_WIDTH), BF16)] + rider_shapes,
        compiler_params=pltpu.CompilerParams(
            dimension_semantics=("arbitrary", "arbitrary"),
            vmem_limit_bytes=_vmem_limit(blocks, 5 * _nbytes((seq, tn), F32))),
        name="conv_proj",
    )(xb, w_ucb, w_ucb, w_ucb, conv_w, *[r.array for r in riders])


def _attn_kernel(qkv1_ref, qkv2_ref, qkv3_ref, o_ref, num_ref, den_ref, max_ref, bias_ref):
    seq = qkv1_ref.shape[1]
    n_tiles = seq // BAND
    scale = HEAD_DIM ** -0.5
    qi = lax.broadcasted_iota(jnp.int32, (BAND, 2 * BAND), 0)
    kj = lax.broadcasted_iota(jnp.int32, (BAND, 2 * BAND), 1)
    visible = ((kj < BAND) & (qi <= kj)) | ((kj >= BAND) & (qi >= kj - BAND))
    bias_ref[...] = jnp.where(visible, 0.0, -jnp.inf)

    def probabilities(q_ref, k_ref, n, has_prev):
        q = q_ref[n * BAND:(n + 1) * BAND, :]
        keys = slice((n - 1) * BAND, (n + 1) * BAND) if has_prev else slice(n * BAND, (n + 1) * BAND)
        bias = bias_ref[...] if has_prev else bias_ref[:, BAND:]
        s = lax.dot_general(q, k_ref[keys, :], (((1,), (1,)), ((), ())),
                            preferred_element_type=F32) * scale + bias
        m = s.max(-1, keepdims=True)
        return jnp.exp(s - m).astype(BF16), m, keys

    assert DILATED_GROUPS[0][1] == 1 and all(d % MAX_ROW_STRIDE == 0 for _, d in DILATED_GROUPS[1:])
    quarter = seq // MAX_ROW_STRIDE

    def result_rows(g, n):
        dilation = DILATED_GROUPS[g][1]
        if dilation == 1:
            return pl.ds(n * BAND, BAND)
        residue, t = divmod(n, n_tiles // dilation)
        step = dilation // MAX_ROW_STRIDE
        start = (residue % MAX_ROW_STRIDE) * quarter + step * t * BAND + residue // MAX_ROW_STRIDE
        return pl.ds(start, BAND, stride=step)

    def finish(g, v_ref, n, p, m, keys):
        rows = result_rows(g, n)
        v = v_ref[keys, :]
        acc = jnp.dot(p, jnp.concatenate([v, jnp.ones_like(v)], axis=1), preferred_element_type=F32)
        num_ref[g, rows, :] = acc[:, :HEAD_DIM]
        den_ref[g, rows, :] = acc[:, HEAD_DIM:]
        max_ref[g, rows, :] = jnp.broadcast_to(m, (BAND, HEAD_DIM))

    groups = [(r.at[0], r.at[1], r.at[2]) for r in (qkv1_ref, qkv2_ref, qkv3_ref)]
    in_flight = []
    for g, (q_ref, k_ref, v_ref) in enumerate(groups):
        tiles_per_residue = n_tiles // DILATED_GROUPS[g][1]
        for n in range(n_tiles):
            has_prev = n % tiles_per_residue > 0
            in_flight.append((g, v_ref, n) + probabilities(q_ref, k_ref, n, has_prev))
            if len(in_flight) > ATTN_TILES_IN_FLIGHT:
                finish(*in_flight.pop(0))
    for pending in in_flight:
        finish(*pending)

    chunks_per_quarter = quarter // BAND

    def mix(c, _):
        mixed = pl.ds(pl.multiple_of(c * BAND, BAND), BAND)
        natural = pl.ds(MAX_ROW_STRIDE * (c % chunks_per_quarter) * BAND + c // chunks_per_quarter,
                        BAND, stride=MAX_ROW_STRIDE)
        rows = [natural] + [mixed] * (N_GROUPS - 1)
        m = [max_ref[g, rows[g], :] for g in range(N_GROUPS)]
        top = jnp.maximum(jnp.maximum(m[0], m[1]), m[2])
        w = [jnp.exp(x - top) for x in m]
        num = sum(w[g] * num_ref[g, rows[g], :] for g in range(N_GROUPS))
        den = sum(w[g] * den_ref[g, rows[g], :] for g in range(N_GROUPS))
        max_ref[0, natural, :] = num / den
        return 0

    lax.fori_loop(0, n_tiles, mix, 0)
    o_ref[...] = max_ref[0].astype(o_ref.dtype)


def _attention(qkv, batch, seq):
    heads = HEADS_PER_GROUP
    qkv_spec = pl.BlockSpec((None, None, 3, seq, HEAD_DIM), lambda b, h: (h, b, 0, 0, 0))
    scratch = 3 * _nbytes((N_GROUPS, seq, HEAD_DIM), F32)
    return pl.pallas_call(
        _attn_kernel,
        grid=(batch, heads),
        in_specs=[qkv_spec] * N_GROUPS,
        out_specs=pl.BlockSpec((seq, HEAD_DIM), lambda b, h: (b, h)),
        out_shape=jax.ShapeDtypeStruct((batch * seq, GROUP_WIDTH), BF16),
        scratch_shapes=[pltpu.VMEM((N_GROUPS, seq, HEAD_DIM), F32),
                        pltpu.VMEM((N_GROUPS, seq, HEAD_DIM), F32),
                        pltpu.VMEM((N_GROUPS, seq, HEAD_DIM), F32),
                        pltpu.VMEM((BAND, 2 * BAND), F32)],
        compiler_params=pltpu.CompilerParams(
            dimension_semantics=("parallel", "parallel"),
            vmem_limit_bytes=_vmem_limit(10 * _nbytes((seq, HEAD_DIM), BF16), scratch + 4 * 1024 * 1024)),
        name="dilated_attention",
    )(*qkv)


def _merge_kernel(*refs, n_riders):
    attn_ref, conv_ref, x_ref, wao_ref, wco_ref, wga_ref, wgc_ref = refs[:7]
    o_ref = refs[7 + n_riders]
    _cast_riders(refs[7:7 + n_riders], refs[8 + n_riders:])
    x, attn, conv = x_ref[...], attn_ref[...], conv_ref[...]
    slab = o_ref.shape[1] // MERGE_SPLIT
    for k in range(MERGE_SPLIT):
        cols = slice(k * slab, (k + 1) * slab)
        a = jnp.dot(attn, wao_ref[:, cols], preferred_element_type=F32)
        ga = _sigmoid(jnp.dot(x, wga_ref[:, cols], preferred_element_type=F32))
        c = jnp.dot(conv, wco_ref[:, cols], preferred_element_type=F32)
        gc = _sigmoid(jnp.dot(x, wgc_ref[:, cols], preferred_element_type=F32))
        o_ref[:, cols] = (ga * a + gc * c).astype(o_ref.dtype)


def _gated_merge(attn, conv, xb, w_attn_o, w_conv_o, w_gates, make_riders, *, tm, tn):
    m = attn.shape[0]
    assert w_gates.shape == (D_MODEL, 2 * D_MODEL)
    col_blocks = D_MODEL // tn
    riders = make_riders(lambda i, j: i * col_blocks + j)
    rider_in_specs, rider_specs, rider_shapes, rider_bytes = _cast_rider_specs(riders)
    blocks = (_nbytes((tm, GROUP_WIDTH), BF16) + _nbytes((tm, CONV_WIDTH), BF16)
              + _nbytes((tm, D_MODEL), BF16) + _nbytes((tm, tn), BF16)
              + _nbytes((GROUP_WIDTH + CONV_WIDTH + 2 * D_MODEL, tn), BF16) + rider_bytes)
    return pl.pallas_call(
        functools.partial(_merge_kernel, n_riders=len(riders)),
        grid=(m // tm, col_blocks),
        in_specs=[pl.BlockSpec((tm, GROUP_WIDTH), lambda i, j: (i, 0)),
                  pl.BlockSpec((tm, CONV_WIDTH), lambda i, j: (i, 0)),
                  pl.BlockSpec((tm, D_MODEL), lambda i, j: (i, 0)),
                  pl.BlockSpec((GROUP_WIDTH, tn), lambda i, j: (0, j)),
                  pl.BlockSpec((CONV_WIDTH, tn), lambda i, j: (0, j)),
                  pl.BlockSpec((D_MODEL, tn), lambda i, j: (0, j)),
                  pl.BlockSpec((D_MODEL, tn), lambda i, j: (0, col_blocks + j))] + rider_in_specs,
        out_specs=[pl.BlockSpec((tm, tn), lambda i, j: (i, j))] + rider_specs,
        out_shape=[jax.ShapeDtypeStruct((m, D_MODEL), BF16)] + rider_shapes,
        compiler_params=pltpu.CompilerParams(
            dimension_semantics=("arbitrary", "arbitrary"),
            vmem_limit_bytes=_vmem_limit(blocks, 5 * _nbytes((tm, tn), F32))),
        name="gated_merge",
    )(attn, conv, xb, w_attn_o, w_conv_o, w_gates, w_gates, *[r.array for r in riders])


def _layer_norm(z, g, b):
    mu = jnp.mean(z, axis=-1, keepdims=True)
    zc = z - mu
    var = jnp.mean(zc * zc, axis=-1, keepdims=True)
    return zc * lax.rsqrt(var + LN_EPS) * g + b


def _out_ln_kernel(mi_ref, w_ref, x_ref, g_ref, b_ref, o_ref, ob_ref):
    part = mi_ref.shape[0] // OUT_LN_SPLIT
    for rows in (slice(k * part, (k + 1) * part) for k in range(OUT_LN_SPLIT)):
        y = jnp.dot(mi_ref[rows, :], w_ref[...], preferred_element_type=F32)
        h = _layer_norm(ALPHA * x_ref[rows, :] + y, g_ref[...], b_ref[...])
        o_ref[rows, :] = h
        ob_ref[rows, :] = h.astype(ob_ref.dtype)


def _out_proj_ln(merged, w_out, x, g, b, *, tm):
    m = merged.shape[0]
    blocks = (2 * _nbytes((tm, D_MODEL), BF16) + _nbytes((D_MODEL, D_MODEL), BF16)
              + 2 * _nbytes((tm, D_MODEL), F32))
    row = pl.BlockSpec((tm, D_MODEL), lambda i: (i, 0))
    vec = pl.BlockSpec((1, D_MODEL), lambda i: (0, 0))
    return pl.pallas_call(
        _out_ln_kernel,
        grid=(m // tm,),
        in_specs=[row, pl.BlockSpec((D_MODEL, D_MODEL), lambda i: (0, 0)), row, vec, vec],
        out_specs=[row, row],
        out_shape=[jax.ShapeDtypeStruct((m, D_MODEL), F32), jax.ShapeDtypeStruct((m, D_MODEL), BF16)],
        compiler_params=pltpu.CompilerParams(
            dimension_semantics=("parallel",),
            vmem_limit_bytes=_vmem_limit(blocks, 3 * _nbytes((tm, D_MODEL), F32))),
        name="out_proj_ln",
    )(merged, w_out, x, g, b)


def _ffn_kernel(hb_ref, h_hbm_ref, wg_ref, wu_ref, wd_ref, g_ref, b_ref, o_ref, res_ref, res_sem):
    i, f, last = pl.program_id(0), pl.program_id(1), pl.num_programs(1) - 1
    tm = o_ref.shape[0]
    residual_copy = pltpu.make_async_copy(
        h_hbm_ref.at[pl.ds(pl.multiple_of(i * tm, tm), tm), :], res_ref, res_sem)

    def hidden_block_term(rows):
        hb = hb_ref[rows, :]
        gate = jnp.dot(hb, wg_ref[...], preferred_element_type=F32)
        up = jnp.dot(hb, wu_ref[...], preferred_element_type=F32)
        hidden = (gate * _sigmoid(gate) * up).astype(BF16)
        return jnp.dot(hidden, wd_ref[...], preferred_element_type=F32)

    @pl.when(f == 0)
    def _():
        residual_copy.start()
        o_ref[...] = hidden_block_term(slice(None))

    @pl.when((f > 0) & (f < last))
    def _():
        o_ref[...] += hidden_block_term(slice(None))

    @pl.when(f == last)
    def _():
        residual_copy.wait()
        part = tm // FFN_LN_SPLIT
        for rows in (slice(k * part, (k + 1) * part) for k in range(FFN_LN_SPLIT)):
            acc = o_ref[rows, :] + hidden_block_term(rows)
            o_ref[rows, :] = _layer_norm(ALPHA * res_ref[rows, :] + acc, g_ref[...], b_ref[...])


def _ffn_ln(h, hb, w_gate, w_up, w_down, g, b, *, tm, tf):
    m = h.shape[0]
    blocks = (_nbytes((tm, D_MODEL), BF16) + _nbytes((tm, D_MODEL), F32)
              + 2 * _nbytes((D_MODEL, tf), BF16) + _nbytes((tf, D_MODEL), BF16))
    scratch = _nbytes((tm, D_MODEL), F32)
    row = pl.BlockSpec((tm, D_MODEL), lambda i, f: (i, 0))
    vec = pl.BlockSpec((1, D_MODEL), lambda i, f: (0, 0))
    return pl.pallas_call(
        _ffn_kernel,
        grid=(m // tm, FFN_HIDDEN // tf),
        in_specs=[row,
                  pl.BlockSpec(memory_space=pl.ANY),
                  pl.BlockSpec((D_MODEL, tf), lambda i, f: (0, f)),
                  pl.BlockSpec((D_MODEL, tf), lambda i, f: (0, f)),
                  pl.BlockSpec((tf, D_MODEL), lambda i, f: (f, 0)),
                  vec, vec],
        out_specs=row,
        out_shape=jax.ShapeDtypeStruct((m, D_MODEL), F32),
        scratch_shapes=[pltpu.VMEM((tm, D_MODEL), F32), pltpu.SemaphoreType.DMA],
        compiler_params=pltpu.CompilerParams(
            dimension_semantics=("arbitrary", "arbitrary"),
            vmem_limit_bytes=_vmem_limit(blocks, scratch + 6 * _nbytes((tm, tf), F32))),
        name="ffn_ln",
    )(hb, h, w_gate, w_up, w_down, g, b)


def kernel(x, w_in, conv_w, w_attn_o, w_conv_o, w_out, ln1_g, ln1_b,
           w_ffn_gate, w_ffn_up, w_ffn_down, ln2_g, ln2_b):
    batch, seq, d = x.shape
    assert d == D_MODEL and w_in.shape == (DEPTH, D_MODEL, IN_COLS)
    assert seq % PROJ_ROWS == 0 and all(PROJ_ROWS % (dil * 16) == 0 for _, dil in DILATED_GROUPS)
    assert all(seq % (dil * BAND) == 0 for _, dil in DILATED_GROUPS)
    h = x.reshape(batch * seq, d)
    for layer in range(DEPTH):
        assert N_GROUPS == 3
        w_in_l = w_in[layer]
        sections = ATTN_WIDTH // PROJ_COLS
        w_qkv0_b = _cast_group_weights(w_in_l, 0, rows=512)

        def group_weight_rider(group, row_tile_of, n_row_tiles):
            return _CastRider(w_in_l, (D_MODEL // n_row_tiles, PROJ_COLS),
                              lambda b, t, j: (row_tile_of(b, t, j), j * sections + group),
                              (D_MODEL, 3 * GROUP_WIDTH), lambda b, t, j: (row_tile_of(b, t, j), j))

        def conv_weight_rider(row_tile_of, n_row_tiles):
            half = 3 * CONV_WIDTH // 2
            return _CastRider(w_in_l, (D_MODEL // n_row_tiles, half),
                              lambda b, t, j: (row_tile_of(b, t, j), COL_U // half + jnp.minimum(j, 1)),
                              (D_MODEL, 3 * CONV_WIDTH), lambda b, t, j: (row_tile_of(b, t, j), jnp.minimum(j, 1)))

        def gate_weight_rider(tn):
            return _CastRider(w_in_l, (D_MODEL // batch, 2 * D_MODEL // (CONV_WIDTH // tn)),
                              lambda b, j: (b, COL_GA // (2 * D_MODEL // (CONV_WIDTH // tn)) + j),
                              (D_MODEL, 2 * D_MODEL), lambda b, j: (b, j))

        qkv0, hb, w_attn_o_b, w_conv_o_b, w_qkv1_b, w_ucb_b = _qkv_proj(
            h, w_qkv0_b, 0, batch, seq,
            lambda step_of, row_tile_of, n_row_tiles: [
                _row_block_rider(w_attn_o[layer], step_of), _row_block_rider(w_conv_o[layer], step_of),
                group_weight_rider(1, row_tile_of, n_row_tiles),
                conv_weight_rider(row_tile_of, n_row_tiles)])
        qkv1, w_qkv2_b = _qkv_proj(
            hb, w_qkv1_b, 1, batch, seq,
            lambda step_of, row_tile_of, n_row_tiles: [group_weight_rider(2, row_tile_of, n_row_tiles)])
        qkv2, = _qkv_proj(hb, w_qkv2_b, 2, batch, seq, lambda step_of, row_tile_of, n_row_tiles: [])
        conv_tn = 512
        conv, w_down_b, w_out_b, w_gates_b = _conv_proj(
            hb, w_ucb_b, conv_w[layer], batch, seq,
            lambda step_of: [_row_block_rider(w_ffn_down[layer], step_of),
                             _row_block_rider(w_out[layer], step_of),
                             gate_weight_rider(conv_tn)],
            tn=conv_tn)
        attn = _attention([qkv0, qkv1, qkv2], batch, seq)
        merged, w_gate_b, w_up_b = _gated_merge(
            attn, conv, hb, w_attn_o_b, w_conv_o_b, w_gates_b,
            lambda step_of: [_row_block_rider(w_ffn_gate[layer], step_of),
                             _row_block_rider(w_ffn_up[layer], step_of)],
            tm=1024, tn=512)
        h, hb = _out_proj_ln(merged, w_out_b, h, ln1_g[layer][None, :], ln1_b[layer][None, :], tm=512)
        h = _ffn_ln(h, hb, w_gate_b, w_up_b, w_down_b, ln2_g[layer][None, :], ln2_b[layer][None, :],
                    tm=1024, tf=512)
    return h.reshape(batch, seq, d)
```

```python
import functools
import math
from typing import Callable, NamedTuple

import jax
import jax.numpy as jnp
from jax import lax
from jax.experimental import pallas as pl
from jax.experimental.pallas import tpu as pltpu

D_MODEL = 2048
HEAD_DIM = 128
HEADS_PER_GROUP = 8
DILATED_GROUPS = ((128, 1), (512, 4), (2048, 16))
N_GROUPS = len(DILATED_GROUPS)
GROUP_WIDTH = HEADS_PER_GROUP * HEAD_DIM
ATTN_WIDTH = N_GROUPS * GROUP_WIDTH
CONV_WIDTH = D_MODEL
CONV_K = 3
FFN_HIDDEN = 5632
DEPTH = 1
ALPHA = (2 * DEPTH) ** 0.25
LN_EPS = 1e-5

COL_U = 3 * ATTN_WIDTH
COL_C = COL_U + CONV_WIDTH
COL_B = COL_C + CONV_WIDTH
COL_GA = COL_B + CONV_WIDTH
COL_GC = COL_GA + D_MODEL
IN_COLS = COL_GC + D_MODEL

V7X_VMEM_BYTES = 64 * 1024 * 1024
BAND = 128
ATTN_TILES_IN_FLIGHT = 4
MAX_ROW_STRIDE = 4
CAST_RIDER_STEPS = 32
PROJ_ROWS = 1024
PROJ_COLS = 1024
OUT_LN_SPLIT = 4
MERGE_SPLIT = 2

F32 = jnp.float32
BF16 = jnp.bfloat16


def _vmem_limit(pipelined_block_bytes, resident_bytes):
    need = 2 * pipelined_block_bytes + resident_bytes
    return min(int(need * 1.25), V7X_VMEM_BYTES - 4 * 1024 * 1024)


def _nbytes(shape, dtype):
    return math.prod(shape) * jnp.dtype(dtype).itemsize


def _sigmoid(x):
    return 0.5 * (jnp.tanh(0.5 * x) + 1.0)


def _store_heads_by_residue(y, o_ref, y_ref, y2_ref, dilation):
    if dilation == 1:
        for head in range(HEADS_PER_GROUP):
            o_ref[head, 0] = y[:, head * HEAD_DIM:(head + 1) * HEAD_DIM].astype(o_ref.dtype)
        return
    f1 = min(dilation, MAX_ROW_STRIDE)
    f2 = dilation // f1
    assert f1 * f2 == dilation and f2 <= MAX_ROW_STRIDE
    rows = y_ref.shape[1]
    per_residue = rows // dilation
    for chunk in range(y.shape[0] // rows):
        dst = slice(chunk * per_residue, (chunk + 1) * per_residue)
        for head in range(HEADS_PER_GROUP):
            src = y_ref.at[head]
            src[...] = y[chunk * rows:(chunk + 1) * rows, head * HEAD_DIM:(head + 1) * HEAD_DIM]
            if f2 > 1:
                for r0 in range(f1):
                    y2_ref[head, r0 * (rows // f1):(r0 + 1) * (rows // f1), :] = src[pl.ds(r0, rows // f1, stride=f1), :]
                src = y2_ref.at[head]
            for r0 in range(f1):
                for r1 in range(f2):
                    start = r0 * (rows // f1) + r1 if f2 > 1 else r0
                    stride = f2 if f2 > 1 else f1
                    o_ref[head, r1 * f1 + r0, dst, :] = (
                        src[pl.ds(start, per_residue, stride=stride), :].astype(o_ref.dtype))


def _cast_riders(rider_refs, rider_out_refs):
    for src, dst in zip(rider_refs, rider_out_refs, strict=True):
        dst[...] = src[...].astype(dst.dtype)


def _qkv_proj_first_kernel(*refs, n_riders):
    x_ref, w_ref = refs[:2]
    rider_refs = refs[2:2 + n_riders]
    o_ref, xb_ref = refs[2 + n_riders:4 + n_riders]
    _cast_riders(rider_refs, refs[4 + n_riders:])

    @pl.when(pl.program_id(2) == 0)
    def _():
        xb_ref[...] = x_ref[...].astype(BF16)

    y = jnp.dot(xb_ref[...], w_ref[...], preferred_element_type=F32)
    _store_heads_by_residue(y, o_ref, None, None, 1)


def _qkv_proj_kernel(*refs, n_riders, dilation):
    xb_ref, w_ref = refs[:2]
    rider_refs = refs[2:2 + n_riders]
    o_ref = refs[2 + n_riders]
    rider_out_refs = refs[3 + n_riders:3 + 2 * n_riders]
    y_ref, *second_pass = refs[3 + 2 * n_riders:]
    y2_ref = second_pass[0] if second_pass else None
    _cast_riders(rider_refs, rider_out_refs)
    y = jnp.dot(xb_ref[...], w_ref[...], preferred_element_type=F32)
    _store_heads_by_residue(y, o_ref, y_ref, y2_ref, dilation)


class _CastRider(NamedTuple):
    array: jax.Array
    block: tuple
    in_index: Callable
    out_shape: tuple
    out_index: Callable


def _cast_kernel(w_ref, o_ref):
    o_ref[...] = w_ref[...].astype(o_ref.dtype)


def _cast_group_weights(w_in, group, *, rows):
    sections = ATTN_WIDTH // GROUP_WIDTH
    blocks = _nbytes((rows, GROUP_WIDTH), F32) + _nbytes((rows, GROUP_WIDTH), BF16)
    return pl.pallas_call(
        _cast_kernel,
        grid=(3, D_MODEL // rows),
        in_specs=[pl.BlockSpec((rows, GROUP_WIDTH), lambda j, i: (i, j * sections + group))],
        out_specs=pl.BlockSpec((rows, GROUP_WIDTH), lambda j, i: (i, j)),
        out_shape=jax.ShapeDtypeStruct((D_MODEL, 3 * GROUP_WIDTH), BF16),
        compiler_params=pltpu.CompilerParams(
            dimension_semantics=("parallel", "parallel"),
            vmem_limit_bytes=_vmem_limit(blocks, 0)),
        name="cast_group_weights",
    )(w_in)


def _row_block_rider(w, step_of):
    rows = w.shape[0] // CAST_RIDER_STEPS
    assert rows * CAST_RIDER_STEPS == w.shape[0] and rows % 16 == 0

    def index(*idx):
        return jnp.minimum(step_of(*idx), CAST_RIDER_STEPS - 1), 0

    return _CastRider(w, (rows, w.shape[1]), index, w.shape, index)


def _cast_rider_specs(riders):
    in_specs = [pl.BlockSpec(r.block, r.in_index) for r in riders]
    out_specs = [pl.BlockSpec(r.block, r.out_index) for r in riders]
    shapes = [jax.ShapeDtypeStruct(r.out_shape, BF16) for r in riders]
    return in_specs, out_specs, shapes, sum(_nbytes(r.block, F32) * 3 // 2 for r in riders)


def _qkv_proj(x, w_qkv, group, batch, seq, make_riders):
    dilation = DILATED_GROUPS[group][1]
    first = x.dtype == F32
    assert first == (group == 0) and dilation == (1 if first else dilation)
    length = seq // dilation
    rows = PROJ_ROWS if first else min(seq, 2 * PROJ_ROWS)
    tiles = seq // rows
    per_residue = rows // dilation
    assert PROJ_COLS == GROUP_WIDTH and w_qkv.shape == (D_MODEL, 3 * GROUP_WIDTH) and rows * tiles == seq
    row_spec = pl.BlockSpec((rows, D_MODEL), lambda b, t, j: (b * tiles + t, 0))
    qkv_spec = pl.BlockSpec((HEADS_PER_GROUP, None, None, dilation, per_residue, HEAD_DIM),
                            lambda b, t, j: (0, b, j, 0, t, 0))
    qkv_shape = jax.ShapeDtypeStruct((HEADS_PER_GROUP, batch, 3, dilation, length, HEAD_DIM), BF16)
    blocks = (_nbytes((rows, D_MODEL), x.dtype) + _nbytes((D_MODEL, PROJ_COLS), BF16)
              + _nbytes((rows, PROJ_COLS), BF16) + (_nbytes((rows, D_MODEL), BF16) if first else 0))
    passes = 0 if first else (1 if dilation <= MAX_ROW_STRIDE else 2)
    regroup = [pltpu.VMEM((HEADS_PER_GROUP, PROJ_ROWS, HEAD_DIM), F32)] * passes
    w_spec = pl.BlockSpec((D_MODEL, PROJ_COLS), lambda b, t, j: (0, j))
    riders = make_riders(lambda b, t, j: (b * tiles + t) * 3 + j, lambda b, t, j: b * tiles + t,
                         batch * tiles)
    rider_in_specs, rider_specs, rider_shapes, rider_bytes = _cast_rider_specs(riders)
    own_out_specs, own_out_shapes = [qkv_spec], [qkv_shape]
    if first:
        own_out_specs.append(row_spec)
        own_out_shapes.append(jax.ShapeDtypeStruct(x.shape, BF16))
        body = functools.partial(_qkv_proj_first_kernel, n_riders=len(riders))
    else:
        body = functools.partial(_qkv_proj_kernel, n_riders=len(riders), dilation=dilation)
    out = pl.pallas_call(
        body,
        grid=(batch, tiles, 3),
        in_specs=[row_spec, w_spec] + rider_in_specs,
        out_specs=own_out_specs + rider_specs,
        out_shape=own_out_shapes + rider_shapes,
        scratch_shapes=regroup,
        compiler_params=pltpu.CompilerParams(
            dimension_semantics=("arbitrary", "arbitrary", "arbitrary"),
            vmem_limit_bytes=_vmem_limit(blocks + rider_bytes,
                                         (0 if first else 2 * _nbytes((PROJ_ROWS, PROJ_COLS), F32))
                                         + 2 * _nbytes((rows, PROJ_COLS), F32))),
        name=f"qkv_proj_dilation{dilation}",
    )(x, w_qkv, *[r.array for r in riders])
    return (out[0].reshape(HEADS_PER_GROUP, batch, 3, seq, HEAD_DIM),) + tuple(out[1:])


def _conv_proj_kernel(*refs, n_riders):
    x_ref, wu_ref, wc_ref, wb_ref, cw_ref = refs[:5]
    o_ref = refs[5 + n_riders]
    _cast_riders(refs[5:5 + n_riders], refs[6 + n_riders:])
    x = x_ref[...]
    z = (jnp.dot(x, wc_ref[...], preferred_element_type=F32)
         * jnp.dot(x, wu_ref[...], preferred_element_type=F32))
    cw = cw_ref[...]
    row = lax.broadcasted_iota(jnp.int32, z.shape, 0)
    y = cw[0:1, :] * z
    for tap in range(1, CONV_K):
        shifted = jnp.where(row >= tap, pltpu.roll(z, tap, axis=0), 0.0)
        y = y + cw[tap:tap + 1, :] * shifted
    o_ref[...] = (jnp.dot(x, wb_ref[...], preferred_element_type=F32) * y).astype(o_ref.dtype)


def _conv_proj(xb, w_ucb, conv_w, batch, seq, make_riders, *, tn):
    def w_spec(section):
        return pl.BlockSpec((D_MODEL, tn), lambda b, j: (0, section * (CONV_WIDTH // tn) + j))

    col_blocks = CONV_WIDTH // tn
    assert w_ucb.shape == (D_MODEL, 3 * CONV_WIDTH)
    riders = make_riders(lambda b, j: b * col_blocks + j)
    rider_in_specs, rider_specs, rider_shapes, rider_bytes = _cast_rider_specs(riders)
    blocks = (_nbytes((seq, D_MODEL), BF16) + 3 * _nbytes((D_MODEL, tn), BF16)
              + _nbytes((CONV_K, tn), F32) + _nbytes((seq, tn), BF16) + rider_bytes)
    return pl.pallas_call(
        functools.partial(_conv_proj_kernel, n_riders=len(riders)),
        grid=(batch, col_blocks),
        in_specs=[pl.BlockSpec((seq, D_MODEL), lambda b, j: (b, 0)),
                  w_spec(0), w_spec(1), w_spec(2),
                  pl.BlockSpec((CONV_K, tn), lambda b, j: (0, j))] + rider_in_specs,
        out_specs=[pl.BlockSpec((seq, tn), lambda b, j: (b, j))] + rider_specs,
        out_shape=[jax.ShapeDtypeStruct((batch * seq, CONV_WIDTH), BF16)] + rider_shapes,
        compiler_params=pltpu.CompilerParams(
            dimension_semantics=("arbitrary", "arbitrary"),
            vmem_limit_bytes=_vmem_limit(blocks, 5 * _nbytes((seq, tn), F32))),
        name="conv_proj",
    )(xb, w_ucb, w_ucb, w_ucb, conv_w, *[r.array for r in riders])


def _attn_kernel(qkv1_ref, qkv2_ref, qkv3_ref, o_ref, num_ref, den_ref, max_ref, bias_ref):
    seq = qkv1_ref.shape[1]
    n_tiles = seq // BAND
    scale = HEAD_DIM ** -0.5
    qi = lax.broadcasted_iota(jnp.int32, (BAND, 2 * BAND), 0)
    kj = lax.broadcasted_iota(jnp.int32, (BAND, 2 * BAND), 1)
    visible = ((kj < BAND) & (qi <= kj)) | ((kj >= BAND) & (qi >= kj - BAND))
    bias_ref[...] = jnp.where(visible, 0.0, -jnp.inf)

    def probabilities(q_ref, k_ref, n, has_prev):
        q = q_ref[n * BAND:(n + 1) * BAND, :]
        keys = slice((n - 1) * BAND, (n + 1) * BAND) if has_prev else slice(n * BAND, (n + 1) * BAND)
        bias = bias_ref[...] if has_prev else bias_ref[:, BAND:]
        s = lax.dot_general(q, k_ref[keys, :], (((1,), (1,)), ((), ())),
                            preferred_element_type=F32) * scale + bias
        m = s.max(-1, keepdims=True)
        return jnp.exp(s - m).astype(BF16), m, keys

    assert DILATED_GROUPS[0][1] == 1 and all(d % MAX_ROW_STRIDE == 0 for _, d in DILATED_GROUPS[1:])
    quarter = seq // MAX_ROW_STRIDE

    def result_rows(g, n):
        dilation = DILATED_GROUPS[g][1]
        if dilation == 1:
            return pl.ds(n * BAND, BAND)
        residue, t = divmod(n, n_tiles // dilation)
        step = dilation // MAX_ROW_STRIDE
        start = (residue % MAX_ROW_STRIDE) * quarter + step * t * BAND + residue // MAX_ROW_STRIDE
        return pl.ds(start, BAND, stride=step)

    def finish(g, v_ref, n, p, m, keys):
        rows = result_rows(g, n)
        v = v_ref[keys, :]
        acc = jnp.dot(p, jnp.concatenate([v, jnp.ones_like(v)], axis=1), preferred_element_type=F32)
        num_ref[g, rows, :] = acc[:, :HEAD_DIM]
        den_ref[g, rows, :] = acc[:, HEAD_DIM:]
        max_ref[g, rows, :] = jnp.broadcast_to(m, (BAND, HEAD_DIM))

    groups = [(r.at[0], r.at[1], r.at[2]) for r in (qkv1_ref, qkv2_ref, qkv3_ref)]
    in_flight = []
    for g, (q_ref, k_ref, v_ref) in enumerate(groups):
        tiles_per_residue = n_tiles // DILATED_GROUPS[g][1]
        for n in range(n_tiles):
            has_prev = n % tiles_per_residue > 0
            in_flight.append((g, v_ref, n) + probabilities(q_ref, k_ref, n, has_prev))
            if len(in_flight) > ATTN_TILES_IN_FLIGHT:
                finish(*in_flight.pop(0))
    for pending in in_flight:
        finish(*pending)

    chunks_per_quarter = quarter // BAND

    def mix(c, _):
        mixed = pl.ds(pl.multiple_of(c * BAND, BAND), BAND)
        natural = pl.ds(MAX_ROW_STRIDE * (c % chunks_per_quarter) * BAND + c // chunks_per_quarter,
                        BAND, stride=MAX_ROW_STRIDE)
        rows = [natural] + [mixed] * (N_GROUPS - 1)
        m = [max_ref[g, rows[g], :] for g in range(N_GROUPS)]
        top = jnp.maximum(jnp.maximum(m[0], m[1]), m[2])
        w = [jnp.exp(x - top) for x in m]
        num = sum(w[g] * num_ref[g, rows[g], :] for g in range(N_GROUPS))
        den = sum(w[g] * den_ref[g, rows[g], :] for g in range(N_GROUPS))
        max_ref[0, natural, :] = num / den
        return 0

    lax.fori_loop(0, n_tiles, mix, 0)
    o_ref[...] = max_ref[0].astype(o_ref.dtype)


def _attention(qkv, batch, seq):
    heads = HEADS_PER_GROUP
    qkv_spec = pl.BlockSpec((None, None, 3, seq, HEAD_DIM), lambda b, h: (h, b, 0, 0, 0))
    scratch = 3 * _nbytes((N_GROUPS, seq, HEAD_DIM), F32)
    return pl.pallas_call(
        _attn_kernel,
        grid=(batch, heads),
        in_specs=[qkv_spec] * N_GROUPS,
        out_specs=pl.BlockSpec((seq, HEAD_DIM), lambda b, h: (b, h)),
        out_shape=jax.ShapeDtypeStruct((batch * seq, GROUP_WIDTH), BF16),
        scratch_shapes=[pltpu.VMEM((N_GROUPS, seq, HEAD_DIM), F32),
                        pltpu.VMEM((N_GROUPS, seq, HEAD_DIM), F32),
                        pltpu.VMEM((N_GROUPS, seq, HEAD_DIM), F32),
                        pltpu.VMEM((BAND, 2 * BAND), F32)],
        compiler_params=pltpu.CompilerParams(
            dimension_semantics=("parallel", "parallel"),
            vmem_limit_bytes=_vmem_limit(10 * _nbytes((seq, HEAD_DIM), BF16), scratch + 4 * 1024 * 1024)),
        name="dilated_attention",
    )(*qkv)


def _merge_kernel(*refs, n_riders):
    attn_ref, conv_ref, x_ref, wao_ref, wco_ref, wga_ref, wgc_ref = refs[:7]
    o_ref = refs[7 + n_riders]
    _cast_riders(refs[7:7 + n_riders], refs[8 + n_riders:])
    x, attn, conv = x_ref[...], attn_ref[...], conv_ref[...]
    slab = o_ref.shape[1] // MERGE_SPLIT
    for k in range(MERGE_SPLIT):
        cols = slice(k * slab, (k + 1) * slab)
        a = jnp.dot(attn, wao_ref[:, cols], preferred_element_type=F32)
        ga = _sigmoid(jnp.dot(x, wga_ref[:, cols], preferred_element_type=F32))
        c = jnp.dot(conv, wco_ref[:, cols], preferred_element_type=F32)
        gc = _sigmoid(jnp.dot(x, wgc_ref[:, cols], preferred_element_type=F32))
        o_ref[:, cols] = (ga * a + gc * c).astype(o_ref.dtype)


def _gated_merge(attn, conv, xb, w_attn_o, w_conv_o, w_gates, make_riders, *, tm, tn):
    m = attn.shape[0]
    assert w_gates.shape == (D_MODEL, 2 * D_MODEL)
    col_blocks = D_MODEL // tn
    riders = make_riders(lambda i, j: i * col_blocks + j)
    rider_in_specs, rider_specs, rider_shapes, rider_bytes = _cast_rider_specs(riders)
    blocks = (_nbytes((tm, GROUP_WIDTH), BF16) + _nbytes((tm, CONV_WIDTH), BF16)
              + _nbytes((tm, D_MODEL), BF16) + _nbytes((tm, tn), BF16)
              + _nbytes((GROUP_WIDTH + CONV_WIDTH + 2 * D_MODEL, tn), BF16) + rider_bytes)
    return pl.pallas_call(
        functools.partial(_merge_kernel, n_riders=len(riders)),
        grid=(m // tm, col_blocks),
        in_specs=[pl.BlockSpec((tm, GROUP_WIDTH), lambda i, j: (i, 0)),
                  pl.BlockSpec((tm, CONV_WIDTH), lambda i, j: (i, 0)),
                  pl.BlockSpec((tm, D_MODEL), lambda i, j: (i, 0)),
                  pl.BlockSpec((GROUP_WIDTH, tn), lambda i, j: (0, j)),
                  pl.BlockSpec((CONV_WIDTH, tn), lambda i, j: (0, j)),
                  pl.BlockSpec((D_MODEL, tn), lambda i, j: (0, j)),
                  pl.BlockSpec((D_MODEL, tn), lambda i, j: (0, col_blocks + j))] + rider_in_specs,
        out_specs=[pl.BlockSpec((tm, tn), lambda i, j: (i, j))] + rider_specs,
        out_shape=[jax.ShapeDtypeStruct((m, D_MODEL), BF16)] + rider_shapes,
        compiler_params=pltpu.CompilerParams(
            dimension_semantics=("arbitrary", "arbitrary"),
            vmem_limit_bytes=_vmem_limit(blocks, 5 * _nbytes((tm, tn), F32))),
        name="gated_merge",
    )(attn, conv, xb, w_attn_o, w_conv_o, w_gates, w_gates, *[r.array for r in riders])


def _layer_norm(z, g, b):
    mu = jnp.mean(z, axis=-1, keepdims=True)
    zc = z - mu
    var = jnp.mean(zc * zc, axis=-1, keepdims=True)
    return zc * lax.rsqrt(var + LN_EPS) * g + b


def _out_ln_kernel(mi_ref, w_ref, x_ref, g_ref, b_ref, o_ref, ob_ref):
    part = mi_ref.shape[0] // OUT_LN_SPLIT
    for rows in (slice(k * part, (k + 1) * part) for k in range(OUT_LN_SPLIT)):
        y = jnp.dot(mi_ref[rows, :], w_ref[...], preferred_element_type=F32)
        h = _layer_norm(ALPHA * x_ref[rows, :] + y, g_ref[...], b_ref[...])
        o_ref[rows, :] = h
        ob_ref[rows, :] = h.astype(ob_ref.dtype)


def _out_proj_ln(merged, w_out, x, g, b, *, tm):
    m = merged.shape[0]
    blocks = (2 * _nbytes((tm, D_MODEL), BF16) + _nbytes((D_MODEL, D_MODEL), BF16)
              + 2 * _nbytes((tm, D_MODEL), F32))
    row = pl.BlockSpec((tm, D_MODEL), lambda i: (i, 0))
    vec = pl.BlockSpec((1, D_MODEL), lambda i: (0, 0))
    return pl.pallas_call(
        _out_ln_kernel,
        grid=(m // tm,),
        in_specs=[row, pl.BlockSpec((D_MODEL, D_MODEL), lambda i: (0, 0)), row, vec, vec],
        out_specs=[row, row],
        out_shape=[jax.ShapeDtypeStruct((m, D_MODEL), F32), jax.ShapeDtypeStruct((m, D_MODEL), BF16)],
        compiler_params=pltpu.CompilerParams(
            dimension_semantics=("parallel",),
            vmem_limit_bytes=_vmem_limit(blocks, 3 * _nbytes((tm, D_MODEL), F32))),
        name="out_proj_ln",
    )(merged, w_out, x, g, b)


def _ffn_kernel(hb_ref, h_hbm_ref, wg_ref, wu_ref, wd_ref, g_ref, b_ref, o_ref, res_ref, res_sem):
    i, f = pl.program_id(0), pl.program_id(1)
    tm = o_ref.shape[0]
    residual_copy = pltpu.make_async_copy(
        h_hbm_ref.at[pl.ds(pl.multiple_of(i * tm, tm), tm), :], res_ref, res_sem)

    @pl.when(f == 0)
    def _():
        residual_copy.start()
        o_ref[...] = jnp.zeros_like(o_ref)

    hb = hb_ref[...]
    gate = jnp.dot(hb, wg_ref[...], preferred_element_type=F32)
    up = jnp.dot(hb, wu_ref[...], preferred_element_type=F32)
    hidden = (gate * _sigmoid(gate) * up).astype(BF16)
    o_ref[...] += jnp.dot(hidden, wd_ref[...], preferred_element_type=F32)

    @pl.when(f == pl.num_programs(1) - 1)
    def _():
        residual_copy.wait()
        o_ref[...] = _layer_norm(ALPHA * res_ref[...] + o_ref[...], g_ref[...], b_ref[...])


def _ffn_ln(h, hb, w_gate, w_up, w_down, g, b, *, tm, tf):
    m = h.shape[0]
    blocks = (_nbytes((tm, D_MODEL), BF16) + _nbytes((tm, D_MODEL), F32)
              + 2 * _nbytes((D_MODEL, tf), BF16) + _nbytes((tf, D_MODEL), BF16))
    scratch = _nbytes((tm, D_MODEL), F32)
    row = pl.BlockSpec((tm, D_MODEL), lambda i, f: (i, 0))
    vec = pl.BlockSpec((1, D_MODEL), lambda i, f: (0, 0))
    return pl.pallas_call(
        _ffn_kernel,
        grid=(m // tm, FFN_HIDDEN // tf),
        in_specs=[row,
                  pl.BlockSpec(memory_space=pl.ANY),
                  pl.BlockSpec((D_MODEL, tf), lambda i, f: (0, f)),
                  pl.BlockSpec((D_MODEL, tf), lambda i, f: (0, f)),
                  pl.BlockSpec((tf, D_MODEL), lambda i, f: (f, 0)),
                  vec, vec],
        out_specs=row,
        out_shape=jax.ShapeDtypeStruct((m, D_MODEL), F32),
        scratch_shapes=[pltpu.VMEM((tm, D_MODEL), F32), pltpu.SemaphoreType.DMA],
        compiler_params=pltpu.CompilerParams(
            dimension_semantics=("arbitrary", "arbitrary"),
            vmem_limit_bytes=_vmem_limit(blocks, scratch + 6 * _nbytes((tm, tf), F32))),
        name="ffn_ln",
    )(hb, h, w_gate, w_up, w_down, g, b)


def kernel(x, w_in, conv_w, w_attn_o, w_conv_o, w_out, ln1_g, ln1_b,
           w_ffn_gate, w_ffn_up, w_ffn_down, ln2_g, ln2_b):
    batch, seq, d = x.shape
    assert d == D_MODEL and w_in.shape == (DEPTH, D_MODEL, IN_COLS)
    assert seq % PROJ_ROWS == 0 and all(PROJ_ROWS % (dil * 16) == 0 for _, dil in DILATED_GROUPS)
    assert all(seq % (dil * BAND) == 0 for _, dil in DILATED_GROUPS)
    h = x.reshape(batch * seq, d)
    for layer in range(DEPTH):
        assert N_GROUPS == 3
        w_in_l = w_in[layer]
        sections = ATTN_WIDTH // PROJ_COLS
        w_qkv0_b = _cast_group_weights(w_in_l, 0, rows=512)

        def group_weight_rider(group, row_tile_of, n_row_tiles):
            return _CastRider(w_in_l, (D_MODEL // n_row_tiles, PROJ_COLS),
                              lambda b, t, j: (row_tile_of(b, t, j), j * sections + group),
                              (D_MODEL, 3 * GROUP_WIDTH), lambda b, t, j: (row_tile_of(b, t, j), j))

        def conv_weight_rider(row_tile_of, n_row_tiles):
            half = 3 * CONV_WIDTH // 2
            return _CastRider(w_in_l, (D_MODEL // n_row_tiles, half),
                              lambda b, t, j: (row_tile_of(b, t, j), COL_U // half + jnp.minimum(j, 1)),
                              (D_MODEL, 3 * CONV_WIDTH), lambda b, t, j: (row_tile_of(b, t, j), jnp.minimum(j, 1)))

        def gate_weight_rider(tn):
            return _CastRider(w_in_l, (D_MODEL // batch, 2 * D_MODEL // (CONV_WIDTH // tn)),
                              lambda b, j: (b, COL_GA // (2 * D_MODEL // (CONV_WIDTH // tn)) + j),
                              (D_MODEL, 2 * D_MODEL), lambda b, j: (b, j))

        qkv0, hb, w_attn_o_b, w_conv_o_b, w_qkv1_b = _qkv_proj(
            h, w_qkv0_b, 0, batch, seq,
            lambda step_of, row_tile_of, n_row_tiles: [
                _row_block_rider(w_attn_o[layer], step_of), _row_block_rider(w_conv_o[layer], step_of),
                group_weight_rider(1, row_tile_of, n_row_tiles)])
        qkv1, w_qkv2_b, w_ucb_b = _qkv_proj(
            hb, w_qkv1_b, 1, batch, seq,
            lambda step_of, row_tile_of, n_row_tiles: [group_weight_rider(2, row_tile_of, n_row_tiles),
                                                       conv_weight_rider(row_tile_of, n_row_tiles)])
        qkv2, = _qkv_proj(hb, w_qkv2_b, 2, batch, seq, lambda step_of, row_tile_of, n_row_tiles: [])
        conv_tn = 512
        conv, w_down_b, w_out_b, w_gates_b = _conv_proj(
            hb, w_ucb_b, conv_w[layer], batch, seq,
            lambda step_of: [_row_block_rider(w_ffn_down[layer], step_of),
                             _row_block_rider(w_out[layer], step_of),
                             gate_weight_rider(conv_tn)],
            tn=conv_tn)
        attn = _attention([qkv0, qkv1, qkv2], batch, seq)
        merged, w_gate_b, w_up_b = _gated_merge(
            attn, conv, hb, w_attn_o_b, w_conv_o_b, w_gates_b,
            lambda step_of: [_row_block_rider(w_ffn_gate[layer], step_of),
                             _row_block_rider(w_ffn_up[layer], step_of)],
            tm=1024, tn=512)
        h, hb = _out_proj_ln(merged, w_out_b, h, ln1_g[layer][None, :], ln1_b[layer][None, :], tm=512)
        h = _ffn_ln(h, hb, w_gate_b, w_up_b, w_down_b, ln2_g[layer][None, :], ln2_b[layer][None, :],
                    tm=1024, tf=512)
    return h.reshape(batch, seq, d)
```

```python
import functools
import math
from typing import Callable, NamedTuple

import jax
import jax.numpy as jnp
from jax import lax
from jax.experimental import pallas as pl
from jax.experimental.pallas import tpu as pltpu

D_MODEL = 2048
HEAD_DIM = 128
HEADS_PER_GROUP = 8
DILATED_GROUPS = ((128, 1), (512, 4), (2048, 16))
N_GROUPS = len(DILATED_GROUPS)
GROUP_WIDTH = HEADS_PER_GROUP * HEAD_DIM
ATTN_WIDTH = N_GROUPS * GROUP_WIDTH
CONV_WIDTH = D_MODEL
CONV_K = 3
FFN_HIDDEN = 5632
DEPTH = 1
ALPHA = (2 * DEPTH) ** 0.25
LN_EPS = 1e-5

COL_U = 3 * ATTN_WIDTH
COL_C = COL_U + CONV_WIDTH
COL_B = COL_C + CONV_WIDTH
COL_GA = COL_B + CONV_WIDTH
COL_GC = COL_GA + D_MODEL
IN_COLS = COL_GC + D_MODEL

V7X_VMEM_BYTES = 64 * 1024 * 1024
BAND = 128
ATTN_HEADS_PER_STEP = 2
ATTN_TILES_IN_FLIGHT = 4
MAX_ROW_STRIDE = 4
CAST_RIDER_STEPS = 32
PROJ_ROWS = 1024
PROJ_COLS = 1024
OUT_LN_SPLIT = 4
MERGE_SPLIT = 2

F32 = jnp.float32
BF16 = jnp.bfloat16


def _vmem_limit(pipelined_block_bytes, resident_bytes):
    need = 2 * pipelined_block_bytes + resident_bytes
    return min(int(need * 1.25), V7X_VMEM_BYTES - 4 * 1024 * 1024)


def _nbytes(shape, dtype):
    return math.prod(shape) * jnp.dtype(dtype).itemsize


def _sigmoid(x):
    return 0.5 * (jnp.tanh(0.5 * x) + 1.0)


def _store_heads_by_residue(y, o_ref, y_ref, y2_ref, dilation):
    if dilation == 1:
        for head in range(HEADS_PER_GROUP):
            o_ref[head, 0] = y[:, head * HEAD_DIM:(head + 1) * HEAD_DIM].astype(o_ref.dtype)
        return
    f1 = min(dilation, MAX_ROW_STRIDE)
    f2 = dilation // f1
    assert f1 * f2 == dilation and f2 <= MAX_ROW_STRIDE
    rows = y_ref.shape[1]
    per_residue = rows // dilation
    for chunk in range(y.shape[0] // rows):
        dst = slice(chunk * per_residue, (chunk + 1) * per_residue)
        for head in range(HEADS_PER_GROUP):
            src = y_ref.at[head]
            src[...] = y[chunk * rows:(chunk + 1) * rows, head * HEAD_DIM:(head + 1) * HEAD_DIM]
            if f2 > 1:
                for r0 in range(f1):
                    y2_ref[head, r0 * (rows // f1):(r0 + 1) * (rows // f1), :] = src[pl.ds(r0, rows // f1, stride=f1), :]
                src = y2_ref.at[head]
            for r0 in range(f1):
                for r1 in range(f2):
                    start = r0 * (rows // f1) + r1 if f2 > 1 else r0
                    stride = f2 if f2 > 1 else f1
                    o_ref[head, r1 * f1 + r0, dst, :] = (
                        src[pl.ds(start, per_residue, stride=stride), :].astype(o_ref.dtype))


def _cast_riders(rider_refs, rider_out_refs):
    for src, dst in zip(rider_refs, rider_out_refs, strict=True):
        dst[...] = src[...].astype(dst.dtype)


def _qkv_proj_first_kernel(*refs, n_riders):
    x_ref, w_ref = refs[:2]
    rider_refs = refs[2:2 + n_riders]
    o_ref, xb_ref = refs[2 + n_riders:4 + n_riders]
    _cast_riders(rider_refs, refs[4 + n_riders:])

    @pl.when(pl.program_id(2) == 0)
    def _():
        xb_ref[...] = x_ref[...].astype(BF16)

    y = jnp.dot(xb_ref[...], w_ref[...], preferred_element_type=F32)
    _store_heads_by_residue(y, o_ref, None, None, 1)


def _qkv_proj_kernel(*refs, n_riders, dilation):
    xb_ref, w_ref = refs[:2]
    rider_refs = refs[2:2 + n_riders]
    o_ref = refs[2 + n_riders]
    rider_out_refs = refs[3 + n_riders:3 + 2 * n_riders]
    y_ref, *second_pass = refs[3 + 2 * n_riders:]
    y2_ref = second_pass[0] if second_pass else None
    _cast_riders(rider_refs, rider_out_refs)
    y = jnp.dot(xb_ref[...], w_ref[...], preferred_element_type=F32)
    _store_heads_by_residue(y, o_ref, y_ref, y2_ref, dilation)


class _CastRider(NamedTuple):
    array: jax.Array
    block: tuple
    in_index: Callable
    out_shape: tuple
    out_index: Callable


def _cast_kernel(w_ref, o_ref):
    o_ref[...] = w_ref[...].astype(o_ref.dtype)


def _cast_group_weights(w_in, group, *, rows):
    sections = ATTN_WIDTH // GROUP_WIDTH
    blocks = _nbytes((rows, GROUP_WIDTH), F32) + _nbytes((rows, GROUP_WIDTH), BF16)
    return pl.pallas_call(
        _cast_kernel,
        grid=(3, D_MODEL // rows),
        in_specs=[pl.BlockSpec((rows, GROUP_WIDTH), lambda j, i: (i, j * sections + group))],
        out_specs=pl.BlockSpec((rows, GROUP_WIDTH), lambda j, i: (i, j)),
        out_shape=jax.ShapeDtypeStruct((D_MODEL, 3 * GROUP_WIDTH), BF16),
        compiler_params=pltpu.CompilerParams(
            dimension_semantics=("parallel", "parallel"),
            vmem_limit_bytes=_vmem_limit(blocks, 0)),
        name="cast_group_weights",
    )(w_in)


def _row_block_rider(w, step_of):
    rows = w.shape[0] // CAST_RIDER_STEPS
    assert rows * CAST_RIDER_STEPS == w.shape[0] and rows % 16 == 0

    def index(*idx):
        return jnp.minimum(step_of(*idx), CAST_RIDER_STEPS - 1), 0

    return _CastRider(w, (rows, w.shape[1]), index, w.shape, index)


def _cast_rider_specs(riders):
    in_specs = [pl.BlockSpec(r.block, r.in_index) for r in riders]
    out_specs = [pl.BlockSpec(r.block, r.out_index) for r in riders]
    shapes = [jax.ShapeDtypeStruct(r.out_shape, BF16) for r in riders]
    return in_specs, out_specs, shapes, sum(_nbytes(r.block, F32) * 3 // 2 for r in riders)


def _qkv_proj(x, w_qkv, group, batch, seq, make_riders):
    dilation = DILATED_GROUPS[group][1]
    first = x.dtype == F32
    assert first == (group == 0) and dilation == (1 if first else dilation)
    length = seq // dilation
    rows = PROJ_ROWS if first else min(seq, 2 * PROJ_ROWS)
    tiles = seq // rows
    per_residue = rows // dilation
    assert PROJ_COLS == GROUP_WIDTH and w_qkv.shape == (D_MODEL, 3 * GROUP_WIDTH) and rows * tiles == seq
    row_spec = pl.BlockSpec((rows, D_MODEL), lambda b, t, j: (b * tiles + t, 0))
    qkv_spec = pl.BlockSpec((HEADS_PER_GROUP, None, None, dilation, per_residue, HEAD_DIM),
                            lambda b, t, j: (0, b, j, 0, t, 0))
    qkv_shape = jax.ShapeDtypeStruct((HEADS_PER_GROUP, batch, 3, dilation, length, HEAD_DIM), BF16)
    blocks = (_nbytes((rows, D_MODEL), x.dtype) + _nbytes((D_MODEL, PROJ_COLS), BF16)
              + _nbytes((rows, PROJ_COLS), BF16) + (_nbytes((rows, D_MODEL), BF16) if first else 0))
    passes = 0 if first else (1 if dilation <= MAX_ROW_STRIDE else 2)
    regroup = [pltpu.VMEM((HEADS_PER_GROUP, PROJ_ROWS, HEAD_DIM), F32)] * passes
    w_spec = pl.BlockSpec((D_MODEL, PROJ_COLS), lambda b, t, j: (0, j))
    riders = make_riders(lambda b, t, j: (b * tiles + t) * 3 + j, lambda b, t, j: b * tiles + t,
                         batch * tiles)
    rider_in_specs, rider_specs, rider_shapes, rider_bytes = _cast_rider_specs(riders)
    own_out_specs, own_out_shapes = [qkv_spec], [qkv_shape]
    if first:
        own_out_specs.append(row_spec)
        own_out_shapes.append(jax.ShapeDtypeStruct(x.shape, BF16))
        body = functools.partial(_qkv_proj_first_kernel, n_riders=len(riders))
    else:
        body = functools.partial(_qkv_proj_kernel, n_riders=len(riders), dilation=dilation)
    out = pl.pallas_call(
        body,
        grid=(batch, tiles, 3),
        in_specs=[row_spec, w_spec] + rider_in_specs,
        out_specs=own_out_specs + rider_specs,
        out_shape=own_out_shapes + rider_shapes,
        scratch_shapes=regroup,
        compiler_params=pltpu.CompilerParams(
            dimension_semantics=("arbitrary", "arbitrary", "arbitrary"),
            vmem_limit_bytes=_vmem_limit(blocks + rider_bytes,
                                         (0 if first else 2 * _nbytes((PROJ_ROWS, PROJ_COLS), F32))
                                         + 2 * _nbytes((rows, PROJ_COLS), F32))),
        name=f"qkv_proj_dilation{dilation}",
    )(x, w_qkv, *[r.array for r in riders])
    return (out[0].reshape(HEADS_PER_GROUP, batch, 3, seq, HEAD_DIM),) + tuple(out[1:])


def _conv_proj_kernel(*refs, n_riders):
    x_ref, wu_ref, wc_ref, wb_ref, cw_ref = refs[:5]
    o_ref = refs[5 + n_riders]
    _cast_riders(refs[5:5 + n_riders], refs[6 + n_riders:])
    x = x_ref[...]
    z = (jnp.dot(x, wc_ref[...], preferred_element_type=F32)
         * jnp.dot(x, wu_ref[...], preferred_element_type=F32))
    cw = cw_ref[...]
    row = lax.broadcasted_iota(jnp.int32, z.shape, 0)
    y = cw[0:1, :] * z
    for tap in range(1, CONV_K):
        shifted = jnp.where(row >= tap, pltpu.roll(z, tap, axis=0), 0.0)
        y = y + cw[tap:tap + 1, :] * shifted
    o_ref[...] = (jnp.dot(x, wb_ref[...], preferred_element_type=F32) * y).astype(o_ref.dtype)


def _conv_proj(xb, w_ucb, conv_w, batch, seq, make_riders, *, tn):
    def w_spec(section):
        return pl.BlockSpec((D_MODEL, tn), lambda b, j: (0, section * (CONV_WIDTH // tn) + j))

    col_blocks = CONV_WIDTH // tn
    assert w_ucb.shape == (D_MODEL, 3 * CONV_WIDTH)
    riders = make_riders(lambda b, j: b * col_blocks + j)
    rider_in_specs, rider_specs, rider_shapes, rider_bytes = _cast_rider_specs(riders)
    blocks = (_nbytes((seq, D_MODEL), BF16) + 3 * _nbytes((D_MODEL, tn), BF16)
              + _nbytes((CONV_K, tn), F32) + _nbytes((seq, tn), BF16) + rider_bytes)
    return pl.pallas_call(
        functools.partial(_conv_proj_kernel, n_riders=len(riders)),
        grid=(batch, col_blocks),
        in_specs=[pl.BlockSpec((seq, D_MODEL), lambda b, j: (b, 0)),
                  w_spec(0), w_spec(1), w_spec(2),
                  pl.BlockSpec((CONV_K, tn), lambda b, j: (0, j))] + rider_in_specs,
        out_specs=[pl.BlockSpec((seq, tn), lambda b, j: (b, j))] + rider_specs,
        out_shape=[jax.ShapeDtypeStruct((batch * seq, CONV_WIDTH), BF16)] + rider_shapes,
        compiler_params=pltpu.CompilerParams(
            dimension_semantics=("arbitrary", "arbitrary"),
            vmem_limit_bytes=_vmem_limit(blocks, 5 * _nbytes((seq, tn), F32))),
        name="conv_proj",
    )(xb, w_ucb, w_ucb, w_ucb, conv_w, *[r.array for r in riders])


def _attn_kernel(qkv1_ref, qkv2_ref, qkv3_ref, o_ref, num_ref, den_ref, max_ref, bias_ref):
    qi = lax.broadcasted_iota(jnp.int32, (BAND, 2 * BAND), 0)
    kj = lax.broadcasted_iota(jnp.int32, (BAND, 2 * BAND), 1)
    visible = ((kj < BAND) & (qi <= kj)) | ((kj >= BAND) & (qi >= kj - BAND))
    bias_ref[...] = jnp.where(visible, 0.0, -jnp.inf)
    for head in range(qkv1_ref.shape[0]):
        _attn_one_head(qkv1_ref.at[head], qkv2_ref.at[head], qkv3_ref.at[head],
                       o_ref.at[:, head * HEAD_DIM:(head + 1) * HEAD_DIM],
                       num_ref, den_ref, max_ref, bias_ref)


def _attn_one_head(qkv1_ref, qkv2_ref, qkv3_ref, o_ref, num_ref, den_ref, max_ref, bias_ref):
    seq = qkv1_ref.shape[1]
    n_tiles = seq // BAND
    scale = HEAD_DIM ** -0.5

    def probabilities(q_ref, k_ref, n, has_prev):
        q = q_ref[n * BAND:(n + 1) * BAND, :]
        keys = slice((n - 1) * BAND, (n + 1) * BAND) if has_prev else slice(n * BAND, (n + 1) * BAND)
        bias = bias_ref[...] if has_prev else bias_ref[:, BAND:]
        s = lax.dot_general(q, k_ref[keys, :], (((1,), (1,)), ((), ())),
                            preferred_element_type=F32) * scale + bias
        m = s.max(-1, keepdims=True)
        return jnp.exp(s - m).astype(BF16), m, keys

    assert DILATED_GROUPS[0][1] == 1 and all(d % MAX_ROW_STRIDE == 0 for _, d in DILATED_GROUPS[1:])
    quarter = seq // MAX_ROW_STRIDE

    def result_rows(g, n):
        dilation = DILATED_GROUPS[g][1]
        if dilation == 1:
            return pl.ds(n * BAND, BAND)
        residue, t = divmod(n, n_tiles // dilation)
        step = dilation // MAX_ROW_STRIDE
        start = (residue % MAX_ROW_STRIDE) * quarter + step * t * BAND + residue // MAX_ROW_STRIDE
        return pl.ds(start, BAND, stride=step)

    def finish(g, v_ref, n, p, m, keys):
        rows = result_rows(g, n)
        v = v_ref[keys, :]
        acc = jnp.dot(p, jnp.concatenate([v, jnp.ones_like(v)], axis=1), preferred_element_type=F32)
        num_ref[g, rows, :] = acc[:, :HEAD_DIM]
        den_ref[g, rows, :] = acc[:, HEAD_DIM:]
        max_ref[g, rows, :] = jnp.broadcast_to(m, (BAND, HEAD_DIM))

    groups = [(r.at[0], r.at[1], r.at[2]) for r in (qkv1_ref, qkv2_ref, qkv3_ref)]
    in_flight = []
    for g, (q_ref, k_ref, v_ref) in enumerate(groups):
        tiles_per_residue = n_tiles // DILATED_GROUPS[g][1]
        for n in range(n_tiles):
            has_prev = n % tiles_per_residue > 0
            in_flight.append((g, v_ref, n) + probabilities(q_ref, k_ref, n, has_prev))
            if len(in_flight) > ATTN_TILES_IN_FLIGHT:
                finish(*in_flight.pop(0))
    for pending in in_flight:
        finish(*pending)

    chunks_per_quarter = quarter // BAND

    def mix(c, _):
        mixed = pl.ds(pl.multiple_of(c * BAND, BAND), BAND)
        natural = pl.ds(MAX_ROW_STRIDE * (c % chunks_per_quarter) * BAND + c // chunks_per_quarter,
                        BAND, stride=MAX_ROW_STRIDE)
        rows = [natural] + [mixed] * (N_GROUPS - 1)
        m = [max_ref[g, rows[g], :] for g in range(N_GROUPS)]
        top = jnp.maximum(jnp.maximum(m[0], m[1]), m[2])
        w = [jnp.exp(x - top) for x in m]
        num = sum(w[g] * num_ref[g, rows[g], :] for g in range(N_GROUPS))
        den = sum(w[g] * den_ref[g, rows[g], :] for g in range(N_GROUPS))
        max_ref[0, natural, :] = num / den
        return 0

    lax.fori_loop(0, n_tiles, mix, 0)
    o_ref[...] = max_ref[0].astype(o_ref.dtype)


def _attention(qkv, batch, seq):
    heads = ATTN_HEADS_PER_STEP
    qkv_spec = pl.BlockSpec((heads, None, 3, seq, HEAD_DIM), lambda b, h: (h, b, 0, 0, 0))
    scratch = 3 * _nbytes((N_GROUPS, seq, HEAD_DIM), F32)
    return pl.pallas_call(
        _attn_kernel,
        grid=(batch, HEADS_PER_GROUP // heads),
        in_specs=[qkv_spec] * N_GROUPS,
        out_specs=pl.BlockSpec((seq, heads * HEAD_DIM), lambda b, h: (b, h)),
        out_shape=jax.ShapeDtypeStruct((batch * seq, GROUP_WIDTH), BF16),
        scratch_shapes=[pltpu.VMEM((N_GROUPS, seq, HEAD_DIM), F32),
                        pltpu.VMEM((N_GROUPS, seq, HEAD_DIM), F32),
                        pltpu.VMEM((N_GROUPS, seq, HEAD_DIM), F32),
                        pltpu.VMEM((BAND, 2 * BAND), F32)],
        compiler_params=pltpu.CompilerParams(
            dimension_semantics=("parallel", "parallel"),
            vmem_limit_bytes=_vmem_limit(10 * heads * _nbytes((seq, HEAD_DIM), BF16),
                                         scratch + 4 * 1024 * 1024)),
        name="dilated_attention",
    )(*qkv)


def _merge_kernel(*refs, n_riders):
    attn_ref, conv_ref, x_ref, wao_ref, wco_ref, wga_ref, wgc_ref = refs[:7]
    o_ref = refs[7 + n_riders]
    _cast_riders(refs[7:7 + n_riders], refs[8 + n_riders:])
    x, attn, conv = x_ref[...], attn_ref[...], conv_ref[...]
    slab = o_ref.shape[1] // MERGE_SPLIT
    for k in range(MERGE_SPLIT):
        cols = slice(k * slab, (k + 1) * slab)
        a = jnp.dot(attn, wao_ref[:, cols], preferred_element_type=F32)
        ga = _sigmoid(jnp.dot(x, wga_ref[:, cols], preferred_element_type=F32))
        c = jnp.dot(conv, wco_ref[:, cols], preferred_element_type=F32)
        gc = _sigmoid(jnp.dot(x, wgc_ref[:, cols], preferred_element_type=F32))
        o_ref[:, cols] = (ga * a + gc * c).astype(o_ref.dtype)


def _gated_merge(attn, conv, xb, w_attn_o, w_conv_o, w_gates, make_riders, *, tm, tn):
    m = attn.shape[0]
    assert w_gates.shape == (D_MODEL, 2 * D_MODEL)
    col_blocks = D_MODEL // tn
    riders = make_riders(lambda i, j: i * col_blocks + j)
    rider_in_specs, rider_specs, rider_shapes, rider_bytes = _cast_rider_specs(riders)
    blocks = (_nbytes((tm, GROUP_WIDTH), BF16) + _nbytes((tm, CONV_WIDTH), BF16)
              + _nbytes((tm, D_MODEL), BF16) + _nbytes((tm, tn), BF16)
              + _nbytes((GROUP_WIDTH + CONV_WIDTH + 2 * D_MODEL, tn), BF16) + rider_bytes)
    return pl.pallas_call(
        functools.partial(_merge_kernel, n_riders=len(riders)),
        grid=(m // tm, col_blocks),
        in_specs=[pl.BlockSpec((tm, GROUP_WIDTH), lambda i, j: (i, 0)),
                  pl.BlockSpec((tm, CONV_WIDTH), lambda i, j: (i, 0)),
                  pl.BlockSpec((tm, D_MODEL), lambda i, j: (i, 0)),
                  pl.BlockSpec((GROUP_WIDTH, tn), lambda i, j: (0, j)),
                  pl.BlockSpec((CONV_WIDTH, tn), lambda i, j: (0, j)),
                  pl.BlockSpec((D_MODEL, tn), lambda i, j: (0, j)),
                  pl.BlockSpec((D_MODEL, tn), lambda i, j: (0, col_blocks + j))] + rider_in_specs,
        out_specs=[pl.BlockSpec((tm, tn), lambda i, j: (i, j))] + rider_specs,
        out_shape=[jax.ShapeDtypeStruct((m, D_MODEL), BF16)] + rider_shapes,
        compiler_params=pltpu.CompilerParams(
            dimension_semantics=("arbitrary", "arbitrary"),
            vmem_limit_bytes=_vmem_limit(blocks, 5 * _nbytes((tm, tn), F32))),
        name="gated_merge",
    )(attn, conv, xb, w_attn_o, w_conv_o, w_gates, w_gates, *[r.array for r in riders])


def _layer_norm(z, g, b):
    mu = jnp.mean(z, axis=-1, keepdims=True)
    zc = z - mu
    var = jnp.mean(zc * zc, axis=-1, keepdims=True)
    return zc * lax.rsqrt(var + LN_EPS) * g + b


def _out_ln_kernel(mi_ref, w_ref, x_ref, g_ref, b_ref, o_ref, ob_ref):
    part = mi_ref.shape[0] // OUT_LN_SPLIT
    for rows in (slice(k * part, (k + 1) * part) for k in range(OUT_LN_SPLIT)):
        y = jnp.dot(mi_ref[rows, :], w_ref[...], preferred_element_type=F32)
        h = _layer_norm(ALPHA * x_ref[rows, :] + y, g_ref[...], b_ref[...])
        o_ref[rows, :] = h
        ob_ref[rows, :] = h.astype(ob_ref.dtype)


def _out_proj_ln(merged, w_out, x, g, b, *, tm):
    m = merged.shape[0]
    blocks = (2 * _nbytes((tm, D_MODEL), BF16) + _nbytes((D_MODEL, D_MODEL), BF16)
              + 2 * _nbytes((tm, D_MODEL), F32))
    row = pl.BlockSpec((tm, D_MODEL), lambda i: (i, 0))
    vec = pl.BlockSpec((1, D_MODEL), lambda i: (0, 0))
    return pl.pallas_call(
        _out_ln_kernel,
        grid=(m // tm,),
        in_specs=[row, pl.BlockSpec((D_MODEL, D_MODEL), lambda i: (0, 0)), row, vec, vec],
        out_specs=[row, row],
        out_shape=[jax.ShapeDtypeStruct((m, D_MODEL), F32), jax.ShapeDtypeStruct((m, D_MODEL), BF16)],
        compiler_params=pltpu.CompilerParams(
            dimension_semantics=("parallel",),
            vmem_limit_bytes=_vmem_limit(blocks, 3 * _nbytes((tm, D_MODEL), F32))),
        name="out_proj_ln",
    )(merged, w_out, x, g, b)


def _ffn_kernel(hb_ref, h_hbm_ref, wg_ref, wu_ref, wd_ref, g_ref, b_ref, o_ref, res_ref, res_sem):
    i, f = pl.program_id(0), pl.program_id(1)
    tm = o_ref.shape[0]
    residual_copy = pltpu.make_async_copy(
        h_hbm_ref.at[pl.ds(pl.multiple_of(i * tm, tm), tm), :], res_ref, res_sem)

    @pl.when(f == 0)
    def _():
        residual_copy.start()
        o_ref[...] = jnp.zeros_like(o_ref)

    hb = hb_ref[...]
    gate = jnp.dot(hb, wg_ref[...], preferred_element_type=F32)
    up = jnp.dot(hb, wu_ref[...], preferred_element_type=F32)
    hidden = (gate * _sigmoid(gate) * up).astype(BF16)
    o_ref[...] += jnp.dot(hidden, wd_ref[...], preferred_element_type=F32)

    @pl.when(f == pl.num_programs(1) - 1)
    def _():
        residual_copy.wait()
        o_ref[...] = _layer_norm(ALPHA * res_ref[...] + o_ref[...], g_ref[...], b_ref[...])


def _ffn_ln(h, hb, w_gate, w_up, w_down, g, b, *, tm, tf):
    m = h.shape[0]
    blocks = (_nbytes((tm, D_MODEL), BF16) + _nbytes((tm, D_MODEL), F32)
              + 2 * _nbytes((D_MODEL, tf), BF16) + _nbytes((tf, D_MODEL), BF16))
    scratch = _nbytes((tm, D_MODEL), F32)
    row = pl.BlockSpec((tm, D_MODEL), lambda i, f: (i, 0))
    vec = pl.BlockSpec((1, D_MODEL), lambda i, f: (0, 0))
    return pl.pallas_call(
        _ffn_kernel,
        grid=(m // tm, FFN_HIDDEN // tf),
        in_specs=[row,
                  pl.BlockSpec(memory_space=pl.ANY),
                  pl.BlockSpec((D_MODEL, tf), lambda i, f: (0, f)),
                  pl.BlockSpec((D_MODEL, tf), lambda i, f: (0, f)),
                  pl.BlockSpec((tf, D_MODEL), lambda i, f: (f, 0)),
                  vec, vec],
        out_specs=row,
        out_shape=jax.ShapeDtypeStruct((m, D_MODEL), F32),
        scratch_shapes=[pltpu.VMEM((tm, D_MODEL), F32), pltpu.SemaphoreType.DMA],
        compiler_params=pltpu.CompilerParams(
            dimension_semantics=("arbitrary", "arbitrary"),
            vmem_limit_bytes=_vmem_limit(blocks, scratch + 6 * _nbytes((tm, tf), F32))),
        name="ffn_ln",
    )(hb, h, w_gate, w_up, w_down, g, b)


def kernel(x, w_in, conv_w, w_attn_o, w_conv_o, w_out, ln1_g, ln1_b,
           w_ffn_gate, w_ffn_up, w_ffn_down, ln2_g, ln2_b):
    batch, seq, d = x.shape
    assert d == D_MODEL and w_in.shape == (DEPTH, D_MODEL, IN_COLS)
    assert seq % PROJ_ROWS == 0 and all(PROJ_ROWS % (dil * 16) == 0 for _, dil in DILATED_GROUPS)
    assert all(seq % (dil * BAND) == 0 for _, dil in DILATED_GROUPS)
    h = x.reshape(batch * seq, d)
    for layer in range(DEPTH):
        assert N_GROUPS == 3
        w_in_l = w_in[layer]
        sections = ATTN_WIDTH // PROJ_COLS
        w_qkv0_b = _cast_group_weights(w_in_l, 0, rows=512)

        def group_weight_rider(group, row_tile_of, n_row_tiles):
            return _CastRider(w_in_l, (D_MODEL // n_row_tiles, PROJ_COLS),
                              lambda b, t, j: (row_tile_of(b, t, j), j * sections + group),
                              (D_MODEL, 3 * GROUP_WIDTH), lambda b, t, j: (row_tile_of(b, t, j), j))

        def conv_weight_rider(row_tile_of, n_row_tiles):
            half = 3 * CONV_WIDTH // 2
            return _CastRider(w_in_l, (D_MODEL // n_row_tiles, half),
                              lambda b, t, j: (row_tile_of(b, t, j), COL_U // half + jnp.minimum(j, 1)),
                              (D_MODEL, 3 * CONV_WIDTH), lambda b, t, j: (row_tile_of(b, t, j), jnp.minimum(j, 1)))

        def gate_weight_rider(tn):
            return _CastRider(w_in_l, (D_MODEL // batch, 2 * D_MODEL // (CONV_WIDTH // tn)),
                              lambda b, j: (b, COL_GA // (2 * D_MODEL // (CONV_WIDTH // tn)) + j),
                              (D_MODEL, 2 * D_MODEL), lambda b, j: (b, j))

        qkv0, hb, w_attn_o_b, w_conv_o_b, w_qkv1_b = _qkv_proj(
            h, w_qkv0_b, 0, batch, seq,
            lambda step_of, row_tile_of, n_row_tiles: [
                _row_block_rider(w_attn_o[layer], step_of), _row_block_rider(w_conv_o[layer], step_of),
                group_weight_rider(1, row_tile_of, n_row_tiles)])
        qkv1, w_qkv2_b, w_ucb_b = _qkv_proj(
            hb, w_qkv1_b, 1, batch, seq,
            lambda step_of, row_tile_of, n_row_tiles: [group_weight_rider(2, row_tile_of, n_row_tiles),
                                                       conv_weight_rider(row_tile_of, n_row_tiles)])
        qkv2, = _qkv_proj(hb, w_qkv2_b, 2, batch, seq, lambda step_of, row_tile_of, n_row_tiles: [])
        conv_tn = 512
        conv, w_down_b, w_out_b, w_gates_b = _conv_proj(
            hb, w_ucb_b, conv_w[layer], batch, seq,
            lambda step_of: [_row_block_rider(w_ffn_down[layer], step_of),
                             _row_block_rider(w_out[layer], step_of),
                             gate_weight_rider(conv_tn)],
            tn=conv_tn)
        attn = _attention([qkv0, qkv1, qkv2], batch, seq)
        merged, w_gate_b, w_up_b = _gated_merge(
            attn, conv, hb, w_attn_o_b, w_conv_o_b, w_gates_b,
            lambda step_of: [_row_block_rider(w_ffn_gate[layer], step_of),
                             _row_block_rider(w_ffn_up[layer], step_of)],
            tm=1024, tn=512)
        h, hb = _out_proj_ln(merged, w_out_b, h, ln1_g[layer][None, :], ln1_b[layer][None, :], tm=512)
        h = _ffn_ln(h, hb, w_gate_b, w_up_b, w_down_b, ln2_g[layer][None, :], ln2_b[layer][None, :],
                    tm=1024, tf=512)
    return h.reshape(batch, seq, d)
```

```python
import functools
import math
from typing import Callable, NamedTuple

import jax
import jax.numpy as jnp
from jax import lax
from jax.experimental import pallas as pl
from jax.experimental.pallas import tpu as pltpu

D_MODEL = 2048
HEAD_DIM = 128
HEADS_PER_GROUP = 8
DILATED_GROUPS = ((128, 1), (512, 4), (2048, 16))
N_GROUPS = len(DILATED_GROUPS)
GROUP_WIDTH = HEADS_PER_GROUP * HEAD_DIM
ATTN_WIDTH = N_GROUPS * GROUP_WIDTH
CONV_WIDTH = D_MODEL
CONV_K = 3
FFN_HIDDEN = 5632
DEPTH = 1
ALPHA = (2 * DEPTH) ** 0.25
LN_EPS = 1e-5

COL_U = 3 * ATTN_WIDTH
COL_C = COL_U + CONV_WIDTH
COL_B = COL_C + CONV_WIDTH
COL_GA = COL_B + CONV_WIDTH
COL_GC = COL_GA + D_MODEL
IN_COLS = COL_GC + D_MODEL

V7X_VMEM_BYTES = 64 * 1024 * 1024
BAND = 128
ATTN_TILES_IN_FLIGHT = 4
MAX_ROW_STRIDE = 4
CAST_RIDER_STEPS = 32
PROJ_ROWS = 1024
PROJ_COLS = 1024
CONV_ROW_PARTS = 2
FFN_ROW_PARTS = 2
OUT_LN_SPLIT = 4
MERGE_SPLIT = 2

F32 = jnp.float32
BF16 = jnp.bfloat16


def _vmem_limit(pipelined_block_bytes, resident_bytes):
    need = 2 * pipelined_block_bytes + resident_bytes
    return min(int(need * 1.25), V7X_VMEM_BYTES - 4 * 1024 * 1024)


def _nbytes(shape, dtype):
    return math.prod(shape) * jnp.dtype(dtype).itemsize


def _sigmoid(x):
    return 0.5 * (jnp.tanh(0.5 * x) + 1.0)


def _store_heads_by_residue(y, chunk, o_ref, y_ref, y2_ref, dilation):
    if dilation == 1:
        for head in range(HEADS_PER_GROUP):
            o_ref[head, 0] = y[:, head * HEAD_DIM:(head + 1) * HEAD_DIM].astype(o_ref.dtype)
        return
    f1 = min(dilation, MAX_ROW_STRIDE)
    f2 = dilation // f1
    assert f1 * f2 == dilation and f2 <= MAX_ROW_STRIDE
    rows = y.shape[0]
    per_residue = rows // dilation
    dst = slice(chunk * per_residue, (chunk + 1) * per_residue)
    for head in range(HEADS_PER_GROUP):
        src = y_ref.at[head]
        src[...] = y[:, head * HEAD_DIM:(head + 1) * HEAD_DIM]
        if f2 > 1:
            for r0 in range(f1):
                y2_ref[head, r0 * (rows // f1):(r0 + 1) * (rows // f1), :] = src[pl.ds(r0, rows // f1, stride=f1), :]
            src = y2_ref.at[head]
        for r0 in range(f1):
            for r1 in range(f2):
                start = r0 * (rows // f1) + r1 if f2 > 1 else r0
                stride = f2 if f2 > 1 else f1
                o_ref[head, r1 * f1 + r0, dst, :] = (
                    src[pl.ds(start, per_residue, stride=stride), :].astype(o_ref.dtype))


def _cast_riders(rider_refs, rider_out_refs):
    for src, dst in zip(rider_refs, rider_out_refs, strict=True):
        dst[...] = src[...].astype(dst.dtype)


def _qkv_proj_first_kernel(*refs, n_riders):
    x_ref, w_ref = refs[:2]
    rider_refs = refs[2:2 + n_riders]
    o_ref, xb_ref = refs[2 + n_riders:4 + n_riders]
    _cast_riders(rider_refs, refs[4 + n_riders:])

    @pl.when(pl.program_id(2) == 0)
    def _():
        xb_ref[...] = x_ref[...].astype(BF16)

    y = jnp.dot(xb_ref[...], w_ref[...], preferred_element_type=F32)
    _store_heads_by_residue(y, 0, o_ref, None, None, 1)


def _qkv_proj_kernel(*refs, n_riders, dilation):
    xb_ref, w_ref = refs[:2]
    rider_refs = refs[2:2 + n_riders]
    o_ref = refs[2 + n_riders]
    rider_out_refs = refs[3 + n_riders:3 + 2 * n_riders]
    y_ref, *second_pass = refs[3 + 2 * n_riders:]
    y2_ref = second_pass[0] if second_pass else None
    _cast_riders(rider_refs, rider_out_refs)
    rows = y_ref.shape[1]
    for chunk in range(xb_ref.shape[0] // rows):
        y = jnp.dot(xb_ref[chunk * rows:(chunk + 1) * rows, :], w_ref[...], preferred_element_type=F32)
        _store_heads_by_residue(y, chunk, o_ref, y_ref, y2_ref, dilation)


class _CastRider(NamedTuple):
    array: jax.Array
    block: tuple
    in_index: Callable
    out_shape: tuple
    out_index: Callable


def _cast_kernel(w_ref, o_ref):
    o_ref[...] = w_ref[...].astype(o_ref.dtype)


def _cast_group_weights(w_in, group, *, rows):
    sections = ATTN_WIDTH // GROUP_WIDTH
    blocks = _nbytes((rows, GROUP_WIDTH), F32) + _nbytes((rows, GROUP_WIDTH), BF16)
    return pl.pallas_call(
        _cast_kernel,
        grid=(3, D_MODEL // rows),
        in_specs=[pl.BlockSpec((rows, GROUP_WIDTH), lambda j, i: (i, j * sections + group))],
        out_specs=pl.BlockSpec((rows, GROUP_WIDTH), lambda j, i: (i, j)),
        out_shape=jax.ShapeDtypeStruct((D_MODEL, 3 * GROUP_WIDTH), BF16),
        compiler_params=pltpu.CompilerParams(
            dimension_semantics=("parallel", "parallel"),
            vmem_limit_bytes=_vmem_limit(blocks, 0)),
        name="cast_group_weights",
    )(w_in)


def _row_block_rider(w, step_of):
    rows = w.shape[0] // CAST_RIDER_STEPS
    assert rows * CAST_RIDER_STEPS == w.shape[0] and rows % 16 == 0

    def index(*idx):
        return jnp.minimum(step_of(*idx), CAST_RIDER_STEPS - 1), 0

    return _CastRider(w, (rows, w.shape[1]), index, w.shape, index)


def _cast_rider_specs(riders):
    in_specs = [pl.BlockSpec(r.block, r.in_index) for r in riders]
    out_specs = [pl.BlockSpec(r.block, r.out_index) for r in riders]
    shapes = [jax.ShapeDtypeStruct(r.out_shape, BF16) for r in riders]
    return in_specs, out_specs, shapes, sum(_nbytes(r.block, F32) * 3 // 2 for r in riders)


def _qkv_proj(x, w_qkv, group, batch, seq, make_riders):
    dilation = DILATED_GROUPS[group][1]
    first = x.dtype == F32
    assert first == (group == 0) and dilation == (1 if first else dilation)
    length = seq // dilation
    rows = PROJ_ROWS if first else min(seq, 2 * PROJ_ROWS)
    tiles = seq // rows
    per_residue = rows // dilation
    assert PROJ_COLS == GROUP_WIDTH and w_qkv.shape == (D_MODEL, 3 * GROUP_WIDTH) and rows * tiles == seq
    row_spec = pl.BlockSpec((rows, D_MODEL), lambda b, t, j: (b * tiles + t, 0))
    qkv_spec = pl.BlockSpec((HEADS_PER_GROUP, None, None, dilation, per_residue, HEAD_DIM),
                            lambda b, t, j: (0, b, j, 0, t, 0))
    qkv_shape = jax.ShapeDtypeStruct((HEADS_PER_GROUP, batch, 3, dilation, length, HEAD_DIM), BF16)
    blocks = (_nbytes((rows, D_MODEL), x.dtype) + _nbytes((D_MODEL, PROJ_COLS), BF16)
              + _nbytes((rows, PROJ_COLS), BF16) + (_nbytes((rows, D_MODEL), BF16) if first else 0))
    passes = 0 if first else (1 if dilation <= MAX_ROW_STRIDE else 2)
    regroup = [pltpu.VMEM((HEADS_PER_GROUP, PROJ_ROWS, HEAD_DIM), F32)] * passes
    w_spec = pl.BlockSpec((D_MODEL, PROJ_COLS), lambda b, t, j: (0, j))
    riders = make_riders(lambda b, t, j: (b * tiles + t) * 3 + j, lambda b, t, j: b * tiles + t,
                         batch * tiles)
    rider_in_specs, rider_specs, rider_shapes, rider_bytes = _cast_rider_specs(riders)
    own_out_specs, own_out_shapes = [qkv_spec], [qkv_shape]
    if first:
        own_out_specs.append(row_spec)
        own_out_shapes.append(jax.ShapeDtypeStruct(x.shape, BF16))
        body = functools.partial(_qkv_proj_first_kernel, n_riders=len(riders))
    else:
        body = functools.partial(_qkv_proj_kernel, n_riders=len(riders), dilation=dilation)
    out = pl.pallas_call(
        body,
        grid=(batch, tiles, 3),
        in_specs=[row_spec, w_spec] + rider_in_specs,
        out_specs=own_out_specs + rider_specs,
        out_shape=own_out_shapes + rider_shapes,
        scratch_shapes=regroup,
        compiler_params=pltpu.CompilerParams(
            dimension_semantics=("arbitrary", "arbitrary", "arbitrary"),
            vmem_limit_bytes=_vmem_limit(blocks + rider_bytes,
                                         (0 if first else 2 * _nbytes((PROJ_ROWS, PROJ_COLS), F32))
                                         + 2 * _nbytes((rows, PROJ_COLS), F32))),
        name=f"qkv_proj_dilation{dilation}",
    )(x, w_qkv, *[r.array for r in riders])
    return (out[0].reshape(HEADS_PER_GROUP, batch, 3, seq, HEAD_DIM),) + tuple(out[1:])


def _conv_proj_kernel(*refs, n_riders):
    x_ref, wu_ref, wc_ref, wb_ref, cw_ref = refs[:5]
    o_ref = refs[5 + n_riders]
    _cast_riders(refs[5:5 + n_riders], refs[6 + n_riders:])
    cw = cw_ref[...]
    part = x_ref.shape[0] // CONV_ROW_PARTS
    row = lax.broadcasted_iota(jnp.int32, (part, o_ref.shape[1]), 0)
    before = None
    for k in range(CONV_ROW_PARTS):
        rows = slice(k * part, (k + 1) * part)
        x = x_ref[rows, :]
        z = (jnp.dot(x, wc_ref[...], preferred_element_type=F32)
             * jnp.dot(x, wu_ref[...], preferred_element_type=F32))
        y = cw[0:1, :] * z
        for tap in range(1, CONV_K):
            lead = 0.0 if before is None else pltpu.roll(before, tap, axis=0)
            shifted = jnp.where(row >= tap, pltpu.roll(z, tap, axis=0), lead)
            y = y + cw[tap:tap + 1, :] * shifted
        o_ref[rows, :] = (jnp.dot(x, wb_ref[...], preferred_element_type=F32) * y).astype(o_ref.dtype)
        before = z


def _conv_proj(xb, w_ucb, conv_w, batch, seq, make_riders, *, tn):
    def w_spec(section):
        return pl.BlockSpec((D_MODEL, tn), lambda b, j: (0, section * (CONV_WIDTH // tn) + j))

    col_blocks = CONV_WIDTH // tn
    assert w_ucb.shape == (D_MODEL, 3 * CONV_WIDTH)
    riders = make_riders(lambda b, j: b * col_blocks + j)
    rider_in_specs, rider_specs, rider_shapes, rider_bytes = _cast_rider_specs(riders)
    blocks = (_nbytes((seq, D_MODEL), BF16) + 3 * _nbytes((D_MODEL, tn), BF16)
              + _nbytes((CONV_K, tn), F32) + _nbytes((seq, tn), BF16) + rider_bytes)
    return pl.pallas_call(
        functools.partial(_conv_proj_kernel, n_riders=len(riders)),
        grid=(batch, col_blocks),
        in_specs=[pl.BlockSpec((seq, D_MODEL), lambda b, j: (b, 0)),
                  w_spec(0), w_spec(1), w_spec(2),
                  pl.BlockSpec((CONV_K, tn), lambda b, j: (0, j))] + rider_in_specs,
        out_specs=[pl.BlockSpec((seq, tn), lambda b, j: (b, j))] + rider_specs,
        out_shape=[jax.ShapeDtypeStruct((batch * seq, CONV_WIDTH), BF16)] + rider_shapes,
        compiler_params=pltpu.CompilerParams(
            dimension_semantics=("arbitrary", "arbitrary"),
            vmem_limit_bytes=_vmem_limit(blocks, 5 * _nbytes((seq, tn), F32))),
        name="conv_proj",
    )(xb, w_ucb, w_ucb, w_ucb, conv_w, *[r.array for r in riders])


def _attn_kernel(qkv1_ref, qkv2_ref, qkv3_ref, o_ref, num_ref, den_ref, max_ref, bias_ref):
    seq = qkv1_ref.shape[1]
    n_tiles = seq // BAND
    scale = HEAD_DIM ** -0.5
    qi = lax.broadcasted_iota(jnp.int32, (BAND, 2 * BAND), 0)
    kj = lax.broadcasted_iota(jnp.int32, (BAND, 2 * BAND), 1)
    visible = ((kj < BAND) & (qi <= kj)) | ((kj >= BAND) & (qi >= kj - BAND))
    bias_ref[...] = jnp.where(visible, 0.0, -jnp.inf)

    def probabilities(q_ref, k_ref, n, has_prev):
        q = q_ref[n * BAND:(n + 1) * BAND, :]
        keys = slice((n - 1) * BAND, (n + 1) * BAND) if has_prev else slice(n * BAND, (n + 1) * BAND)
        bias = bias_ref[...] if has_prev else bias_ref[:, BAND:]
        s = lax.dot_general(q, k_ref[keys, :], (((1,), (1,)), ((), ())),
                            preferred_element_type=F32) * scale + bias
        m = s.max(-1, keepdims=True)
        return jnp.exp(s - m).astype(BF16), m, keys

    assert DILATED_GROUPS[0][1] == 1 and all(d % MAX_ROW_STRIDE == 0 for _, d in DILATED_GROUPS[1:])
    quarter = seq // MAX_ROW_STRIDE

    def result_rows(g, n):
        dilation = DILATED_GROUPS[g][1]
        if dilation == 1:
            return pl.ds(n * BAND, BAND)
        residue, t = divmod(n, n_tiles // dilation)
        step = dilation // MAX_ROW_STRIDE
        start = (residue % MAX_ROW_STRIDE) * quarter + step * t * BAND + residue // MAX_ROW_STRIDE
        return pl.ds(start, BAND, stride=step)

    def finish(g, v_ref, n, p, m, keys):
        rows = result_rows(g, n)
        v = v_ref[keys, :]
        acc = jnp.dot(p, jnp.concatenate([v, jnp.ones_like(v)], axis=1), preferred_element_type=F32)
        num_ref[g, rows, :] = acc[:, :HEAD_DIM]
        den_ref[g, rows, :] = acc[:, HEAD_DIM:]
        max_ref[g, rows, :] = jnp.broadcast_to(m, (BAND, HEAD_DIM))

    groups = [(r.at[0], r.at[1], r.at[2]) for r in (qkv1_ref, qkv2_ref, qkv3_ref)]
    in_flight = []
    for g, (q_ref, k_ref, v_ref) in enumerate(groups):
        tiles_per_residue = n_tiles // DILATED_GROUPS[g][1]
        for n in range(n_tiles):
            has_prev = n % tiles_per_residue > 0
            in_flight.append((g, v_ref, n) + probabilities(q_ref, k_ref, n, has_prev))
            if len(in_flight) > ATTN_TILES_IN_FLIGHT:
                finish(*in_flight.pop(0))
    for pending in in_flight:
        finish(*pending)

    chunks_per_quarter = quarter // BAND

    def mix(c, _):
        mixed = pl.ds(pl.multiple_of(c * BAND, BAND), BAND)
        natural = pl.ds(MAX_ROW_STRIDE * (c % chunks_per_quarter) * BAND + c // chunks_per_quarter,
                        BAND, stride=MAX_ROW_STRIDE)
        rows = [natural] + [mixed] * (N_GROUPS - 1)
        m = [max_ref[g, rows[g], :] for g in range(N_GROUPS)]
        top = jnp.maximum(jnp.maximum(m[0], m[1]), m[2])
        w = [jnp.exp(x - top) for x in m]
        num = sum(w[g] * num_ref[g, rows[g], :] for g in range(N_GROUPS))
        den = sum(w[g] * den_ref[g, rows[g], :] for g in range(N_GROUPS))
        max_ref[0, natural, :] = num / den
        return 0

    lax.fori_loop(0, n_tiles, mix, 0)
    o_ref[...] = max_ref[0].astype(o_ref.dtype)


def _attention(qkv, batch, seq):
    heads = HEADS_PER_GROUP
    qkv_spec = pl.BlockSpec((None, None, 3, seq, HEAD_DIM), lambda b, h: (h, b, 0, 0, 0))
    scratch = 3 * _nbytes((N_GROUPS, seq, HEAD_DIM), F32)
    return pl.pallas_call(
        _attn_kernel,
        grid=(batch, heads),
        in_specs=[qkv_spec] * N_GROUPS,
        out_specs=pl.BlockSpec((seq, HEAD_DIM), lambda b, h: (b, h)),
        out_shape=jax.ShapeDtypeStruct((batch * seq, GROUP_WIDTH), BF16),
        scratch_shapes=[pltpu.VMEM((N_GROUPS, seq, HEAD_DIM), F32),
                        pltpu.VMEM((N_GROUPS, seq, HEAD_DIM), F32),
                        pltpu.VMEM((N_GROUPS, seq, HEAD_DIM), F32),
                        pltpu.VMEM((BAND, 2 * BAND), F32)],
        compiler_params=pltpu.CompilerParams(
            dimension_semantics=("parallel", "parallel"),
            vmem_limit_bytes=_vmem_limit(10 * _nbytes((seq, HEAD_DIM), BF16), scratch + 4 * 1024 * 1024)),
        name="dilated_attention",
    )(*qkv)


def _merge_kernel(*refs, n_riders):
    attn_ref, conv_ref, x_ref, wao_ref, wco_ref, wga_ref, wgc_ref = refs[:7]
    o_ref = refs[7 + n_riders]
    _cast_riders(refs[7:7 + n_riders], refs[8 + n_riders:])
    x, attn, conv = x_ref[...], attn_ref[...], conv_ref[...]
    slab = o_ref.shape[1] // MERGE_SPLIT
    for k in range(MERGE_SPLIT):
        cols = slice(k * slab, (k + 1) * slab)
        a = jnp.dot(attn, wao_ref[:, cols], preferred_element_type=F32)
        ga = _sigmoid(jnp.dot(x, wga_ref[:, cols], preferred_element_type=F32))
        c = jnp.dot(conv, wco_ref[:, cols], preferred_element_type=F32)
        gc = _sigmoid(jnp.dot(x, wgc_ref[:, cols], preferred_element_type=F32))
        o_ref[:, cols] = (ga * a + gc * c).astype(o_ref.dtype)


def _gated_merge(attn, conv, xb, w_attn_o, w_conv_o, w_gates, make_riders, *, tm, tn):
    m = attn.shape[0]
    assert w_gates.shape == (D_MODEL, 2 * D_MODEL)
    col_blocks = D_MODEL // tn
    riders = make_riders(lambda i, j: i * col_blocks + j)
    rider_in_specs, rider_specs, rider_shapes, rider_bytes = _cast_rider_specs(riders)
    blocks = (_nbytes((tm, GROUP_WIDTH), BF16) + _nbytes((tm, CONV_WIDTH), BF16)
              + _nbytes((tm, D_MODEL), BF16) + _nbytes((tm, tn), BF16)
              + _nbytes((GROUP_WIDTH + CONV_WIDTH + 2 * D_MODEL, tn), BF16) + rider_bytes)
    return pl.pallas_call(
        functools.partial(_merge_kernel, n_riders=len(riders)),
        grid=(m // tm, col_blocks),
        in_specs=[pl.BlockSpec((tm, GROUP_WIDTH), lambda i, j: (i, 0)),
                  pl.BlockSpec((tm, CONV_WIDTH), lambda i, j: (i, 0)),
                  pl.BlockSpec((tm, D_MODEL), lambda i, j: (i, 0)),
                  pl.BlockSpec((GROUP_WIDTH, tn), lambda i, j: (0, j)),
                  pl.BlockSpec((CONV_WIDTH, tn), lambda i, j: (0, j)),
                  pl.BlockSpec((D_MODEL, tn), lambda i, j: (0, j)),
                  pl.BlockSpec((D_MODEL, tn), lambda i, j: (0, col_blocks + j))] + rider_in_specs,
        out_specs=[pl.BlockSpec((tm, tn), lambda i, j: (i, j))] + rider_specs,
        out_shape=[jax.ShapeDtypeStruct((m, D_MODEL), BF16)] + rider_shapes,
        compiler_params=pltpu.CompilerParams(
            dimension_semantics=("arbitrary", "arbitrary"),
            vmem_limit_bytes=_vmem_limit(blocks, 5 * _nbytes((tm, tn), F32))),
        name="gated_merge",
    )(attn, conv, xb, w_attn_o, w_conv_o, w_gates, w_gates, *[r.array for r in riders])


def _layer_norm(z, g, b):
    mu = jnp.mean(z, axis=-1, keepdims=True)
    zc = z - mu
    var = jnp.mean(zc * zc, axis=-1, keepdims=True)
    return zc * lax.rsqrt(var + LN_EPS) * g + b


def _out_ln_kernel(mi_ref, w_ref, x_ref, g_ref, b_ref, o_ref, ob_ref):
    part = mi_ref.shape[0] // OUT_LN_SPLIT
    for rows in (slice(k * part, (k + 1) * part) for k in range(OUT_LN_SPLIT)):
        y = jnp.dot(mi_ref[rows, :], w_ref[...], preferred_element_type=F32)
        h = _layer_norm(ALPHA * x_ref[rows, :] + y, g_ref[...], b_ref[...])
        o_ref[rows, :] = h
        ob_ref[rows, :] = h.astype(ob_ref.dtype)


def _out_proj_ln(merged, w_out, x, g, b, *, tm):
    m = merged.shape[0]
    blocks = (2 * _nbytes((tm, D_MODEL), BF16) + _nbytes((D_MODEL, D_MODEL), BF16)
              + 2 * _nbytes((tm, D_MODEL), F32))
    row = pl.BlockSpec((tm, D_MODEL), lambda i: (i, 0))
    vec = pl.BlockSpec((1, D_MODEL), lambda i: (0, 0))
    return pl.pallas_call(
        _out_ln_kernel,
        grid=(m // tm,),
        in_specs=[row, pl.BlockSpec((D_MODEL, D_MODEL), lambda i: (0, 0)), row, vec, vec],
        out_specs=[row, row],
        out_shape=[jax.ShapeDtypeStruct((m, D_MODEL), F32), jax.ShapeDtypeStruct((m, D_MODEL), BF16)],
        compiler_params=pltpu.CompilerParams(
            dimension_semantics=("parallel",),
            vmem_limit_bytes=_vmem_limit(blocks, 3 * _nbytes((tm, D_MODEL), F32))),
        name="out_proj_ln",
    )(merged, w_out, x, g, b)


def _ffn_kernel(hb_ref, h_hbm_ref, wg_ref, wu_ref, wd_ref, g_ref, b_ref, o_ref, res_ref, res_sem):
    i, f = pl.program_id(0), pl.program_id(1)
    tm = o_ref.shape[0]
    residual_copy = pltpu.make_async_copy(
        h_hbm_ref.at[pl.ds(pl.multiple_of(i * tm, tm), tm), :], res_ref, res_sem)

    @pl.when(f == 0)
    def _():
        residual_copy.start()
        o_ref[...] = jnp.zeros_like(o_ref)

    part = tm // FFN_ROW_PARTS
    for k in range(FFN_ROW_PARTS):
        rows = slice(k * part, (k + 1) * part)
        hb = hb_ref[rows, :]
        gate = jnp.dot(hb, wg_ref[...], preferred_element_type=F32)
        up = jnp.dot(hb, wu_ref[...], preferred_element_type=F32)
        hidden = (gate * _sigmoid(gate) * up).astype(BF16)
        o_ref[rows, :] += jnp.dot(hidden, wd_ref[...], preferred_element_type=F32)

    @pl.when(f == pl.num_programs(1) - 1)
    def _():
        residual_copy.wait()
        o_ref[...] = _layer_norm(ALPHA * res_ref[...] + o_ref[...], g_ref[...], b_ref[...])


def _ffn_ln(h, hb, w_gate, w_up, w_down, g, b, *, tm, tf):
    m = h.shape[0]
    blocks = (_nbytes((tm, D_MODEL), BF16) + _nbytes((tm, D_MODEL), F32)
              + 2 * _nbytes((D_MODEL, tf), BF16) + _nbytes((tf, D_MODEL), BF16))
    scratch = _nbytes((tm, D_MODEL), F32)
    row = pl.BlockSpec((tm, D_MODEL), lambda i, f: (i, 0))
    vec = pl.BlockSpec((1, D_MODEL), lambda i, f: (0, 0))
    return pl.pallas_call(
        _ffn_kernel,
        grid=(m // tm, FFN_HIDDEN // tf),
        in_specs=[row,
                  pl.BlockSpec(memory_space=pl.ANY),
                  pl.BlockSpec((D_MODEL, tf), lambda i, f: (0, f)),
                  pl.BlockSpec((D_MODEL, tf), lambda i, f: (0, f)),
                  pl.BlockSpec((tf, D_MODEL), lambda i, f: (f, 0)),
                  vec, vec],
        out_specs=row,
        out_shape=jax.ShapeDtypeStruct((m, D_MODEL), F32),
        scratch_shapes=[pltpu.VMEM((tm, D_MODEL), F32), pltpu.SemaphoreType.DMA],
        compiler_params=pltpu.CompilerParams(
            dimension_semantics=("arbitrary", "arbitrary"),
            vmem_limit_bytes=_vmem_limit(blocks, scratch + 6 * _nbytes((tm, tf), F32))),
        name="ffn_ln",
    )(hb, h, w_gate, w_up, w_down, g, b)


def kernel(x, w_in, conv_w, w_attn_o, w_conv_o, w_out, ln1_g, ln1_b,
           w_ffn_gate, w_ffn_up, w_ffn_down, ln2_g, ln2_b):
    batch, seq, d = x.shape
    assert d == D_MODEL and w_in.shape == (DEPTH, D_MODEL, IN_COLS)
    assert seq % PROJ_ROWS == 0 and all(PROJ_ROWS % (dil * 16) == 0 for _, dil in DILATED_GROUPS)
    assert all(seq % (dil * BAND) == 0 for _, dil in DILATED_GROUPS)
    h = x.reshape(batch * seq, d)
    for layer in range(DEPTH):
        assert N_GROUPS == 3
        w_in_l = w_in[layer]
        sections = ATTN_WIDTH // PROJ_COLS
        w_qkv0_b = _cast_group_weights(w_in_l, 0, rows=512)

        def group_weight_rider(group, row_tile_of, n_row_tiles):
            return _CastRider(w_in_l, (D_MODEL // n_row_tiles, PROJ_COLS),
                              lambda b, t, j: (row_tile_of(b, t, j), j * sections + group),
                              (D_MODEL, 3 * GROUP_WIDTH), lambda b, t, j: (row_tile_of(b, t, j), j))

        def conv_weight_rider(row_tile_of, n_row_tiles):
            half = 3 * CONV_WIDTH // 2
            return _CastRider(w_in_l, (D_MODEL // n_row_tiles, half),
                              lambda b, t, j: (row_tile_of(b, t, j), COL_U // half + jnp.minimum(j, 1)),
                              (D_MODEL, 3 * CONV_WIDTH), lambda b, t, j: (row_tile_of(b, t, j), jnp.minimum(j, 1)))

        def gate_weight_rider(tn):
            return _CastRider(w_in_l, (D_MODEL // batch, 2 * D_MODEL // (CONV_WIDTH // tn)),
                              lambda b, j: (b, COL_GA // (2 * D_MODEL // (CONV_WIDTH // tn)) + j),
                              (D_MODEL, 2 * D_MODEL), lambda b, j: (b, j))

        qkv0, hb, w_attn_o_b, w_conv_o_b, w_qkv1_b = _qkv_proj(
            h, w_qkv0_b, 0, batch, seq,
            lambda step_of, row_tile_of, n_row_tiles: [
                _row_block_rider(w_attn_o[layer], step_of), _row_block_rider(w_conv_o[layer], step_of),
                group_weight_rider(1, row_tile_of, n_row_tiles)])
        qkv1, w_qkv2_b, w_ucb_b = _qkv_proj(
            hb, w_qkv1_b, 1, batch, seq,
            lambda step_of, row_tile_of, n_row_tiles: [group_weight_rider(2, row_tile_of, n_row_tiles),
                                                       conv_weight_rider(row_tile_of, n_row_tiles)])
        qkv2, = _qkv_proj(hb, w_qkv2_b, 2, batch, seq, lambda step_of, row_tile_of, n_row_tiles: [])
        conv_tn = 512
        conv, w_down_b, w_out_b, w_gates_b = _conv_proj(
            hb, w_ucb_b, conv_w[layer], batch, seq,
            lambda step_of: [_row_block_rider(w_ffn_down[layer], step_of),
                             _row_block_rider(w_out[layer], step_of),
                             gate_weight_rider(conv_tn)],
            tn=conv_tn)
        attn = _attention([qkv0, qkv1, qkv2], batch, seq)
        merged, w_gate_b, w_up_b = _gated_merge(
            attn, conv, hb, w_attn_o_b, w_conv_o_b, w_gates_b,
            lambda step_of: [_row_block_rider(w_ffn_gate[layer], step_of),
                             _row_block_rider(w_ffn_up[layer], step_of)],
            tm=1024, tn=512)
        h, hb = _out_proj_ln(merged, w_out_b, h, ln1_g[layer][None, :], ln1_b[layer][None, :], tm=512)
        h = _ffn_ln(h, hb, w_gate_b, w_up_b, w_down_b, ln2_g[layer][None, :], ln2_b[layer][None, :],
                    tm=1024, tf=512)
    return h.reshape(batch, seq, d)
```

```python
import functools
import math
from typing import Callable, NamedTuple

import jax
import jax.numpy as jnp
from jax import lax
from jax.experimental import pallas as pl
from jax.experimental.pallas import tpu as pltpu

D_MODEL = 2048
HEAD_DIM = 128
HEADS_PER_GROUP = 8
DILATED_GROUPS = ((128, 1), (512, 4), (2048, 16))
N_GROUPS = len(DILATED_GROUPS)
GROUP_WIDTH = HEADS_PER_GROUP * HEAD_DIM
ATTN_WIDTH = N_GROUPS * GROUP_WIDTH
CONV_WIDTH = D_MODEL
CONV_K = 3
FFN_HIDDEN = 5632
DEPTH = 1
ALPHA = (2 * DEPTH) ** 0.25
LN_EPS = 1e-5

COL_U = 3 * ATTN_WIDTH
COL_C = COL_U + CONV_WIDTH
COL_B = COL_C + CONV_WIDTH
COL_GA = COL_B + CONV_WIDTH
COL_GC = COL_GA + D_MODEL
IN_COLS = COL_GC + D_MODEL

V7X_VMEM_BYTES = 64 * 1024 * 1024
BAND = 128
ATTN_TILES_IN_FLIGHT = 4
MAX_ROW_STRIDE = 4
CAST_RIDER_STEPS = 32
PROJ_ROWS = 1024
PROJ_COLS = 1024
CONV_ROW_PARTS = 2
FFN_ROW_PARTS = 2
OUT_LN_SPLIT = 4
MERGE_ROW_PARTS = 2

F32 = jnp.float32
BF16 = jnp.bfloat16


def _vmem_limit(pipelined_block_bytes, resident_bytes):
    need = 2 * pipelined_block_bytes + resident_bytes
    return min(int(need * 1.25), V7X_VMEM_BYTES - 4 * 1024 * 1024)


def _nbytes(shape, dtype):
    return math.prod(shape) * jnp.dtype(dtype).itemsize


def _sigmoid(x):
    return 0.5 * (jnp.tanh(0.5 * x) + 1.0)


def _store_heads_by_residue(y, chunk, o_ref, y_ref, y2_ref, dilation):
    if dilation == 1:
        for head in range(HEADS_PER_GROUP):
            o_ref[head, 0] = y[:, head * HEAD_DIM:(head + 1) * HEAD_DIM].astype(o_ref.dtype)
        return
    f1 = min(dilation, MAX_ROW_STRIDE)
    f2 = dilation // f1
    assert f1 * f2 == dilation and f2 <= MAX_ROW_STRIDE
    rows = y.shape[0]
    per_residue = rows // dilation
    dst = slice(chunk * per_residue, (chunk + 1) * per_residue)
    for head in range(HEADS_PER_GROUP):
        src = y_ref.at[head]
        src[...] = y[:, head * HEAD_DIM:(head + 1) * HEAD_DIM]
        if f2 > 1:
            for r0 in range(f1):
                y2_ref[head, r0 * (rows // f1):(r0 + 1) * (rows // f1), :] = src[pl.ds(r0, rows // f1, stride=f1), :]
            src = y2_ref.at[head]
        for r0 in range(f1):
            for r1 in range(f2):
                start = r0 * (rows // f1) + r1 if f2 > 1 else r0
                stride = f2 if f2 > 1 else f1
                o_ref[head, r1 * f1 + r0, dst, :] = (
                    src[pl.ds(start, per_residue, stride=stride), :].astype(o_ref.dtype))


def _cast_riders(rider_refs, rider_out_refs):
    for src, dst in zip(rider_refs, rider_out_refs, strict=True):
        dst[...] = src[...].astype(dst.dtype)


def _qkv_proj_first_kernel(*refs, n_riders):
    x_ref, w_ref = refs[:2]
    rider_refs = refs[2:2 + n_riders]
    o_ref, xb_ref = refs[2 + n_riders:4 + n_riders]
    _cast_riders(rider_refs, refs[4 + n_riders:])

    @pl.when(pl.program_id(2) == 0)
    def _():
        xb_ref[...] = x_ref[...].astype(BF16)

    y = jnp.dot(xb_ref[...], w_ref[...], preferred_element_type=F32)
    _store_heads_by_residue(y, 0, o_ref, None, None, 1)


def _qkv_proj_kernel(*refs, n_riders, dilation):
    xb_ref, w_ref = refs[:2]
    rider_refs = refs[2:2 + n_riders]
    o_ref = refs[2 + n_riders]
    rider_out_refs = refs[3 + n_riders:3 + 2 * n_riders]
    y_ref, *second_pass = refs[3 + 2 * n_riders:]
    y2_ref = second_pass[0] if second_pass else None
    _cast_riders(rider_refs, rider_out_refs)
    rows = y_ref.shape[1]
    for chunk in range(xb_ref.shape[0] // rows):
        y = jnp.dot(xb_ref[chunk * rows:(chunk + 1) * rows, :], w_ref[...], preferred_element_type=F32)
        _store_heads_by_residue(y, chunk, o_ref, y_ref, y2_ref, dilation)


class _CastRider(NamedTuple):
    array: jax.Array
    block: tuple
    in_index: Callable
    out_shape: tuple
    out_index: Callable


def _cast_kernel(w_ref, o_ref):
    o_ref[...] = w_ref[...].astype(o_ref.dtype)


def _cast_group_weights(w_in, group, *, rows):
    sections = ATTN_WIDTH // GROUP_WIDTH
    blocks = _nbytes((rows, GROUP_WIDTH), F32) + _nbytes((rows, GROUP_WIDTH), BF16)
    return pl.pallas_call(
        _cast_kernel,
        grid=(3, D_MODEL // rows),
        in_specs=[pl.BlockSpec((rows, GROUP_WIDTH), lambda j, i: (i, j * sections + group))],
        out_specs=pl.BlockSpec((rows, GROUP_WIDTH), lambda j, i: (i, j)),
        out_shape=jax.ShapeDtypeStruct((D_MODEL, 3 * GROUP_WIDTH), BF16),
        compiler_params=pltpu.CompilerParams(
            dimension_semantics=("parallel", "parallel"),
            vmem_limit_bytes=_vmem_limit(blocks, 0)),
        name="cast_group_weights",
    )(w_in)


def _row_block_rider(w, step_of):
    rows = w.shape[0] // CAST_RIDER_STEPS
    assert rows * CAST_RIDER_STEPS == w.shape[0] and rows % 16 == 0

    def index(*idx):
        return jnp.minimum(step_of(*idx), CAST_RIDER_STEPS - 1), 0

    return _CastRider(w, (rows, w.shape[1]), index, w.shape, index)


def _cast_rider_specs(riders):
    in_specs = [pl.BlockSpec(r.block, r.in_index) for r in riders]
    out_specs = [pl.BlockSpec(r.block, r.out_index) for r in riders]
    shapes = [jax.ShapeDtypeStruct(r.out_shape, BF16) for r in riders]
    return in_specs, out_specs, shapes, sum(_nbytes(r.block, F32) * 3 // 2 for r in riders)


def _qkv_proj(x, w_qkv, group, batch, seq, make_riders):
    dilation = DILATED_GROUPS[group][1]
    first = x.dtype == F32
    assert first == (group == 0) and dilation == (1 if first else dilation)
    length = seq // dilation
    rows = PROJ_ROWS if first else min(seq, 2 * PROJ_ROWS)
    tiles = seq // rows
    per_residue = rows // dilation
    assert PROJ_COLS == GROUP_WIDTH and w_qkv.shape == (D_MODEL, 3 * GROUP_WIDTH) and rows * tiles == seq
    row_spec = pl.BlockSpec((rows, D_MODEL), lambda b, t, j: (b * tiles + t, 0))
    qkv_spec = pl.BlockSpec((HEADS_PER_GROUP, None, None, dilation, per_residue, HEAD_DIM),
                            lambda b, t, j: (0, b, j, 0, t, 0))
    qkv_shape = jax.ShapeDtypeStruct((HEADS_PER_GROUP, batch, 3, dilation, length, HEAD_DIM), BF16)
    blocks = (_nbytes((rows, D_MODEL), x.dtype) + _nbytes((D_MODEL, PROJ_COLS), BF16)
              + _nbytes((rows, PROJ_COLS), BF16) + (_nbytes((rows, D_MODEL), BF16) if first else 0))
    passes = 0 if first else (1 if dilation <= MAX_ROW_STRIDE else 2)
    regroup = [pltpu.VMEM((HEADS_PER_GROUP, PROJ_ROWS, HEAD_DIM), F32)] * passes
    w_spec = pl.BlockSpec((D_MODEL, PROJ_COLS), lambda b, t, j: (0, j))
    riders = make_riders(lambda b, t, j: (b * tiles + t) * 3 + j, lambda b, t, j: b * tiles + t,
                         batch * tiles)
    rider_in_specs, rider_specs, rider_shapes, rider_bytes = _cast_rider_specs(riders)
    own_out_specs, own_out_shapes = [qkv_spec], [qkv_shape]
    if first:
        own_out_specs.append(row_spec)
        own_out_shapes.append(jax.ShapeDtypeStruct(x.shape, BF16))
        body = functools.partial(_qkv_proj_first_kernel, n_riders=len(riders))
    else:
        body = functools.partial(_qkv_proj_kernel, n_riders=len(riders), dilation=dilation)
    out = pl.pallas_call(
        body,
        grid=(batch, tiles, 3),
        in_specs=[row_spec, w_spec] + rider_in_specs,
        out_specs=own_out_specs + rider_specs,
        out_shape=own_out_shapes + rider_shapes,
        scratch_shapes=regroup,
        compiler_params=pltpu.CompilerParams(
            dimension_semantics=("arbitrary", "arbitrary", "arbitrary"),
            vmem_limit_bytes=_vmem_limit(blocks + rider_bytes,
                                         (0 if first else 2 * _nbytes((PROJ_ROWS, PROJ_COLS), F32))
                                         + 2 * _nbytes((rows, PROJ_COLS), F32))),
        name=f"qkv_proj_dilation{dilation}",
    )(x, w_qkv, *[r.array for r in riders])
    return (out[0].reshape(HEADS_PER_GROUP, batch, 3, seq, HEAD_DIM),) + tuple(out[1:])


def _conv_proj_kernel(*refs, n_riders):
    x_ref, wu_ref, wc_ref, wb_ref, cw_ref = refs[:5]
    o_ref = refs[5 + n_riders]
    _cast_riders(refs[5:5 + n_riders], refs[6 + n_riders:])
    cw = cw_ref[...]
    part = x_ref.shape[0] // CONV_ROW_PARTS
    row = lax.broadcasted_iota(jnp.int32, (part, o_ref.shape[1]), 0)
    before = None
    for k in range(CONV_ROW_PARTS):
        rows = slice(k * part, (k + 1) * part)
        x = x_ref[rows, :]
        z = (jnp.dot(x, wc_ref[...], preferred_element_type=F32)
             * jnp.dot(x, wu_ref[...], preferred_element_type=F32))
        y = cw[0:1, :] * z
        for tap in range(1, CONV_K):
            lead = 0.0 if before is None else pltpu.roll(before, tap, axis=0)
            shifted = jnp.where(row >= tap, pltpu.roll(z, tap, axis=0), lead)
            y = y + cw[tap:tap + 1, :] * shifted
        o_ref[rows, :] = (jnp.dot(x, wb_ref[...], preferred_element_type=F32) * y).astype(o_ref.dtype)
        before = z


def _conv_proj(xb, w_ucb, conv_w, batch, seq, make_riders, *, tn):
    def w_spec(section):
        return pl.BlockSpec((D_MODEL, tn), lambda b, j: (0, section * (CONV_WIDTH // tn) + j))

    col_blocks = CONV_WIDTH // tn
    assert w_ucb.shape == (D_MODEL, 3 * CONV_WIDTH)
    riders = make_riders(lambda b, j: b * col_blocks + j)
    rider_in_specs, rider_specs, rider_shapes, rider_bytes = _cast_rider_specs(riders)
    blocks = (_nbytes((seq, D_MODEL), BF16) + 3 * _nbytes((D_MODEL, tn), BF16)
              + _nbytes((CONV_K, tn), F32) + _nbytes((seq, tn), BF16) + rider_bytes)
    return pl.pallas_call(
        functools.partial(_conv_proj_kernel, n_riders=len(riders)),
        grid=(batch, col_blocks),
        in_specs=[pl.BlockSpec((seq, D_MODEL), lambda b, j: (b, 0)),
                  w_spec(0), w_spec(1), w_spec(2),
                  pl.BlockSpec((CONV_K, tn), lambda b, j: (0, j))] + rider_in_specs,
        out_specs=[pl.BlockSpec((seq, tn), lambda b, j: (b, j))] + rider_specs,
        out_shape=[jax.ShapeDtypeStruct((batch * seq, CONV_WIDTH), BF16)] + rider_shapes,
        compiler_params=pltpu.CompilerParams(
            dimension_semantics=("arbitrary", "arbitrary"),
            vmem_limit_bytes=_vmem_limit(blocks, 5 * _nbytes((seq, tn), F32))),
        name="conv_proj",
    )(xb, w_ucb, w_ucb, w_ucb, conv_w, *[r.array for r in riders])


def _attn_kernel(qkv1_ref, qkv2_ref, qkv3_ref, o_ref, num_ref, den_ref, max_ref, bias_ref):
    seq = qkv1_ref.shape[1]
    n_tiles = seq // BAND
    scale = HEAD_DIM ** -0.5
    qi = lax.broadcasted_iota(jnp.int32, (BAND, 2 * BAND), 0)
    kj = lax.broadcasted_iota(jnp.int32, (BAND, 2 * BAND), 1)
    visible = ((kj < BAND) & (qi <= kj)) | ((kj >= BAND) & (qi >= kj - BAND))
    bias_ref[...] = jnp.where(visible, 0.0, -jnp.inf)

    def probabilities(q_ref, k_ref, n, has_prev):
        q = q_ref[n * BAND:(n + 1) * BAND, :]
        keys = slice((n - 1) * BAND, (n + 1) * BAND) if has_prev else slice(n * BAND, (n + 1) * BAND)
        bias = bias_ref[...] if has_prev else bias_ref[:, BAND:]
        s = lax.dot_general(q, k_ref[keys, :], (((1,), (1,)), ((), ())),
                            preferred_element_type=F32) * scale + bias
        m = s.max(-1, keepdims=True)
        return jnp.exp(s - m).astype(BF16), m, keys

    assert DILATED_GROUPS[0][1] == 1 and all(d % MAX_ROW_STRIDE == 0 for _, d in DILATED_GROUPS[1:])
    quarter = seq // MAX_ROW_STRIDE

    def result_rows(g, n):
        dilation = DILATED_GROUPS[g][1]
        if dilation == 1:
            return pl.ds(n * BAND, BAND)
        residue, t = divmod(n, n_tiles // dilation)
        step = dilation // MAX_ROW_STRIDE
        start = (residue % MAX_ROW_STRIDE) * quarter + step * t * BAND + residue // MAX_ROW_STRIDE
        return pl.ds(start, BAND, stride=step)

    def finish(g, v_ref, n, p, m, keys):
        rows = result_rows(g, n)
        v = v_ref[keys, :]
        acc = jnp.dot(p, jnp.concatenate([v, jnp.ones_like(v)], axis=1), preferred_element_type=F32)
        num_ref[g, rows, :] = acc[:, :HEAD_DIM]
        den_ref[g, rows, :] = acc[:, HEAD_DIM:]
        max_ref[g, rows, :] = jnp.broadcast_to(m, (BAND, HEAD_DIM))

    groups = [(r.at[0], r.at[1], r.at[2]) for r in (qkv1_ref, qkv2_ref, qkv3_ref)]
    in_flight = []
    for g, (q_ref, k_ref, v_ref) in enumerate(groups):
        tiles_per_residue = n_tiles // DILATED_GROUPS[g][1]
        for n in range(n_tiles):
            has_prev = n % tiles_per_residue > 0
            in_flight.append((g, v_ref, n) + probabilities(q_ref, k_ref, n, has_prev))
            if len(in_flight) > ATTN_TILES_IN_FLIGHT:
                finish(*in_flight.pop(0))
    for pending in in_flight:
        finish(*pending)

    chunks_per_quarter = quarter // BAND

    def mix(c, _):
        mixed = pl.ds(pl.multiple_of(c * BAND, BAND), BAND)
        natural = pl.ds(MAX_ROW_STRIDE * (c % chunks_per_quarter) * BAND + c // chunks_per_quarter,
                        BAND, stride=MAX_ROW_STRIDE)
        rows = [natural] + [mixed] * (N_GROUPS - 1)
        m = [max_ref[g, rows[g], :] for g in range(N_GROUPS)]
        top = jnp.maximum(jnp.maximum(m[0], m[1]), m[2])
        w = [jnp.exp(x - top) for x in m]
        num = sum(w[g] * num_ref[g, rows[g], :] for g in range(N_GROUPS))
        den = sum(w[g] * den_ref[g, rows[g], :] for g in range(N_GROUPS))
        max_ref[0, natural, :] = num / den
        return 0

    lax.fori_loop(0, n_tiles, mix, 0)
    o_ref[...] = max_ref[0].astype(o_ref.dtype)


def _attention(qkv, batch, seq):
    heads = HEADS_PER_GROUP
    qkv_spec = pl.BlockSpec((None, None, 3, seq, HEAD_DIM), lambda b, h: (h, b, 0, 0, 0))
    scratch = 3 * _nbytes((N_GROUPS, seq, HEAD_DIM), F32)
    return pl.pallas_call(
        _attn_kernel,
        grid=(batch, heads),
        in_specs=[qkv_spec] * N_GROUPS,
        out_specs=pl.BlockSpec((seq, HEAD_DIM), lambda b, h: (b, h)),
        out_shape=jax.ShapeDtypeStruct((batch * seq, GROUP_WIDTH), BF16),
        scratch_shapes=[pltpu.VMEM((N_GROUPS, seq, HEAD_DIM), F32),
                        pltpu.VMEM((N_GROUPS, seq, HEAD_DIM), F32),
                        pltpu.VMEM((N_GROUPS, seq, HEAD_DIM), F32),
                        pltpu.VMEM((BAND, 2 * BAND), F32)],
        compiler_params=pltpu.CompilerParams(
            dimension_semantics=("parallel", "parallel"),
            vmem_limit_bytes=_vmem_limit(10 * _nbytes((seq, HEAD_DIM), BF16), scratch + 4 * 1024 * 1024)),
        name="dilated_attention",
    )(*qkv)


def _merge_kernel(*refs, n_riders):
    attn_ref, conv_ref, x_ref, wao_ref, wco_ref, wga_ref, wgc_ref = refs[:7]
    o_ref = refs[7 + n_riders]
    _cast_riders(refs[7:7 + n_riders], refs[8 + n_riders:])
    part = o_ref.shape[0] // MERGE_ROW_PARTS
    for k in range(MERGE_ROW_PARTS):
        rows = slice(k * part, (k + 1) * part)
        x = x_ref[rows, :]
        a = jnp.dot(attn_ref[rows, :], wao_ref[...], preferred_element_type=F32)
        ga = _sigmoid(jnp.dot(x, wga_ref[...], preferred_element_type=F32))
        c = jnp.dot(conv_ref[rows, :], wco_ref[...], preferred_element_type=F32)
        gc = _sigmoid(jnp.dot(x, wgc_ref[...], preferred_element_type=F32))
        o_ref[rows, :] = (ga * a + gc * c).astype(o_ref.dtype)


def _gated_merge(attn, conv, xb, w_attn_o, w_conv_o, w_gates, make_riders, *, tm, tn):
    m = attn.shape[0]
    assert w_gates.shape == (D_MODEL, 2 * D_MODEL)
    col_blocks = D_MODEL // tn
    riders = make_riders(lambda i, j: i * col_blocks + j)
    rider_in_specs, rider_specs, rider_shapes, rider_bytes = _cast_rider_specs(riders)
    blocks = (_nbytes((tm, GROUP_WIDTH), BF16) + _nbytes((tm, CONV_WIDTH), BF16)
              + _nbytes((tm, D_MODEL), BF16) + _nbytes((tm, tn), BF16)
              + _nbytes((GROUP_WIDTH + CONV_WIDTH + 2 * D_MODEL, tn), BF16) + rider_bytes)
    return pl.pallas_call(
        functools.partial(_merge_kernel, n_riders=len(riders)),
        grid=(m // tm, col_blocks),
        in_specs=[pl.BlockSpec((tm, GROUP_WIDTH), lambda i, j: (i, 0)),
                  pl.BlockSpec((tm, CONV_WIDTH), lambda i, j: (i, 0)),
                  pl.BlockSpec((tm, D_MODEL), lambda i, j: (i, 0)),
                  pl.BlockSpec((GROUP_WIDTH, tn), lambda i, j: (0, j)),
                  pl.BlockSpec((CONV_WIDTH, tn), lambda i, j: (0, j)),
                  pl.BlockSpec((D_MODEL, tn), lambda i, j: (0, j)),
                  pl.BlockSpec((D_MODEL, tn), lambda i, j: (0, col_blocks + j))] + rider_in_specs,
        out_specs=[pl.BlockSpec((tm, tn), lambda i, j: (i, j))] + rider_specs,
        out_shape=[jax.ShapeDtypeStruct((m, D_MODEL), BF16)] + rider_shapes,
        compiler_params=pltpu.CompilerParams(
            dimension_semantics=("arbitrary", "arbitrary"),
            vmem_limit_bytes=_vmem_limit(blocks, 5 * _nbytes((tm, tn), F32))),
        name="gated_merge",
    )(attn, conv, xb, w_attn_o, w_conv_o, w_gates, w_gates, *[r.array for r in riders])


def _layer_norm(z, g, b):
    mu = jnp.mean(z, axis=-1, keepdims=True)
    zc = z - mu
    var = jnp.mean(zc * zc, axis=-1, keepdims=True)
    return zc * lax.rsqrt(var + LN_EPS) * g + b


def _out_ln_kernel(mi_ref, w_ref, x_ref, g_ref, b_ref, o_ref, ob_ref):
    part = mi_ref.shape[0] // OUT_LN_SPLIT
    for rows in (slice(k * part, (k + 1) * part) for k in range(OUT_LN_SPLIT)):
        y = jnp.dot(mi_ref[rows, :], w_ref[...], preferred_element_type=F32)
        h = _layer_norm(ALPHA * x_ref[rows, :] + y, g_ref[...], b_ref[...])
        o_ref[rows, :] = h
        ob_ref[rows, :] = h.astype(ob_ref.dtype)


def _out_proj_ln(merged, w_out, x, g, b, *, tm):
    m = merged.shape[0]
    blocks = (2 * _nbytes((tm, D_MODEL), BF16) + _nbytes((D_MODEL, D_MODEL), BF16)
              + 2 * _nbytes((tm, D_MODEL), F32))
    row = pl.BlockSpec((tm, D_MODEL), lambda i: (i, 0))
    vec = pl.BlockSpec((1, D_MODEL), lambda i: (0, 0))
    return pl.pallas_call(
        _out_ln_kernel,
        grid=(m // tm,),
        in_specs=[row, pl.BlockSpec((D_MODEL, D_MODEL), lambda i: (0, 0)), row, vec, vec],
        out_specs=[row, row],
        out_shape=[jax.ShapeDtypeStruct((m, D_MODEL), F32), jax.ShapeDtypeStruct((m, D_MODEL), BF16)],
        compiler_params=pltpu.CompilerParams(
            dimension_semantics=("parallel",),
            vmem_limit_bytes=_vmem_limit(blocks, 3 * _nbytes((tm, D_MODEL), F32))),
        name="out_proj_ln",
    )(merged, w_out, x, g, b)


def _ffn_kernel(hb_ref, h_hbm_ref, wg_ref, wu_ref, wd_ref, g_ref, b_ref, o_ref, res_ref, res_sem):
    i, f = pl.program_id(0), pl.program_id(1)
    tm = o_ref.shape[0]
    residual_copy = pltpu.make_async_copy(
        h_hbm_ref.at[pl.ds(pl.multiple_of(i * tm, tm), tm), :], res_ref, res_sem)

    @pl.when(f == 0)
    def _():
        residual_copy.start()
        o_ref[...] = jnp.zeros_like(o_ref)

    part = tm // FFN_ROW_PARTS
    for k in range(FFN_ROW_PARTS):
        rows = slice(k * part, (k + 1) * part)
        hb = hb_ref[rows, :]
        gate = jnp.dot(hb, wg_ref[...], preferred_element_type=F32)
        up = jnp.dot(hb, wu_ref[...], preferred_element_type=F32)
        hidden = (gate * _sigmoid(gate) * up).astype(BF16)
        o_ref[rows, :] += jnp.dot(hidden, wd_ref[...], preferred_element_type=F32)

    @pl.when(f == pl.num_programs(1) - 1)
    def _():
        residual_copy.wait()
        o_ref[...] = _layer_norm(ALPHA * res_ref[...] + o_ref[...], g_ref[...], b_ref[...])


def _ffn_ln(h, hb, w_gate, w_up, w_down, g, b, *, tm, tf):
    m = h.shape[0]
    blocks = (_nbytes((tm, D_MODEL), BF16) + _nbytes((tm, D_MODEL), F32)
              + 2 * _nbytes((D_MODEL, tf), BF16) + _nbytes((tf, D_MODEL), BF16))
    scratch = _nbytes((tm, D_MODEL), F32)
    row = pl.BlockSpec((tm, D_MODEL), lambda i, f: (i, 0))
    vec = pl.BlockSpec((1, D_MODEL), lambda i, f: (0, 0))
    return pl.pallas_call(
        _ffn_kernel,
        grid=(m // tm, FFN_HIDDEN // tf),
        in_specs=[row,
                  pl.BlockSpec(memory_space=pl.ANY),
                  pl.BlockSpec((D_MODEL, tf), lambda i, f: (0, f)),
                  pl.BlockSpec((D_MODEL, tf), lambda i, f: (0, f)),
                  pl.BlockSpec((tf, D_MODEL), lambda i, f: (f, 0)),
                  vec, vec],
        out_specs=row,
        out_shape=jax.ShapeDtypeStruct((m, D_MODEL), F32),
        scratch_shapes=[pltpu.VMEM((tm, D_MODEL), F32), pltpu.SemaphoreType.DMA],
        compiler_params=pltpu.CompilerParams(
            dimension_semantics=("arbitrary", "arbitrary"),
            vmem_limit_bytes=_vmem_limit(blocks, scratch + 6 * _nbytes((tm, tf), F32))),
        name="ffn_ln",
    )(hb, h, w_gate, w_up, w_down, g, b)


def kernel(x, w_in, conv_w, w_attn_o, w_conv_o, w_out, ln1_g, ln1_b,
           w_ffn_gate, w_ffn_up, w_ffn_down, ln2_g, ln2_b):
    batch, seq, d = x.shape
    assert d == D_MODEL and w_in.shape == (DEPTH, D_MODEL, IN_COLS)
    assert seq % PROJ_ROWS == 0 and all(PROJ_ROWS % (dil * 16) == 0 for _, dil in DILATED_GROUPS)
    assert all(seq % (dil * BAND) == 0 for _, dil in DILATED_GROUPS)
    h = x.reshape(batch * seq, d)
    for layer in range(DEPTH):
        assert N_GROUPS == 3
        w_in_l = w_in[layer]
        sections = ATTN_WIDTH // PROJ_COLS
        w_qkv0_b = _cast_group_weights(w_in_l, 0, rows=512)

        def group_weight_rider(group, row_tile_of, n_row_tiles):
            return _CastRider(w_in_l, (D_MODEL // n_row_tiles, PROJ_COLS),
                              lambda b, t, j: (row_tile_of(b, t, j), j * sections + group),
                              (D_MODEL, 3 * GROUP_WIDTH), lambda b, t, j: (row_tile_of(b, t, j), j))

        def conv_weight_rider(row_tile_of, n_row_tiles):
            half = 3 * CONV_WIDTH // 2
            return _CastRider(w_in_l, (D_MODEL // n_row_tiles, half),
                              lambda b, t, j: (row_tile_of(b, t, j), COL_U // half + jnp.minimum(j, 1)),
                              (D_MODEL, 3 * CONV_WIDTH), lambda b, t, j: (row_tile_of(b, t, j), jnp.minimum(j, 1)))

        def gate_weight_rider(tn):
            return _CastRider(w_in_l, (D_MODEL // batch, 2 * D_MODEL // (CONV_WIDTH // tn)),
                              lambda b, j: (b, COL_GA // (2 * D_MODEL // (CONV_WIDTH // tn)) + j),
                              (D_MODEL, 2 * D_MODEL), lambda b, j: (b, j))

        qkv0, hb, w_attn_o_b, w_conv_o_b, w_qkv1_b = _qkv_proj(
            h, w_qkv0_b, 0, batch, seq,
            lambda step_of, row_tile_of, n_row_tiles: [
                _row_block_rider(w_attn_o[layer], step_of), _row_block_rider(w_conv_o[layer], step_of),
                group_weight_rider(1, row_tile_of, n_row_tiles)])
        qkv1, w_qkv2_b, w_ucb_b = _qkv_proj(
            hb, w_qkv1_b, 1, batch, seq,
            lambda step_of, row_tile_of, n_row_tiles: [group_weight_rider(2, row_tile_of, n_row_tiles),
                                                       conv_weight_rider(row_tile_of, n_row_tiles)])
        qkv2, = _qkv_proj(hb, w_qkv2_b, 2, batch, seq, lambda step_of, row_tile_of, n_row_tiles: [])
        conv_tn = 512
        conv, w_down_b, w_out_b, w_gates_b = _conv_proj(
            hb, w_ucb_b, conv_w[layer], batch, seq,
            lambda step_of: [_row_block_rider(w_ffn_down[layer], step_of),
                             _row_block_rider(w_out[layer], step_of),
                             gate_weight_rider(conv_tn)],
            tn=conv_tn)
        attn = _attention([qkv0, qkv1, qkv2], batch, seq)
        merged, w_gate_b, w_up_b = _gated_merge(
            attn, conv, hb, w_attn_o_b, w_conv_o_b, w_gates_b,
            lambda step_of: [_row_block_rider(w_ffn_gate[layer], step_of),
                             _row_block_rider(w_ffn_up[layer], step_of)],
            tm=1024, tn=512)
        h, hb = _out_proj_ln(merged, w_out_b, h, ln1_g[layer][None, :], ln1_b[layer][None, :], tm=512)
        h = _ffn_ln(h, hb, w_gate_b, w_up_b, w_down_b, ln2_g[layer][None, :], ln2_b[layer][None, :],
                    tm=1024, tf=512)
    return h.reshape(batch, seq, d)
```
